```python
import math
import jax, jax.numpy as jnp
from jax import lax
import numpy as np

D_MODEL = 1024
BATCH = 8
SEQ = 4096
DEPTH = 4

N_EVEN = (DEPTH + 1) // 2
N_ODD = DEPTH // 2
W_A = D_MODEL // 2
CONV_A = 3
W_B = D_MODEL // 2
SG_HEADS = 8
SG_HEAD_DIM = W_B // SG_HEADS
CHUNK = 128
EVEN_IN = 3 * W_A + 2 * W_B
EVEN_OUT = W_A + W_B
W_C = D_MODEL // 2
CONV_C = 31
MLA_HEADS = 8
QK_NOPE = 64
QK_ROPE = 32
V_DIM = 64
Q_RANK = 256
KV_RANK = 128
ROPE_THETA = 10000.0
Q_BLOCK = 128
ODD_IN = 2 * W_C + Q_RANK + KV_RANK + QK_ROPE
ODD_OUT = W_C + MLA_HEADS * V_DIM
N_EXPERTS = 32
TOP_K = 4
D_EXPERT = D_MODEL
SWIGLU_LIMIT = 7.0
SWIGLU_ALPHA = 1.702
EXPERT_BLOCK = 128
DEEPNORM_ALPHA = (2.0 * DEPTH) ** 0.25
DEEPNORM_BETA = (8.0 * DEPTH) ** -0.25
LN_EPS = 1e-5
RMS_EPS = 1e-6

kernel_name = 'hybrid_shortconv_sgmlp_conformer_mla_moe_trunk'


def layer_norm(x, g, b):
    xf = x.astype(jnp.float32)
    mu = jnp.mean(xf, axis=-1, keepdims=True)
    var = jnp.mean(jnp.square(xf - mu), axis=-1, keepdims=True)
    y = (xf - mu) * lax.rsqrt(var + LN_EPS)
    return (y * g.astype(jnp.float32) + b.astype(jnp.float32)).astype(x.dtype)


def rms_norm(x, g):
    xf = x.astype(jnp.float32)
    y = xf * lax.rsqrt(jnp.mean(jnp.square(xf), axis=-1, keepdims=True) + RMS_EPS)
    return (y * g.astype(jnp.float32)).astype(x.dtype)


def causal_dwconv(x, w):
    k = w.shape[0]
    return lax.conv_general_dilated(
        x, w[:, None, :].astype(x.dtype), window_strides=(1,), padding=[(k - 1, 0)],
        dimension_numbers=('NWC', 'WIO', 'NWC'), feature_group_count=x.shape[-1])


def rope_tables(positions):
    inv_freq = ROPE_THETA ** (-jnp.arange(0, QK_ROPE, 2, dtype=jnp.float32) / QK_ROPE)
    ang = positions.astype(jnp.float32)[..., None] * inv_freq
    return jnp.cos(ang), jnp.sin(ang)


def apply_rope(x, cos, sin):
    half = x.shape[-1] // 2
    x1 = x[..., :half].astype(jnp.float32)
    x2 = x[..., half:].astype(jnp.float32)
    return jnp.concatenate([x1 * cos - x2 * sin, x2 * cos + x1 * sin], axis=-1).astype(x.dtype)


def causal_block_attention(q, k, v):
    bsz, seq, heads, dqk = q.shape
    n_blk = seq // Q_BLOCK
    scale = 1.0 / math.sqrt(dqk)
    q_blocks = jnp.moveaxis(q.reshape(bsz, n_blk, Q_BLOCK, heads, dqk), 1, 0)
    k_pos = jnp.arange(seq)

    def attend(args):
        qb, i = args
        s = jnp.einsum('bqhd,bkhd->bhqk', qb, k, preferred_element_type=jnp.float32) * scale
        q_pos = i * Q_BLOCK + jnp.arange(Q_BLOCK)
        s = jnp.where(k_pos[None, :] <= q_pos[:, None], s, -1e30)
        p = jax.nn.softmax(s, axis=-1).astype(v.dtype)
        return jnp.einsum('bhqk,bkhd->bqhd', p, v)

    out = lax.map(attend, (q_blocks, jnp.arange(n_blk)))
    return jnp.moveaxis(out, 0, 1).reshape(bsz, seq, heads, v.shape[-1])


def even_mixer(u, w_in, conv_w, sg_w, sg_b, vn_g, vn_b, w_out):
    bsz, seq, _ = u.shape
    proj = u @ w_in
    b_gate, c_gate, xa, zu, zv = jnp.split(proj, [W_A, 2 * W_A, 3 * W_A, 3 * W_A + W_B], axis=-1)
    y_a = b_gate * causal_dwconv(c_gate * xa, conv_w)
    zu = jax.nn.gelu(zu, approximate=False)
    zv = layer_norm(jax.nn.gelu(zv, approximate=False), vn_g, vn_b)
    zv = zv.reshape(bsz, seq // CHUNK, CHUNK, SG_HEADS, SG_HEAD_DIM)
    causal = jnp.tril(jnp.ones((CHUNK, CHUNK), dtype=sg_w.dtype))
    mixed = jnp.einsum('hij,bnjhd->bnihd', sg_w * causal, zv) + jnp.swapaxes(sg_b, 0, 1)[:, :, None]
    y_b = zu * mixed.reshape(bsz, seq, W_B)
    return jnp.concatenate([y_a, y_b], axis=-1) @ w_out


def odd_mixer(u, cos, sin, w_in, dw_w, dw_b, cn_g, cn_b, qn_g, w_uq, kvn_g, w_ukv, w_out):
    bsz, seq, _ = u.shape
    proj = u @ w_in
    ga, gb, cq, ckv, k_rope = jnp.split(
        proj, [W_C, 2 * W_C, 2 * W_C + Q_RANK, 2 * W_C + Q_RANK + KV_RANK], axis=-1)
    h = causal_dwconv(ga * jax.nn.sigmoid(gb), dw_w) + dw_b
    y_c = jax.nn.silu(layer_norm(h, cn_g, cn_b))
    q = (rms_norm(cq, qn_g) @ w_uq).reshape(bsz, seq, MLA_HEADS, QK_NOPE + QK_ROPE)
    kv = (rms_norm(ckv, kvn_g) @ w_ukv).reshape(bsz, seq, MLA_HEADS, QK_NOPE + V_DIM)
    q = jnp.concatenate(
        [q[..., :QK_NOPE], apply_rope(q[..., QK_NOPE:], cos[:, :, None], sin[:, :, None])], axis=-1)
    k_rope = apply_rope(k_rope, cos, sin)[:, :, None, :]
    k = jnp.concatenate(
        [kv[..., :QK_NOPE], jnp.broadcast_to(k_rope, (bsz, seq, MLA_HEADS, QK_ROPE))], axis=-1)
    v = kv[..., QK_NOPE:]
    y_d = causal_block_attention(q, k, v).reshape(bsz, seq, MLA_HEADS * V_DIM)
    return jnp.concatenate([y_c, y_d], axis=-1) @ w_out


def clamped_swiglu(xb, w_gu, b_gu, w_dn, b_dn):
    gu = xb @ w_gu + b_gu
    gate, up = gu[..., :D_EXPERT], gu[..., D_EXPERT:]
    gate = jnp.minimum(gate, SWIGLU_LIMIT)
    up = jnp.clip(up, -SWIGLU_LIMIT, SWIGLU_LIMIT)
    glu = gate * jax.nn.sigmoid(SWIGLU_ALPHA * gate)
    return ((up + 1.0) * glu) @ w_dn + b_dn


def moe_ffn(h, router_w, router_b, w_gu, b_gu, w_dn, b_dn):
    bsz, seq, d = h.shape
    n_tok = bsz * seq
    n_assign = n_tok * TOP_K
    xt = h.reshape(n_tok, d)
    logits = (xt @ router_w + router_b).astype(jnp.float32)
    top_val, top_exp = lax.top_k(logits, TOP_K)
    gates = jax.nn.softmax(top_val, axis=-1)
    flat_exp = top_exp.reshape(-1)
    flat_tok = jnp.arange(n_assign, dtype=jnp.int32) // TOP_K
    flat_gate = gates.reshape(-1)
    order = jnp.argsort(flat_exp)
    sorted_exp = flat_exp[order]
    counts = jnp.bincount(flat_exp, length=N_EXPERTS)
    padded = (counts + EXPERT_BLOCK - 1) // EXPERT_BLOCK * EXPERT_BLOCK
    start = jnp.cumsum(counts) - counts
    pad_end = jnp.cumsum(padded)
    pad_start = pad_end - padded
    dest = pad_start[sorted_exp] + jnp.arange(n_assign, dtype=jnp.int32) - start[sorted_exp]
    n_slots = n_assign + N_EXPERTS * EXPERT_BLOCK
    n_blocks = n_slots // EXPERT_BLOCK
    slot_tok = jnp.full((n_slots,), n_tok, jnp.int32).at[dest].set(flat_tok[order])
    slot_gate = jnp.zeros((n_slots,), jnp.float32).at[dest].set(flat_gate[order])
    block_exp = jnp.minimum(
        jnp.searchsorted(pad_end, jnp.arange(n_blocks, dtype=jnp.int32) * EXPERT_BLOCK, side='right'),
        N_EXPERTS - 1)
    x_pad = jnp.concatenate([xt, jnp.zeros((1, d), xt.dtype)], axis=0)
    xs = x_pad[slot_tok].reshape(n_blocks, EXPERT_BLOCK, d)

    def run_block(args):
        xb, e = args
        return clamped_swiglu(xb, w_gu[e], b_gu[e], w_dn[e], b_dn[e])

    ys = lax.map(run_block, (xs, block_exp)).reshape(n_slots, d)
    out = jnp.zeros((n_tok + 1, d), ys.dtype).at[slot_tok].add(ys * slot_gate[:, None].astype(ys.dtype))
    return out[:n_tok].reshape(bsz, seq, d)


def setup_inputs(seed: int = 0) -> dict:
    key = jax.random.key(seed)
    ks = iter(jax.random.split(key, 48))

    def nrm(shape, scale):
        return jax.random.normal(next(ks), shape, jnp.float32) * scale

    def gain(shape):
        return 1.0 + nrm(shape, 0.02)

    d = D_MODEL
    x = nrm((BATCH, SEQ, d), 1.0)
    c = nrm((BATCH, d), 1.0)
    offsets = jax.random.randint(next(ks), (BATCH, 1), 0, 1024, dtype=jnp.int32)
    positions = (offsets + jnp.arange(SEQ, dtype=jnp.int32)[None, :]).astype(jnp.int32)
    return {
        'x': x,
        'c': c,
        'positions': positions,
        'ada_w': nrm((DEPTH, d, 6 * d), 0.5 * d ** -0.5),
        'ada_b': nrm((DEPTH, 6 * d), 0.02),
        'ln_mix_g': gain((DEPTH, d)),
        'ln_mix_b': nrm((DEPTH, d), 0.02),
        'ln_ffn_g': gain((DEPTH, d)),
        'ln_ffn_b': nrm((DEPTH, d), 0.02),
        'ev_w_in': nrm((N_EVEN, d, EVEN_IN), d ** -0.5),
        'ev_conv_w': nrm((N_EVEN, CONV_A, W_A), CONV_A ** -0.5),
        'ev_sg_w': nrm((N_EVEN, SG_HEADS, CHUNK, CHUNK), CHUNK ** -0.5),
        'ev_sg_b': 1.0 + nrm((N_EVEN, SG_HEADS, CHUNK), 0.1),
        'ev_vn_g': gain((N_EVEN, W_B)),
        'ev_vn_b': nrm((N_EVEN, W_B), 0.02),
        'ev_w_out': nrm((N_EVEN, EVEN_OUT, d), EVEN_OUT ** -0.5 * DEEPNORM_BETA),
        'od_w_in': nrm((N_ODD, d, ODD_IN), d ** -0.5),
        'od_dw_w': nrm((N_ODD, CONV_C, W_C), CONV_C ** -0.5),
        'od_dw_b': nrm((N_ODD, W_C), 0.02),
        'od_cn_g': gain((N_ODD, W_C)),
        'od_cn_b': nrm((N_ODD, W_C), 0.02),
        'od_qn_g': gain((N_ODD, Q_RANK)),
        'od_w_uq': nrm((N_ODD, Q_RANK, MLA_HEADS * (QK_NOPE + QK_ROPE)), Q_RANK ** -0.5),
        'od_kvn_g': gain((N_ODD, KV_RANK)),
        'od_w_ukv': nrm((N_ODD, KV_RANK, MLA_HEADS * (QK_NOPE + V_DIM)), KV_RANK ** -0.5),
        'od_w_out': nrm((N_ODD, ODD_OUT, d), ODD_OUT ** -0.5 * DEEPNORM_BETA),
        'moe_router_w': nrm((DEPTH, d, N_EXPERTS), d ** -0.5),
        'moe_router_b': nrm((DEPTH, N_EXPERTS), 0.01),
        'moe_w_gu': nrm((DEPTH, N_EXPERTS, d, 2 * D_EXPERT), d ** -0.5),
        'moe_b_gu': nrm((DEPTH, N_EXPERTS, 2 * D_EXPERT), 0.02),
        'moe_w_dn': nrm((DEPTH, N_EXPERTS, D_EXPERT, d), D_EXPERT ** -0.5 * DEEPNORM_BETA),
        'moe_b_dn': nrm((DEPTH, N_EXPERTS, d), 0.02),
    }


def reference(x, c, positions, ada_w, ada_b, ln_mix_g, ln_mix_b, ln_ffn_g, ln_ffn_b,
              ev_w_in, ev_conv_w, ev_sg_w, ev_sg_b, ev_vn_g, ev_vn_b, ev_w_out,
              od_w_in, od_dw_w, od_dw_b, od_cn_g, od_cn_b, od_qn_g, od_w_uq, od_kvn_g, od_w_ukv, od_w_out,
              moe_router_w, moe_router_b, moe_w_gu, moe_b_gu, moe_w_dn, moe_b_dn):
    cos, sin = rope_tables(positions)
    cond = jax.nn.silu(c)
    for layer in range(DEPTH):
        mod = (cond @ ada_w[layer] + ada_b[layer])[:, None, :]
        shift_m, scale_m, gate_m, shift_f, scale_f, gate_f = jnp.split(mod, 6, axis=-1)
        u = x * (1.0 + scale_m) + shift_m
        i = layer // 2
        if layer % 2 == 0:
            y = even_mixer(u, ev_w_in[i], ev_conv_w[i], ev_sg_w[i], ev_sg_b[i],
                           ev_vn_g[i], ev_vn_b[i], ev_w_out[i])
        else:
            y = odd_mixer(u, cos, sin, od_w_in[i], od_dw_w[i], od_dw_b[i], od_cn_g[i], od_cn_b[i],
                          od_qn_g[i], od_w_uq[i], od_kvn_g[i], od_w_ukv[i], od_w_out[i])
        x = layer_norm(DEEPNORM_ALPHA * x + (1.0 + gate_m) * y, ln_mix_g[layer], ln_mix_b[layer])
        u = x * (1.0 + scale_f) + shift_f
        y = moe_ffn(u, moe_router_w[layer], moe_router_b[layer], moe_w_gu[layer], moe_b_gu[layer],
                    moe_w_dn[layer], moe_b_dn[layer])
        x = layer_norm(DEEPNORM_ALPHA * x + (1.0 + gate_f) * y, ln_ffn_g[layer], ln_ffn_b[layer])
    return x
```

```python
import functools
import math

import jax
import jax.numpy as jnp
from jax import lax
from jax.experimental import pallas as pl
from jax.experimental.pallas import tpu as pltpu

F32 = jnp.float32
BF16 = jnp.bfloat16
HIGHEST = lax.Precision.HIGHEST

D_MODEL = 1024
DEPTH = 4
W_A = 512
W_B = 512
SG_HEADS = 8
SG_HEAD_DIM = 64
CHUNK = 128
CONV_A = 3
W_C = 512
CONV_C = 31
MLA_HEADS = 8
QK_NOPE = 64
QK_ROPE = 32
V_DIM = 64
Q_RANK = 256
KV_RANK = 128
ROPE_THETA = 10000.0
N_EXPERTS = 32
TOP_K = 4
D_EXPERT = 1024
SWIGLU_LIMIT = 7.0
SWIGLU_ALPHA = 1.702
DEEPNORM_ALPHA = (2.0 * DEPTH) ** 0.25
LN_EPS = 1e-5
RMS_EPS = 1e-6

LANES = 128
HEAD_PAD = 128
CONV_C_HALO = 32
CONV_A_HALO = 8

T_EVEN = 512
T_ODD = 256
T_POST = 512
T_ATT = 512
T_ROUTE = 512
T_DISP = 256
T_COMB = 256
BM = 512
VMEM_LIMIT = 56 * 1024 * 1024


def _cparams(sem):
    return pltpu.CompilerParams(dimension_semantics=sem, vmem_limit_bytes=VMEM_LIMIT)


def _layer_norm(v, g, b):
    mu = jnp.mean(v, axis=-1, keepdims=True)
    d = v - mu
    var = jnp.mean(d * d, axis=-1, keepdims=True)
    return d * lax.rsqrt(var + LN_EPS) * g + b


def _gelu(v):
    return 0.5 * v * (1.0 + lax.erf(v * (1.0 / math.sqrt(2.0))))


def _sigmoid(v):
    return 1.0 / (1.0 + jnp.exp(-v))


def _ada_kernel(c_ref, w_ref, b_ref, o_ref):
    c = c_ref[...]
    cond = c * _sigmoid(c)
    o_ref[0] = jnp.dot(cond, w_ref[0], precision=HIGHEST, preferred_element_type=F32) + b_ref[0]


def _ada(c, ada_w, ada_b):
    bsz, d = c.shape
    depth = ada_w.shape[0]
    n_chunk = ada_w.shape[2] // d
    return pl.pallas_call(
        _ada_kernel,
        grid=(depth, n_chunk),
        in_specs=[
            pl.BlockSpec((bsz, d), lambda l, j: (0, 0)),
            pl.BlockSpec((1, d, d), lambda l, j: (l, 0, j)),
            pl.BlockSpec((1, 1, d), lambda l, j: (l, 0, j)),
        ],
        out_specs=pl.BlockSpec((1, bsz, d), lambda l, j: (l, 0, j)),
        out_shape=jax.ShapeDtypeStruct((depth, bsz, n_chunk * d), F32),
        compiler_params=_cparams(("arbitrary", "arbitrary")),
        name="ada",
    )(c, ada_w, ada_b.reshape(depth, 1, n_chunk * d))


def _rope_kernel(pos_ref, freq_ref, cos_ref, sin_ref):
    ang = pos_ref[...].astype(F32) * freq_ref[...]
    cos_ref[...] = jnp.cos(ang)
    sin_ref[...] = jnp.sin(ang)


def _rope_tables(positions):
    n = positions.size
    t = 1024
    inv_freq = ROPE_THETA ** (-jnp.arange(0, QK_ROPE, 2, dtype=F32) / QK_ROPE)
    half = QK_ROPE // 2
    freq = jnp.zeros((1, LANES), F32).at[0, QK_NOPE:QK_NOPE + QK_ROPE].set(jnp.tile(inv_freq, 2))
    del half
    return pl.pallas_call(
        _rope_kernel,
        grid=(n // t,),
        in_specs=[pl.BlockSpec((t, 1), lambda i: (i, 0)), pl.BlockSpec((1, LANES), lambda i: (0, 0))],
        out_specs=[pl.BlockSpec((t, LANES), lambda i: (i, 0))] * 2,
        out_shape=[jax.ShapeDtypeStruct((n, LANES), F32)] * 2,
        compiler_params=_cparams(("arbitrary",)),
        name="rope",
    )(positions.reshape(n, 1), freq)


def _even_kernel(x_ref, mod_ref, win_ref, cw_ref, sgw_ref, sgb_ref, vng_ref, vnb_ref, wout_ref,
                 lng_ref, lnb_ref, o_ref, gbuf, ybuf):
    t_len = x_ref.shape[1]
    halo = CONV_A_HALO

    @pl.when(pl.program_id(1) == 0)
    def _():
        gbuf[0:halo, :] = jnp.zeros((halo, W_A), F32)

    x = x_ref[0]
    shift, scale, gate = mod_ref[0, 0:1, :], mod_ref[0, 1:2, :], mod_ref[0, 2:3, :]
    u = x * (1.0 + scale) + shift
    proj = jnp.dot(u.astype(BF16), win_ref[...], preferred_element_type=F32)
    b_gate = proj[:, 0:W_A]
    c_gate = proj[:, W_A:2 * W_A]
    xa = proj[:, 2 * W_A:3 * W_A]
    zu = proj[:, 3 * W_A:3 * W_A + W_B]
    zv = proj[:, 3 * W_A + W_B:3 * W_A + 2 * W_B]

    g = c_gate * xa
    gbuf[halo:halo + t_len, :] = g
    conv = (cw_ref[0:1, :] * gbuf[halo - 2:halo - 2 + t_len, :]
            + cw_ref[1:2, :] * gbuf[halo - 1:halo - 1 + t_len, :]
            + cw_ref[2:3, :] * g)
    gbuf[0:halo, :] = g[t_len - halo:t_len, :]
    ybuf[:, 0:W_A] = (b_gate * conv).astype(BF16)

    zu = _gelu(zu)
    zv = _layer_norm(_gelu(zv), vng_ref[...], vnb_ref[...]).astype(BF16)
    row = lax.broadcasted_iota(jnp.int32, (CHUNK, CHUNK), 0)
    col = lax.broadcasted_iota(jnp.int32, (CHUNK, CHUNK), 1)
    w_stack = jnp.concatenate(
        [jnp.where(row >= col, sgw_ref[h], 0.0).astype(BF16) for h in range(SG_HEADS)], axis=0)
    col_head = lax.broadcasted_iota(jnp.int32, (CHUNK, W_B), 1) // SG_HEAD_DIM
    for ci in range(t_len // CHUNK):
        lo = ci * CHUNK
        full = jnp.dot(w_stack, zv[lo:lo + CHUNK, :], preferred_element_type=F32)
        mixed = sgb_ref[...]
        for h in range(SG_HEADS):
            mixed = mixed + jnp.where(col_head == h, full[h * CHUNK:(h + 1) * CHUNK, :], 0.0)
        ybuf[lo:lo + CHUNK, W_A:W_A + W_B] = (zu[lo:lo + CHUNK, :] * mixed).astype(BF16)

    y = jnp.dot(ybuf[...], wout_ref[...], preferred_element_type=F32)
    r = DEEPNORM_ALPHA * x + (1.0 + gate) * y
    o_ref[0] = _layer_norm(r, lng_ref[...], lnb_ref[...])


def _even_layer(x, mod_l, w_in, conv_w, sg_w, sg_b, vn_g, vn_b, w_out, ln_g, ln_b):
    bsz, seq, d = x.shape
    t = min(T_EVEN, seq)
    sgb_full = jnp.repeat(sg_b.T, SG_HEAD_DIM, axis=1)
    full = lambda shape: pl.BlockSpec(shape, lambda b, i: (0,) * len(shape))
    return pl.pallas_call(
        _even_kernel,
        grid=(bsz, seq // t),
        in_specs=[
            pl.BlockSpec((1, t, d), lambda b, i: (b, i, 0)),
            pl.BlockSpec((1, 6, d), lambda b, i: (b, 0, 0)),
            full(w_in.shape), full(conv_w.shape), full(sg_w.shape), full(sgb_full.shape),
            full((1, W_B)), full((1, W_B)), full(w_out.shape), full((1, d)), full((1, d)),
        ],
        out_specs=pl.BlockSpec((1, t, d), lambda b, i: (b, i, 0)),
        out_shape=jax.ShapeDtypeStruct((bsz, seq, d), F32),
        scratch_shapes=[pltpu.VMEM((CONV_A_HALO + t, W_A), F32), pltpu.VMEM((t, W_A + W_B), BF16)],
        compiler_params=_cparams(("arbitrary", "arbitrary")),
        name="even",
    )(x, mod_l, w_in.astype(BF16), conv_w, sg_w, sgb_full, vn_g.reshape(1, W_B), vn_b.reshape(1, W_B),
      w_out.astype(BF16), ln_g.reshape(1, d), ln_b.reshape(1, d))


N_GLU = 2 * W_C
OFF_Q = N_GLU
OFF_KV = OFF_Q + Q_RANK
OFF_KR = OFF_KV + KV_RANK
OFF_KR_ROT = OFF_KR + HEAD_PAD
ODD_IN_PAD = OFF_KR_ROT + HEAD_PAD


def _odd_pre_kernel(x_ref, mod_ref, cos_ref, sin_ref, win_ref, dww_ref, dwb_ref, cng_ref, cnb_ref,
                    qng_ref, wq_ref, wqr_ref, kvng_ref, wk_ref, wv_ref,
                    yc_ref, q_ref, k_ref, v_ref, gbuf):
    t_len = x_ref.shape[1]
    halo = CONV_C_HALO

    @pl.when(pl.program_id(1) == 0)
    def _():
        gbuf[0:halo, :] = jnp.zeros((halo, W_C), F32)

    x = x_ref[0]
    shift, scale = mod_ref[0, 0:1, :], mod_ref[0, 1:2, :]
    u = x * (1.0 + scale) + shift
    proj = jnp.dot(u.astype(BF16), win_ref[...], preferred_element_type=F32)

    g = proj[:, 0:W_C] * _sigmoid(proj[:, W_C:2 * W_C])
    gbuf[halo:halo + t_len, :] = g
    acc = dwb_ref[...] + dww_ref[CONV_C - 1:CONV_C, :] * g
    for tap in range(CONV_C - 1):
        lo = halo - (CONV_C - 1) + tap
        acc = acc + dww_ref[tap:tap + 1, :] * gbuf[lo:lo + t_len, :]
    gbuf[0:halo, :] = g[t_len - halo:t_len, :]
    hn = _layer_norm(acc, cng_ref[...], cnb_ref[...])
    yc_ref[0] = (hn * _sigmoid(hn)).astype(BF16)

    cos = cos_ref[...]
    sin = sin_ref[...]
    cq = proj[:, OFF_Q:OFF_Q + Q_RANK]
    q_lat = (cq * lax.rsqrt(jnp.mean(cq * cq, axis=-1, keepdims=True) + RMS_EPS) * qng_ref[...]).astype(BF16)
    ckv = proj[:, OFF_KV:OFF_KV + KV_RANK]
    kv_lat = (ckv * lax.rsqrt(jnp.mean(ckv * ckv, axis=-1, keepdims=True) + RMS_EPS) * kvng_ref[...]).astype(BF16)
    q_all = jnp.dot(q_lat, wq_ref[...], preferred_element_type=F32)
    q_rot = jnp.dot(q_lat, wqr_ref[...], preferred_element_type=F32)
    k_all = jnp.dot(kv_lat, wk_ref[...], preferred_element_type=F32)
    v_all = jnp.dot(kv_lat, wv_ref[...], preferred_element_type=F32)
    k_rope = proj[:, OFF_KR:OFF_KR + HEAD_PAD] * cos + proj[:, OFF_KR_ROT:OFF_KR_ROT + HEAD_PAD] * sin
    sm_scale = 1.0 / math.sqrt(QK_NOPE + QK_ROPE)
    for h in range(MLA_HEADS):
        sl = slice(h * HEAD_PAD, (h + 1) * HEAD_PAD)
        q_ref[0, h] = ((q_all[:, sl] * cos + q_rot[:, sl] * sin) * sm_scale).astype(BF16)
        k_ref[0, h] = (k_all[:, sl] + k_rope).astype(BF16)
    for hp in range(MLA_HEADS // 2):
        v_ref[0, hp] = v_all[:, hp * LANES:(hp + 1) * LANES].astype(BF16)


def _odd_weights(w_in, w_uq, w_ukv):
    d = w_in.shape[0]
    half = QK_ROPE // 2
    kr = w_in[:, OFF_KR:OFF_KR + QK_ROPE]
    z = lambda n: jnp.zeros((d, n), w_in.dtype)
    kr_blk = jnp.concatenate([z(QK_NOPE), kr, z(HEAD_PAD - QK_NOPE - QK_ROPE)], axis=1)
    kr_rot = jnp.concatenate([z(QK_NOPE), -kr[:, half:], kr[:, :half], z(HEAD_PAD - QK_NOPE - QK_ROPE)], axis=1)
    w_in_p = jnp.concatenate([w_in[:, :OFF_KR], kr_blk, kr_rot], axis=1).astype(BF16)

    dq = QK_NOPE + QK_ROPE
    wq = w_uq.reshape(Q_RANK, MLA_HEADS, dq)
    zq = lambda n: jnp.zeros((Q_RANK, MLA_HEADS, n), w_uq.dtype)
    wq_p = jnp.concatenate([wq, zq(HEAD_PAD - dq)], axis=2).reshape(Q_RANK, MLA_HEADS * HEAD_PAD)
    wq_r = jnp.concatenate([zq(QK_NOPE), -wq[:, :, QK_NOPE + half:], wq[:, :, QK_NOPE:QK_NOPE + half],
                            zq(HEAD_PAD - dq)], axis=2).reshape(Q_RANK, MLA_HEADS * HEAD_PAD)

    wkv = w_ukv.reshape(KV_RANK, MLA_HEADS, QK_NOPE + V_DIM)
    wk_p = jnp.concatenate([wkv[:, :, :QK_NOPE], jnp.zeros((KV_RANK, MLA_HEADS, HEAD_PAD - QK_NOPE), w_ukv.dtype)],
                           axis=2).reshape(KV_RANK, MLA_HEADS * HEAD_PAD)
    wv_p = wkv[:, :, QK_NOPE:].reshape(KV_RANK, MLA_HEADS * V_DIM)
    return w_in_p, wq_p.astype(BF16), wq_r.astype(BF16), wk_p.astype(BF16), wv_p.astype(BF16)


def _odd_pre(x, mod_l, cos_t, sin_t, w_in, dw_w, dw_b, cn_g, cn_b, qn_g, w_uq, kvn_g, w_ukv):
    bsz, seq, d = x.shape
    t = min(T_ODD, seq)
    nt = seq // t
    w_in_p, wq_p, wq_r, wk_p, wv_p = _odd_weights(w_in, w_uq, w_ukv)
    full = lambda shape: pl.BlockSpec(shape, lambda b, i: (0,) * len(shape))
    return pl.pallas_call(
        _odd_pre_kernel,
        grid=(bsz, nt),
        in_specs=[
            pl.BlockSpec((1, t, d), lambda b, i: (b, i, 0)),
            pl.BlockSpec((1, 6, d), lambda b, i: (b, 0, 0)),
            pl.BlockSpec((t, LANES), lambda b, i: (b * nt + i, 0)),
            pl.BlockSpec((t, LANES), lambda b, i: (b * nt + i, 0)),
            full(w_in_p.shape), full(dw_w.shape), full((1, W_C)), full((1, W_C)), full((1, W_C)),
            full((1, Q_RANK)), full(wq_p.shape), full(wq_r.shape), full((1, KV_RANK)),
            full(wk_p.shape), full(wv_p.shape),
        ],
        out_specs=[
            pl.BlockSpec((1, t, W_C), lambda b, i: (b, i, 0)),
            pl.BlockSpec((1, MLA_HEADS, t, HEAD_PAD), lambda b, i: (b, 0, i, 0)),
            pl.BlockSpec((1, MLA_HEADS, t, HEAD_PAD), lambda b, i: (b, 0, i, 0)),
            pl.BlockSpec((1, MLA_HEADS // 2, t, LANES), lambda b, i: (b, 0, i, 0)),
        ],
        out_shape=[
            jax.ShapeDtypeStruct((bsz, seq, W_C), BF16),
            jax.ShapeDtypeStruct((bsz, MLA_HEADS, seq, HEAD_PAD), BF16),
            jax.ShapeDtypeStruct((bsz, MLA_HEADS, seq, HEAD_PAD), BF16),
            jax.ShapeDtypeStruct((bsz, MLA_HEADS // 2, seq, LANES), BF16),
        ],
        scratch_shapes=[pltpu.VMEM((CONV_C_HALO + t, W_C), F32)],
        compiler_params=_cparams(("arbitrary", "arbitrary")),
        name="odd_pre",
    )(x, mod_l, cos_t, sin_t, w_in_p, dw_w, dw_b.reshape(1, W_C), cn_g.reshape(1, W_C), cn_b.reshape(1, W_C),
      qn_g.reshape(1, Q_RANK), wq_p, wq_r, kvn_g.reshape(1, KV_RANK), wk_p, wv_p)


def _attn_kernel(q_ref, k_ref, v_ref, o_ref, m_sc, l_sc, acc_sc):
    qi = pl.program_id(2)
    kj = pl.program_id(3)
    tq = q_ref.shape[2]
    tk = k_ref.shape[2]

    @pl.when(kj == 0)
    def _():
        m_sc[...] = jnp.full(m_sc.shape, -jnp.inf, F32)
        l_sc[...] = jnp.zeros(l_sc.shape, F32)
        acc_sc[...] = jnp.zeros(acc_sc.shape, F32)

    first_lanes = lax.broadcasted_iota(jnp.int32, (tq, LANES), 1) < V_DIM

    def step(masked):
        v_pair = v_ref[0, 0]
        alphas = []
        pvs = []
        for a in range(2):
            s = lax.dot_general(q_ref[0, a], k_ref[0, a], (((1,), (1,)), ((), ())),
                                preferred_element_type=F32)
            if masked:
                rows = lax.broadcasted_iota(jnp.int32, (tq, tk), 0)
                cols = lax.broadcasted_iota(jnp.int32, (tq, tk), 1)
                s = jnp.where(cols <= rows, s, -1e30)
            m_prev = m_sc[a]
            m_new = jnp.maximum(m_prev, jnp.max(s, axis=-1, keepdims=True))
            alpha = jnp.exp(m_prev - m_new)
            p = jnp.exp(s - m_new)
            l_sc[a] = alpha * l_sc[a] + jnp.sum(p, axis=-1, keepdims=True)
            m_sc[a] = m_new
            alphas.append(alpha)
            pvs.append(jnp.dot(p.astype(BF16), v_pair, preferred_element_type=F32))
        alpha_sel = jnp.where(first_lanes, alphas[0], alphas[1])
        acc_sc[...] = alpha_sel * acc_sc[...] + jnp.where(first_lanes, pvs[0], pvs[1])

    @pl.when(kj < qi)
    def _():
        step(False)

    @pl.when(kj == qi)
    def _():
        step(True)
        l_sel = jnp.where(first_lanes, l_sc[0], l_sc[1])
        o_ref[0] = (acc_sc[...] / l_sel).astype(o_ref.dtype)


def _attention(q, k, v):
    bsz, heads, seq, _ = q.shape
    t = min(T_ATT, seq)
    nq = seq // t
    return pl.pallas_call(
        _attn_kernel,
        grid=(bsz, heads // 2, nq, nq),
        in_specs=[
            pl.BlockSpec((1, 2, t, HEAD_PAD), lambda b, hp, i, j: (b, hp, i, 0)),
            pl.BlockSpec((1, 2, t, HEAD_PAD), lambda b, hp, i, j: (b, hp, jnp.minimum(i, j), 0)),
            pl.BlockSpec((1, 1, t, LANES), lambda b, hp, i, j: (b, hp, jnp.minimum(i, j), 0)),
        ],
        out_specs=pl.BlockSpec((1, t, LANES), lambda b, hp, i, j: (b, i, hp)),
        out_shape=jax.ShapeDtypeStruct((bsz, seq, heads * V_DIM), BF16),
        scratch_shapes=[pltpu.VMEM((2, t, 1), F32), pltpu.VMEM((2, t, 1), F32), pltpu.VMEM((t, LANES), F32)],
        compiler_params=_cparams(("arbitrary", "arbitrary", "arbitrary", "arbitrary")),
        name="attn",
    )(q, k, v)


def _odd_post_kernel(x_ref, mod_ref, yc_ref, yd_ref, wout_ref, lng_ref, lnb_ref, o_ref):
    x = x_ref[0]
    gate = mod_ref[0, 2:3, :]
    y = (jnp.dot(yc_ref[0], wout_ref[0:W_C, :], preferred_element_type=F32)
         + jnp.dot(yd_ref[0], wout_ref[W_C:, :], preferred_element_type=F32))
    r = DEEPNORM_ALPHA * x + (1.0 + gate) * y
    o_ref[0] = _layer_norm(r, lng_ref[...], lnb_ref[...])


def _odd_post(x, mod_l, y_c, y_d, w_out, ln_g, ln_b):
    bsz, seq, d = x.shape
    t = min(T_POST, seq)
    full = lambda shape: pl.BlockSpec(shape, lambda b, i: (0,) * len(shape))
    return pl.pallas_call(
        _odd_post_kernel,
        grid=(bsz, seq // t),
        in_specs=[
            pl.BlockSpec((1, t, d), lambda b, i: (b, i, 0)),
            pl.BlockSpec((1, 6, d), lambda b, i: (b, 0, 0)),
            pl.BlockSpec((1, t, W_C), lambda b, i: (b, i, 0)),
            pl.BlockSpec((1, t, MLA_HEADS * V_DIM), lambda b, i: (b, i, 0)),
            full(w_out.shape), full((1, d)), full((1, d)),
        ],
        out_specs=pl.BlockSpec((1, t, d), lambda b, i: (b, i, 0)),
        out_shape=jax.ShapeDtypeStruct((bsz, seq, d), F32),
        compiler_params=_cparams(("arbitrary", "arbitrary")),
        name="odd_post",
    )(x, mod_l, y_c, y_d, w_out.astype(BF16), ln_g.reshape(1, d), ln_b.reshape(1, d))


def _router_kernel(x_ref, mod_ref, rw_ref, rb_ref, topi_ref, gate_ref, rank_ref, cnt_ref, carry):
    t_len = x_ref.shape[1]

    @pl.when((pl.program_id(0) == 0) & (pl.program_id(1) == 0))
    def _():
        carry[...] = jnp.zeros(carry.shape, F32)

    x = x_ref[0]
    shift, scale = mod_ref[0, 3:4, :], mod_ref[0, 4:5, :]
    u = x * (1.0 + scale) + shift
    logits = lax.dot_general(rw_ref[...], u, (((1,), (1,)), ((), ())), precision=HIGHEST,
                             preferred_element_type=F32) + rb_ref[...]
    eidx = lax.broadcasted_iota(jnp.int32, (N_EXPERTS, t_len), 0)
    vals, hots = [], []
    for _ in range(TOP_K):
        m = jnp.max(logits, axis=0, keepdims=True)
        idx = jnp.min(jnp.where(logits == m, eidx, N_EXPERTS), axis=0, keepdims=True)
        hot = eidx == idx
        vals.append(m)
        hots.append(hot)
        logits = jnp.where(hot, -jnp.inf, logits)
        topi_ref[len(vals) - 1:len(vals), :] = idx
    exps = [jnp.exp(v - vals[0]) for v in vals]
    denom = exps[0] + exps[1] + exps[2] + exps[3]
    for kk in range(TOP_K):
        gate_ref[kk:kk + 1, :] = exps[kk] / denom

    chosen = jnp.where(hots[0] | hots[1] | hots[2] | hots[3], 1.0, 0.0)
    r_i = lax.broadcasted_iota(jnp.int32, (t_len, t_len), 0)
    c_i = lax.broadcasted_iota(jnp.int32, (t_len, t_len), 1)
    before = jnp.where(r_i < c_i, 1.0, 0.0).astype(BF16)
    base = jnp.dot(chosen.astype(BF16), before, preferred_element_type=F32) + carry[...]
    for kk in range(TOP_K):
        rank_ref[kk:kk + 1, :] = jnp.sum(jnp.where(hots[kk], base, 0.0), axis=0, keepdims=True).astype(jnp.int32)
    carry[...] = carry[...] + jnp.sum(chosen, axis=1, keepdims=True)
    cnt_ref[...] = jnp.broadcast_to(carry[...], cnt_ref.shape)


def _router(x, mod_l, router_w, router_b):
    bsz, seq, d = x.shape
    t = min(T_ROUTE, seq)
    nt = seq // t
    n_tok = bsz * seq
    full = lambda shape: pl.BlockSpec(shape, lambda b, i: (0,) * len(shape))
    tok_spec = pl.BlockSpec((TOP_K, t), lambda b, i: (0, b * nt + i))
    return pl.pallas_call(
        _router_kernel,
        grid=(bsz, nt),
        in_specs=[
            pl.BlockSpec((1, t, d), lambda b, i: (b, i, 0)),
            pl.BlockSpec((1, 6, d), lambda b, i: (b, 0, 0)),
            full((N_EXPERTS, d)), full((N_EXPERTS, 1)),
        ],
        out_specs=[tok_spec, tok_spec, tok_spec, full((N_EXPERTS, LANES))],
        out_shape=[
            jax.ShapeDtypeStruct((TOP_K, n_tok), jnp.int32),
            jax.ShapeDtypeStruct((TOP_K, n_tok), F32),
            jax.ShapeDtypeStruct((TOP_K, n_tok), jnp.int32),
            jax.ShapeDtypeStruct((N_EXPERTS, LANES), F32),
        ],
        scratch_shapes=[pltpu.VMEM((N_EXPERTS, 1), F32)],
        compiler_params=_cparams(("arbitrary", "arbitrary")),
        name="router",
    )(x, mod_l, router_w.T, router_b.reshape(N_EXPERTS, 1))


def _dispatch_kernel(dest_ref, x_ref, mod_ref, xs_in_ref, xs_ref, ubuf, sem):
    del xs_in_ref
    t_len = x_ref.shape[1]
    x = x_ref[0]
    shift, scale = mod_ref[0, 3:4, :], mod_ref[0, 4:5, :]
    ubuf[...] = x * (1.0 + scale) + shift

    def row_copy(r, kk):
        return pltpu.make_async_copy(ubuf.at[pl.ds(r, 1)], xs_ref.at[pl.ds(dest_ref[0, 0, kk * t_len + r], 1)], sem)

    def issue(r, c):
        for kk in range(TOP_K):
            row_copy(r, kk).start()
        return c

    lax.fori_loop(0, t_len, issue, 0)

    def drain(r, c):
        for kk in range(TOP_K):
            row_copy(r, kk).wait()
        return c

    lax.fori_loop(0, t_len, drain, 0)


def _dispatch(x, mod_l, dest_tiles, n_slots):
    bsz, seq, d = x.shape
    t = min(T_DISP, seq)
    nt = seq // t
    xs0 = jnp.zeros((n_slots, d), F32)
    return pl.pallas_call(
        _dispatch_kernel,
        grid=(bsz, nt),
        in_specs=[
            pl.BlockSpec((1, 1, TOP_K * t), lambda b, i: (b * nt + i, 0, 0), memory_space=pltpu.SMEM),
            pl.BlockSpec((1, t, d), lambda b, i: (b, i, 0)),
            pl.BlockSpec((1, 6, d), lambda b, i: (b, 0, 0)),
            pl.BlockSpec(memory_space=pl.ANY),
        ],
        out_specs=pl.BlockSpec(memory_space=pl.ANY),
        out_shape=jax.ShapeDtypeStruct((n_slots, d), F32),
        scratch_shapes=[pltpu.VMEM((t, d), F32), pltpu.SemaphoreType.DMA(())],
        input_output_aliases={3: 0},
        compiler_params=_cparams(("arbitrary", "arbitrary")),
        name="dispatch",
    )(dest_tiles, x, mod_l, xs0)


def _expert_kernel(bexp_ref, nused_ref, xs_ref, wgu_ref, bgu_ref, wdn_ref, bdn_ref, ys_ref):
    del bexp_ref

    @pl.when(pl.program_id(0) < nused_ref[0])
    def _():
        gu = jnp.dot(xs_ref[...].astype(BF16), wgu_ref[0], preferred_element_type=F32) + bgu_ref[0]
        gate = jnp.minimum(gu[:, :D_EXPERT], SWIGLU_LIMIT)
        up = jnp.clip(gu[:, D_EXPERT:], -SWIGLU_LIMIT, SWIGLU_LIMIT)
        glu = gate * _sigmoid(SWIGLU_ALPHA * gate)
        h = ((up + 1.0) * glu).astype(BF16)
        ys_ref[...] = jnp.dot(h, wdn_ref[0], preferred_element_type=F32) + bdn_ref[0]

    @pl.when(pl.program_id(0) >= nused_ref[0])
    def _():
        ys_ref[...] = jnp.zeros(ys_ref.shape, F32)


def _experts(xs, block_exp, n_used, w_gu, b_gu, w_dn, b_dn):
    n_slots, d = xs.shape
    n_blocks = n_slots // BM
    row_map = lambda i, be, nu: (jnp.minimum(i, nu[0] - 1), 0)
    exp_map = lambda i, be, nu: (be[i], 0, 0)
    grid_spec = pltpu.PrefetchScalarGridSpec(
        num_scalar_prefetch=2,
        grid=(n_blocks,),
        in_specs=[
            pl.BlockSpec((BM, d), row_map),
            pl.BlockSpec((1, d, 2 * D_EXPERT), exp_map),
            pl.BlockSpec((1, 1, 2 * D_EXPERT), exp_map),
            pl.BlockSpec((1, D_EXPERT, d), exp_map),
            pl.BlockSpec((1, 1, d), exp_map),
        ],
        out_specs=pl.BlockSpec((BM, d), lambda i, be, nu: (i, 0)),
    )
    return pl.pallas_call(
        _expert_kernel,
        grid_spec=grid_spec,
        out_shape=jax.ShapeDtypeStruct((n_slots, d), F32),
        compiler_params=_cparams(("arbitrary",)),
        name="experts",
    )(block_exp, n_used, xs, w_gu, b_gu.reshape(N_EXPERTS, 1, 2 * D_EXPERT), w_dn,
      b_dn.reshape(N_EXPERTS, 1, d))


def _combine_kernel(dest_ref, x_ref, mod_ref, gate_ref, ys_ref, lng_ref, lnb_ref, o_ref, buf, sem):
    t_len = x_ref.shape[1]

    def row_copy(r, kk):
        return pltpu.make_async_copy(ys_ref.at[pl.ds(dest_ref[0, 0, kk * t_len + r], 1)],
                                     buf.at[kk, pl.ds(r, 1)], sem)

    def issue(r, c):
        for kk in range(TOP_K):
            row_copy(r, kk).start()
        return c

    lax.fori_loop(0, t_len, issue, 0)

    def drain(r, c):
        for kk in range(TOP_K):
            row_copy(r, kk).wait()
        return c

    lax.fori_loop(0, t_len, drain, 0)

    gates = gate_ref[...]
    y = gates[:, 0:1] * buf[0]
    for kk in range(1, TOP_K):
        y = y + gates[:, kk:kk + 1] * buf[kk]
    x = x_ref[0]
    gate_f = mod_ref[0, 5:6, :]
    r = DEEPNORM_ALPHA * x + (1.0 + gate_f) * y
    o_ref[0] = _layer_norm(r, lng_ref[...], lnb_ref[...])


def _combine(x, mod_l, dest_tiles, gates_t, ys, ln_g, ln_b):
    bsz, seq, d = x.shape
    t = min(T_COMB, seq)
    nt = seq // t
    full = lambda shape: pl.BlockSpec(shape, lambda b, i: (0,) * len(shape))
    return pl.pallas_call(
        _combine_kernel,
        grid=(bsz, nt),
        in_specs=[
            pl.BlockSpec((1, 1, TOP_K * t), lambda b, i: (b * nt + i, 0, 0), memory_space=pltpu.SMEM),
            pl.BlockSpec((1, t, d), lambda b, i: (b, i, 0)),
            pl.BlockSpec((1, 6, d), lambda b, i: (b, 0, 0)),
            pl.BlockSpec((t, TOP_K), lambda b, i: (b * nt + i, 0)),
            pl.BlockSpec(memory_space=pl.ANY),
            full((1, d)), full((1, d)),
        ],
        out_specs=pl.BlockSpec((1, t, d), lambda b, i: (b, i, 0)),
        out_shape=jax.ShapeDtypeStruct((bsz, seq, d), F32),
        scratch_shapes=[pltpu.VMEM((TOP_K, t, d), F32), pltpu.SemaphoreType.DMA(())],
        compiler_params=_cparams(("arbitrary", "arbitrary")),
        name="combine",
    )(dest_tiles, x, mod_l, gates_t, ys, ln_g.reshape(1, d), ln_b.reshape(1, d))


def _tile_major(a, t):
    k, n = a.shape
    return a.reshape(k, n // t, t).transpose(1, 0, 2).reshape(n // t, 1, k * t)


def _moe_layer(x, mod_l, router_w, router_b, w_gu, b_gu, w_dn, b_dn, ln_g, ln_b):
    bsz, seq, d = x.shape
    n_tok = bsz * seq
    n_slots = n_tok * TOP_K + N_EXPERTS * BM
    n_blocks = n_slots // BM
    topi, gates, rank, cnt = _router(x, mod_l, router_w, router_b)
    counts = cnt[:, 0].astype(jnp.int32)
    padded = (counts + BM - 1) // BM * BM
    pad_end = jnp.cumsum(padded)
    pad_start = pad_end - padded
    n_used = (pad_end[-1:] // BM).astype(jnp.int32)
    block_exp = jnp.minimum(
        jnp.searchsorted(pad_end, jnp.arange(n_blocks, dtype=jnp.int32) * BM, side='right'),
        N_EXPERTS - 1).astype(jnp.int32)
    dest = pad_start[topi] + rank
    xs = _dispatch(x, mod_l, _tile_major(dest, min(T_DISP, seq)), n_slots)
    ys = _experts(xs, block_exp, n_used, w_gu, b_gu, w_dn, b_dn)
    return _combine(x, mod_l, _tile_major(dest, min(T_COMB, seq)), gates.T, ys, ln_g, ln_b)


def kernel(x, c, positions, ada_w, ada_b, ln_mix_g, ln_mix_b, ln_ffn_g, ln_ffn_b, ev_w_in, ev_conv_w, ev_sg_w, ev_sg_b, ev_vn_g, ev_vn_b, ev_w_out, od_w_in, od_dw_w, od_dw_b, od_cn_g, od_cn_b, od_qn_g, od_w_uq, od_kvn_g, od_w_ukv, od_w_out, moe_router_w, moe_router_b, moe_w_gu, moe_b_gu, moe_w_dn, moe_b_dn):
    bsz, seq, d = x.shape
    depth = ada_w.shape[0]
    mod = _ada(c, ada_w, ada_b).reshape(depth, bsz, 6, d)
    cos_t, sin_t = _rope_tables(positions)
    w_gu = moe_w_gu.astype(BF16)
    w_dn = moe_w_dn.astype(BF16)
    for layer in range(depth):
        i = layer // 2
        mod_l = mod[layer]
        if layer % 2 == 0:
            x = _even_layer(x, mod_l, ev_w_in[i], ev_conv_w[i], ev_sg_w[i], ev_sg_b[i], ev_vn_g[i], ev_vn_b[i],
                            ev_w_out[i], ln_mix_g[layer], ln_mix_b[layer])
        else:
            y_c, q, k, v = _odd_pre(x, mod_l, cos_t, sin_t, od_w_in[i], od_dw_w[i], od_dw_b[i], od_cn_g[i],
                                    od_cn_b[i], od_qn_g[i], od_w_uq[i], od_kvn_g[i], od_w_ukv[i])
            y_d = _attention(q, k, v)
            x = _odd_post(x, mod_l, y_c, y_d, od_w_out[i], ln_mix_g[layer], ln_mix_b[layer])
        x = _moe_layer(x, mod_l, moe_router_w[layer], moe_router_b[layer], w_gu[layer], moe_b_gu[layer],
                       w_dn[layer], moe_b_dn[layer], ln_ffn_g[layer], ln_ffn_b[layer])
    return x
```

```python
import functools
import math

import jax
import jax.numpy as jnp
from jax import lax
from jax.experimental import pallas as pl
from jax.experimental.pallas import tpu as pltpu

F32 = jnp.float32
BF16 = jnp.bfloat16
HIGHEST = lax.Precision.HIGHEST

D_MODEL = 1024
DEPTH = 4
W_A = 512
W_B = 512
SG_HEADS = 8
SG_HEAD_DIM = 64
CHUNK = 128
CONV_A = 3
W_C = 512
CONV_C = 31
MLA_HEADS = 8
QK_NOPE = 64
QK_ROPE = 32
V_DIM = 64
Q_RANK = 256
KV_RANK = 128
ROPE_THETA = 10000.0
N_EXPERTS = 32
TOP_K = 4
D_EXPERT = 1024
SWIGLU_LIMIT = 7.0
SWIGLU_ALPHA = 1.702
DEEPNORM_ALPHA = (2.0 * DEPTH) ** 0.25
LN_EPS = 1e-5
RMS_EPS = 1e-6

LANES = 128
HEAD_PAD = 128
CONV_C_HALO = 32
CONV_A_HALO = 8

T_EVEN = 512
T_ODD = 256
T_POST = 512
T_ATT = 512
T_ROUTE = 512
T_DISP = 256
T_COMB = 256
BM = 512
VMEM_LIMIT = 56 * 1024 * 1024


def _cparams(sem):
    return pltpu.CompilerParams(dimension_semantics=sem, vmem_limit_bytes=VMEM_LIMIT)


def _layer_norm(v, g, b):
    mu = jnp.mean(v, axis=-1, keepdims=True)
    d = v - mu
    var = jnp.mean(d * d, axis=-1, keepdims=True)
    return d * lax.rsqrt(var + LN_EPS) * g + b


def _gelu(v):
    return 0.5 * v * (1.0 + lax.erf(v * (1.0 / math.sqrt(2.0))))


def _sigmoid(v):
    return 1.0 / (1.0 + jnp.exp(-v))


def _ada_kernel(c_ref, w_ref, b_ref, o_ref):
    c = c_ref[...]
    cond = c * _sigmoid(c)
    o_ref[0] = jnp.dot(cond, w_ref[0], precision=HIGHEST, preferred_element_type=F32) + b_ref[0]


def _ada(c, ada_w, ada_b):
    bsz, d = c.shape
    depth = ada_w.shape[0]
    n_chunk = ada_w.shape[2] // d
    return pl.pallas_call(
        _ada_kernel,
        grid=(depth, n_chunk),
        in_specs=[
            pl.BlockSpec((bsz, d), lambda l, j: (0, 0)),
            pl.BlockSpec((1, d, d), lambda l, j: (l, 0, j)),
            pl.BlockSpec((1, 1, d), lambda l, j: (l, 0, j)),
        ],
        out_specs=pl.BlockSpec((1, bsz, d), lambda l, j: (l, 0, j)),
        out_shape=jax.ShapeDtypeStruct((depth, bsz, n_chunk * d), F32),
        compiler_params=_cparams(("arbitrary", "arbitrary")),
        name="ada",
    )(c, ada_w, ada_b.reshape(depth, 1, n_chunk * d))


def _rope_kernel(pos_ref, freq_ref, cos_ref, sin_ref):
    ang = pos_ref[...].astype(F32) * freq_ref[...]
    cos_ref[...] = jnp.cos(ang)
    sin_ref[...] = jnp.sin(ang)


def _rope_tables(positions):
    n = positions.size
    t = 1024
    inv_freq = ROPE_THETA ** (-jnp.arange(0, QK_ROPE, 2, dtype=F32) / QK_ROPE)
    half = QK_ROPE // 2
    freq = jnp.zeros((1, LANES), F32).at[0, QK_NOPE:QK_NOPE + QK_ROPE].set(jnp.tile(inv_freq, 2))
    del half
    return pl.pallas_call(
        _rope_kernel,
        grid=(n // t,),
        in_specs=[pl.BlockSpec((t, 1), lambda i: (i, 0)), pl.BlockSpec((1, LANES), lambda i: (0, 0))],
        out_specs=[pl.BlockSpec((t, LANES), lambda i: (i, 0))] * 2,
        out_shape=[jax.ShapeDtypeStruct((n, LANES), F32)] * 2,
        compiler_params=_cparams(("arbitrary",)),
        name="rope",
    )(positions.reshape(n, 1), freq)


def _even_kernel(x_ref, mod_ref, win_ref, cw_ref, sgw_ref, sgb_ref, vng_ref, vnb_ref, wout_ref,
                 lng_ref, lnb_ref, o_ref, gbuf, ybuf):
    t_len = x_ref.shape[1]
    halo = CONV_A_HALO

    @pl.when(pl.program_id(1) == 0)
    def _():
        gbuf[0:halo, :] = jnp.zeros((halo, W_A), F32)

    x = x_ref[0]
    shift, scale, gate = mod_ref[0, 0:1, :], mod_ref[0, 1:2, :], mod_ref[0, 2:3, :]
    u = x * (1.0 + scale) + shift
    proj = jnp.dot(u.astype(BF16), win_ref[...], preferred_element_type=F32)
    b_gate = proj[:, 0:W_A]
    c_gate = proj[:, W_A:2 * W_A]
    xa = proj[:, 2 * W_A:3 * W_A]
    zu = proj[:, 3 * W_A:3 * W_A + W_B]
    zv = proj[:, 3 * W_A + W_B:3 * W_A + 2 * W_B]

    g = c_gate * xa
    gbuf[halo:halo + t_len, :] = g
    conv = (cw_ref[0:1, :] * gbuf[halo - 2:halo - 2 + t_len, :]
            + cw_ref[1:2, :] * gbuf[halo - 1:halo - 1 + t_len, :]
            + cw_ref[2:3, :] * g)
    gbuf[0:halo, :] = g[t_len - halo:t_len, :]
    ybuf[:, 0:W_A] = (b_gate * conv).astype(BF16)

    zu = _gelu(zu)
    zv = _layer_norm(_gelu(zv), vng_ref[...], vnb_ref[...]).astype(BF16)
    row = lax.broadcasted_iota(jnp.int32, (CHUNK, CHUNK), 0)
    col = lax.broadcasted_iota(jnp.int32, (CHUNK, CHUNK), 1)
    w_stack = jnp.concatenate(
        [jnp.where(row >= col, sgw_ref[h], 0.0).astype(BF16) for h in range(SG_HEADS)], axis=0)
    col_head = lax.broadcasted_iota(jnp.int32, (CHUNK, W_B), 1) // SG_HEAD_DIM
    for ci in range(t_len // CHUNK):
        lo = ci * CHUNK
        full = jnp.dot(w_stack, zv[lo:lo + CHUNK, :], preferred_element_type=F32)
        mixed = sgb_ref[...]
        for h in range(SG_HEADS):
            mixed = mixed + jnp.where(col_head == h, full[h * CHUNK:(h + 1) * CHUNK, :], 0.0)
        ybuf[lo:lo + CHUNK, W_A:W_A + W_B] = (zu[lo:lo + CHUNK, :] * mixed).astype(BF16)

    y = jnp.dot(ybuf[...], wout_ref[...], preferred_element_type=F32)
    r = DEEPNORM_ALPHA * x + (1.0 + gate) * y
    o_ref[0] = _layer_norm(r, lng_ref[...], lnb_ref[...])


def _even_layer(x, mod_l, w_in, conv_w, sg_w, sg_b, vn_g, vn_b, w_out, ln_g, ln_b):
    bsz, seq, d = x.shape
    t = min(T_EVEN, seq)
    sgb_full = jnp.repeat(sg_b.T, SG_HEAD_DIM, axis=1)
    full = lambda shape: pl.BlockSpec(shape, lambda b, i: (0,) * len(shape))
    return pl.pallas_call(
        _even_kernel,
        grid=(bsz, seq // t),
        in_specs=[
            pl.BlockSpec((1, t, d), lambda b, i: (b, i, 0)),
            pl.BlockSpec((1, 6, d), lambda b, i: (b, 0, 0)),
            full(w_in.shape), full(conv_w.shape), full(sg_w.shape), full(sgb_full.shape),
            full((1, W_B)), full((1, W_B)), full(w_out.shape), full((1, d)), full((1, d)),
        ],
        out_specs=pl.BlockSpec((1, t, d), lambda b, i: (b, i, 0)),
        out_shape=jax.ShapeDtypeStruct((bsz, seq, d), F32),
        scratch_shapes=[pltpu.VMEM((CONV_A_HALO + t, W_A), F32), pltpu.VMEM((t, W_A + W_B), BF16)],
        compiler_params=_cparams(("arbitrary", "arbitrary")),
        name="even",
    )(x, mod_l, w_in.astype(BF16), conv_w, sg_w, sgb_full, vn_g.reshape(1, W_B), vn_b.reshape(1, W_B),
      w_out.astype(BF16), ln_g.reshape(1, d), ln_b.reshape(1, d))


N_GLU = 2 * W_C
OFF_Q = N_GLU
OFF_KV = OFF_Q + Q_RANK
OFF_KR = OFF_KV + KV_RANK
OFF_KR_ROT = OFF_KR + HEAD_PAD
ODD_IN_PAD = OFF_KR_ROT + HEAD_PAD


def _odd_pre_kernel(x_ref, mod_ref, cos_ref, sin_ref, win_ref, dww_ref, dwb_ref, cng_ref, cnb_ref,
                    qng_ref, wq_ref, wqr_ref, kvng_ref, wk_ref, wv_ref,
                    yc_ref, q_ref, k_ref, v_ref, gbuf):
    t_len = x_ref.shape[1]
    halo = CONV_C_HALO

    @pl.when(pl.program_id(1) == 0)
    def _():
        gbuf[0:halo, :] = jnp.zeros((halo, W_C), F32)

    x = x_ref[0]
    shift, scale = mod_ref[0, 0:1, :], mod_ref[0, 1:2, :]
    u = x * (1.0 + scale) + shift
    proj = jnp.dot(u.astype(BF16), win_ref[...], preferred_element_type=F32)

    g = proj[:, 0:W_C] * _sigmoid(proj[:, W_C:2 * W_C])
    gbuf[halo:halo + t_len, :] = g
    acc = dwb_ref[...] + dww_ref[CONV_C - 1:CONV_C, :] * g
    for tap in range(CONV_C - 1):
        lo = halo - (CONV_C - 1) + tap
        acc = acc + dww_ref[tap:tap + 1, :] * gbuf[lo:lo + t_len, :]
    gbuf[0:halo, :] = g[t_len - halo:t_len, :]
    hn = _layer_norm(acc, cng_ref[...], cnb_ref[...])
    yc_ref[0] = (hn * _sigmoid(hn)).astype(BF16)

    cos = cos_ref[...]
    sin = sin_ref[...]
    cq = proj[:, OFF_Q:OFF_Q + Q_RANK]
    q_lat = (cq * lax.rsqrt(jnp.mean(cq * cq, axis=-1, keepdims=True) + RMS_EPS) * qng_ref[...]).astype(BF16)
    ckv = proj[:, OFF_KV:OFF_KV + KV_RANK]
    kv_lat = (ckv * lax.rsqrt(jnp.mean(ckv * ckv, axis=-1, keepdims=True) + RMS_EPS) * kvng_ref[...]).astype(BF16)
    q_all = jnp.dot(q_lat, wq_ref[...], preferred_element_type=F32)
    q_rot = jnp.dot(q_lat, wqr_ref[...], preferred_element_type=F32)
    k_all = jnp.dot(kv_lat, wk_ref[...], preferred_element_type=F32)
    v_all = jnp.dot(kv_lat, wv_ref[...], preferred_element_type=F32)
    k_rope = proj[:, OFF_KR:OFF_KR + HEAD_PAD] * cos + proj[:, OFF_KR_ROT:OFF_KR_ROT + HEAD_PAD] * sin
    sm_scale = math.log2(math.e) / math.sqrt(QK_NOPE + QK_ROPE)
    for h in range(MLA_HEADS):
        sl = slice(h * HEAD_PAD, (h + 1) * HEAD_PAD)
        q_ref[0, h] = ((q_all[:, sl] * cos + q_rot[:, sl] * sin) * sm_scale).astype(BF16)
        k_ref[0, h] = (k_all[:, sl] + k_rope).astype(BF16)
    for hp in range(MLA_HEADS // 2):
        v_ref[0, hp] = v_all[:, hp * LANES:(hp + 1) * LANES].astype(BF16)


def _odd_weights(w_in, w_uq, w_ukv):
    d = w_in.shape[0]
    half = QK_ROPE // 2
    kr = w_in[:, OFF_KR:OFF_KR + QK_ROPE]
    z = lambda n: jnp.zeros((d, n), w_in.dtype)
    kr_blk = jnp.concatenate([z(QK_NOPE), kr, z(HEAD_PAD - QK_NOPE - QK_ROPE)], axis=1)
    kr_rot = jnp.concatenate([z(QK_NOPE), -kr[:, half:], kr[:, :half], z(HEAD_PAD - QK_NOPE - QK_ROPE)], axis=1)
    w_in_p = jnp.concatenate([w_in[:, :OFF_KR], kr_blk, kr_rot], axis=1).astype(BF16)

    dq = QK_NOPE + QK_ROPE
    wq = w_uq.reshape(Q_RANK, MLA_HEADS, dq)
    zq = lambda n: jnp.zeros((Q_RANK, MLA_HEADS, n), w_uq.dtype)
    wq_p = jnp.concatenate([wq, zq(HEAD_PAD - dq)], axis=2).reshape(Q_RANK, MLA_HEADS * HEAD_PAD)
    wq_r = jnp.concatenate([zq(QK_NOPE), -wq[:, :, QK_NOPE + half:], wq[:, :, QK_NOPE:QK_NOPE + half],
                            zq(HEAD_PAD - dq)], axis=2).reshape(Q_RANK, MLA_HEADS * HEAD_PAD)

    wkv = w_ukv.reshape(KV_RANK, MLA_HEADS, QK_NOPE + V_DIM)
    wk_p = jnp.concatenate([wkv[:, :, :QK_NOPE], jnp.zeros((KV_RANK, MLA_HEADS, HEAD_PAD - QK_NOPE), w_ukv.dtype)],
                           axis=2).reshape(KV_RANK, MLA_HEADS * HEAD_PAD)
    wv_p = wkv[:, :, QK_NOPE:].reshape(KV_RANK, MLA_HEADS * V_DIM)
    return w_in_p, wq_p.astype(BF16), wq_r.astype(BF16), wk_p.astype(BF16), wv_p.astype(BF16)


def _odd_pre(x, mod_l, cos_t, sin_t, w_in, dw_w, dw_b, cn_g, cn_b, qn_g, w_uq, kvn_g, w_ukv):
    bsz, seq, d = x.shape
    t = min(T_ODD, seq)
    nt = seq // t
    w_in_p, wq_p, wq_r, wk_p, wv_p = _odd_weights(w_in, w_uq, w_ukv)
    full = lambda shape: pl.BlockSpec(shape, lambda b, i: (0,) * len(shape))
    return pl.pallas_call(
        _odd_pre_kernel,
        grid=(bsz, nt),
        in_specs=[
            pl.BlockSpec((1, t, d), lambda b, i: (b, i, 0)),
            pl.BlockSpec((1, 6, d), lambda b, i: (b, 0, 0)),
            pl.BlockSpec((t, LANES), lambda b, i: (b * nt + i, 0)),
            pl.BlockSpec((t, LANES), lambda b, i: (b * nt + i, 0)),
            full(w_in_p.shape), full(dw_w.shape), full((1, W_C)), full((1, W_C)), full((1, W_C)),
            full((1, Q_RANK)), full(wq_p.shape), full(wq_r.shape), full((1, KV_RANK)),
            full(wk_p.shape), full(wv_p.shape),
        ],
        out_specs=[
            pl.BlockSpec((1, t, W_C), lambda b, i: (b, i, 0)),
            pl.BlockSpec((1, MLA_HEADS, t, HEAD_PAD), lambda b, i: (b, 0, i, 0)),
            pl.BlockSpec((1, MLA_HEADS, t, HEAD_PAD), lambda b, i: (b, 0, i, 0)),
            pl.BlockSpec((1, MLA_HEADS // 2, t, LANES), lambda b, i: (b, 0, i, 0)),
        ],
        out_shape=[
            jax.ShapeDtypeStruct((bsz, seq, W_C), BF16),
            jax.ShapeDtypeStruct((bsz, MLA_HEADS, seq, HEAD_PAD), BF16),
            jax.ShapeDtypeStruct((bsz, MLA_HEADS, seq, HEAD_PAD), BF16),
            jax.ShapeDtypeStruct((bsz, MLA_HEADS // 2, seq, LANES), BF16),
        ],
        scratch_shapes=[pltpu.VMEM((CONV_C_HALO + t, W_C), F32)],
        compiler_params=_cparams(("arbitrary", "arbitrary")),
        name="odd_pre",
    )(x, mod_l, cos_t, sin_t, w_in_p, dw_w, dw_b.reshape(1, W_C), cn_g.reshape(1, W_C), cn_b.reshape(1, W_C),
      qn_g.reshape(1, Q_RANK), wq_p, wq_r, kvn_g.reshape(1, KV_RANK), wk_p, wv_p)


def _attn_kernel(qi_tab, kj_tab, q_ref, k_ref, v_ref, o_ref, m_sc, l_sc, acc_sc):
    step_id = pl.program_id(2)
    qi = qi_tab[step_id]
    kj = kj_tab[step_id]
    tq = q_ref.shape[2]
    tk = k_ref.shape[2]
    n_col = tk // LANES

    @pl.when(kj == 0)
    def _():
        m_sc[...] = jnp.full(m_sc.shape, -jnp.inf, F32)
        l_sc[...] = jnp.zeros(l_sc.shape, F32)
        acc_sc[...] = jnp.zeros(acc_sc.shape, F32)

    first_lanes = lax.broadcasted_iota(jnp.int32, (tq, LANES), 1) < V_DIM

    def step(masked):
        v_pair = v_ref[0, 0]
        alphas = []
        pvs = []
        for a in range(2):
            s = lax.dot_general(q_ref[0, a], k_ref[0, a], (((1,), (1,)), ((), ())),
                                preferred_element_type=F32)
            if masked:
                rows = lax.broadcasted_iota(jnp.int32, (tq, tk), 0)
                cols = lax.broadcasted_iota(jnp.int32, (tq, tk), 1)
                s = jnp.where(cols <= rows, s, -1e30)
            cols_s = [s[:, c * LANES:(c + 1) * LANES] for c in range(n_col)]
            m_lane = cols_s[0]
            for c in range(1, n_col):
                m_lane = jnp.maximum(m_lane, cols_s[c])
            m_prev = m_sc[a]
            m_next = jnp.maximum(m_prev, jnp.max(m_lane, axis=1, keepdims=True))
            alpha = jnp.exp2(m_prev - m_next)
            ps = [jnp.exp2(cs - m_next) for cs in cols_s]
            l_lane = ps[0]
            for c in range(1, n_col):
                l_lane = l_lane + ps[c]
            l_sc[a] = alpha * l_sc[a] + l_lane
            m_sc[a] = m_next
            alphas.append(alpha)
            p = jnp.concatenate([pc.astype(BF16) for pc in ps], axis=1)
            pvs.append(jnp.dot(p, v_pair, preferred_element_type=F32))
        alpha_sel = jnp.where(first_lanes, alphas[0], alphas[1])
        acc_sc[...] = alpha_sel * acc_sc[...] + jnp.where(first_lanes, pvs[0], pvs[1])

    @pl.when(kj < qi)
    def _():
        step(False)

    @pl.when(kj == qi)
    def _():
        step(True)
        l_sel = jnp.where(first_lanes, jnp.sum(l_sc[0], axis=1, keepdims=True),
                          jnp.sum(l_sc[1], axis=1, keepdims=True))
        o_ref[0] = (acc_sc[...] / l_sel).astype(o_ref.dtype)


def _attention(q, k, v):
    bsz, heads, seq, _ = q.shape
    t = min(T_ATT, seq)
    nq = seq // t
    pairs = [(i, j) for i in range(nq) for j in range(i + 1)]
    qi_tab = jnp.asarray([p[0] for p in pairs], jnp.int32)
    kj_tab = jnp.asarray([p[1] for p in pairs], jnp.int32)
    grid_spec = pltpu.PrefetchScalarGridSpec(
        num_scalar_prefetch=2,
        grid=(bsz, heads // 2, len(pairs)),
        in_specs=[
            pl.BlockSpec((1, 2, t, HEAD_PAD), lambda b, hp, s, qt, kt: (b, hp, qt[s], 0)),
            pl.BlockSpec((1, 2, t, HEAD_PAD), lambda b, hp, s, qt, kt: (b, hp, kt[s], 0)),
            pl.BlockSpec((1, 1, t, LANES), lambda b, hp, s, qt, kt: (b, hp, kt[s], 0)),
        ],
        out_specs=pl.BlockSpec((1, t, LANES), lambda b, hp, s, qt, kt: (b, qt[s], hp)),
        scratch_shapes=[pltpu.VMEM((2, t, LANES), F32), pltpu.VMEM((2, t, LANES), F32),
                        pltpu.VMEM((t, LANES), F32)],
    )
    return pl.pallas_call(
        _attn_kernel,
        grid_spec=grid_spec,
        out_shape=jax.ShapeDtypeStruct((bsz, seq, heads * V_DIM), BF16),
        compiler_params=_cparams(("arbitrary", "arbitrary", "arbitrary")),
        name="attn",
    )(qi_tab, kj_tab, q, k, v)


def _odd_post_kernel(x_ref, mod_ref, yc_ref, yd_ref, wout_ref, lng_ref, lnb_ref, o_ref):
    x = x_ref[0]
    gate = mod_ref[0, 2:3, :]
    y = (jnp.dot(yc_ref[0], wout_ref[0:W_C, :], preferred_element_type=F32)
         + jnp.dot(yd_ref[0], wout_ref[W_C:, :], preferred_element_type=F32))
    r = DEEPNORM_ALPHA * x + (1.0 + gate) * y
    o_ref[0] = _layer_norm(r, lng_ref[...], lnb_ref[...])


def _odd_post(x, mod_l, y_c, y_d, w_out, ln_g, ln_b):
    bsz, seq, d = x.shape
    t = min(T_POST, seq)
    full = lambda shape: pl.BlockSpec(shape, lambda b, i: (0,) * len(shape))
    return pl.pallas_call(
        _odd_post_kernel,
        grid=(bsz, seq // t),
        in_specs=[
            pl.BlockSpec((1, t, d), lambda b, i: (b, i, 0)),
            pl.BlockSpec((1, 6, d), lambda b, i: (b, 0, 0)),
            pl.BlockSpec((1, t, W_C), lambda b, i: (b, i, 0)),
            pl.BlockSpec((1, t, MLA_HEADS * V_DIM), lambda b, i: (b, i, 0)),
            full(w_out.shape), full((1, d)), full((1, d)),
        ],
        out_specs=pl.BlockSpec((1, t, d), lambda b, i: (b, i, 0)),
        out_shape=jax.ShapeDtypeStruct((bsz, seq, d), F32),
        compiler_params=_cparams(("arbitrary", "arbitrary")),
        name="odd_post",
    )(x, mod_l, y_c, y_d, w_out.astype(BF16), ln_g.reshape(1, d), ln_b.reshape(1, d))


def _router_kernel(x_ref, mod_ref, rw_ref, rb_ref, topi_ref, gate_ref, rank_ref, cnt_ref, carry):
    t_len = x_ref.shape[1]

    @pl.when((pl.program_id(0) == 0) & (pl.program_id(1) == 0))
    def _():
        carry[...] = jnp.zeros(carry.shape, F32)

    x = x_ref[0]
    shift, scale = mod_ref[0, 3:4, :], mod_ref[0, 4:5, :]
    u = x * (1.0 + scale) + shift
    logits = lax.dot_general(rw_ref[...], u, (((1,), (1,)), ((), ())), precision=HIGHEST,
                             preferred_element_type=F32) + rb_ref[...]
    eidx = lax.broadcasted_iota(jnp.int32, (N_EXPERTS, t_len), 0)
    vals, hots = [], []
    for _ in range(TOP_K):
        m = jnp.max(logits, axis=0, keepdims=True)
        idx = jnp.min(jnp.where(logits == m, eidx, N_EXPERTS), axis=0, keepdims=True)
        hot = eidx == idx
        vals.append(m)
        hots.append(hot)
        logits = jnp.where(hot, -jnp.inf, logits)
        topi_ref[len(vals) - 1:len(vals), :] = idx
    exps = [jnp.exp(v - vals[0]) for v in vals]
    denom = exps[0] + exps[1] + exps[2] + exps[3]
    for kk in range(TOP_K):
        gate_ref[kk:kk + 1, :] = exps[kk] / denom

    chosen = jnp.where(hots[0] | hots[1] | hots[2] | hots[3], 1.0, 0.0)
    r_i = lax.broadcasted_iota(jnp.int32, (t_len, t_len), 0)
    c_i = lax.broadcasted_iota(jnp.int32, (t_len, t_len), 1)
    before = jnp.where(r_i < c_i, 1.0, 0.0).astype(BF16)
    base = jnp.dot(chosen.astype(BF16), before, preferred_element_type=F32) + carry[...]
    for kk in range(TOP_K):
        rank_ref[kk:kk + 1, :] = jnp.sum(jnp.where(hots[kk], base, 0.0), axis=0, keepdims=True).astype(jnp.int32)
    carry[...] = carry[...] + jnp.sum(chosen, axis=1, keepdims=True)
    cnt_ref[...] = jnp.broadcast_to(carry[...], cnt_ref.shape)


def _router(x, mod_l, router_w, router_b):
    bsz, seq, d = x.shape
    t = min(T_ROUTE, seq)
    nt = seq // t
    n_tok = bsz * seq
    full = lambda shape: pl.BlockSpec(shape, lambda b, i: (0,) * len(shape))
    tok_spec = pl.BlockSpec((TOP_K, t), lambda b, i: (0, b * nt + i))
    return pl.pallas_call(
        _router_kernel,
        grid=(bsz, nt),
        in_specs=[
            pl.BlockSpec((1, t, d), lambda b, i: (b, i, 0)),
            pl.BlockSpec((1, 6, d), lambda b, i: (b, 0, 0)),
            full((N_EXPERTS, d)), full((N_EXPERTS, 1)),
        ],
        out_specs=[tok_spec, tok_spec, tok_spec, full((N_EXPERTS, LANES))],
        out_shape=[
            jax.ShapeDtypeStruct((TOP_K, n_tok), jnp.int32),
            jax.ShapeDtypeStruct((TOP_K, n_tok), F32),
            jax.ShapeDtypeStruct((TOP_K, n_tok), jnp.int32),
            jax.ShapeDtypeStruct((N_EXPERTS, LANES), F32),
        ],
        scratch_shapes=[pltpu.VMEM((N_EXPERTS, 1), F32)],
        compiler_params=_cparams(("arbitrary", "arbitrary")),
        name="router",
    )(x, mod_l, router_w.T, router_b.reshape(N_EXPERTS, 1))


def _dispatch_kernel(dest_ref, x_ref, mod_ref, xs_in_ref, xs_ref, ubuf, sem):
    del xs_in_ref
    t_len = x_ref.shape[1]
    x = x_ref[0]
    shift, scale = mod_ref[0, 3:4, :], mod_ref[0, 4:5, :]
    ubuf[...] = x * (1.0 + scale) + shift

    def row_copy(r, kk):
        return pltpu.make_async_copy(ubuf.at[pl.ds(r, 1)], xs_ref.at[pl.ds(dest_ref[0, 0, kk * t_len + r], 1)], sem)

    def issue(r, c):
        for kk in range(TOP_K):
            row_copy(r, kk).start()
        return c

    lax.fori_loop(0, t_len, issue, 0)

    def drain(r, c):
        for kk in range(TOP_K):
            row_copy(r, kk).wait()
        return c

    lax.fori_loop(0, t_len, drain, 0)


def _dispatch(x, mod_l, dest_tiles, n_slots):
    bsz, seq, d = x.shape
    t = min(T_DISP, seq)
    nt = seq // t
    xs0 = jnp.zeros((n_slots, d), F32)
    return pl.pallas_call(
        _dispatch_kernel,
        grid=(bsz, nt),
        in_specs=[
            pl.BlockSpec((1, 1, TOP_K * t), lambda b, i: (b * nt + i, 0, 0), memory_space=pltpu.SMEM),
            pl.BlockSpec((1, t, d), lambda b, i: (b, i, 0)),
            pl.BlockSpec((1, 6, d), lambda b, i: (b, 0, 0)),
            pl.BlockSpec(memory_space=pl.ANY),
        ],
        out_specs=pl.BlockSpec(memory_space=pl.ANY),
        out_shape=jax.ShapeDtypeStruct((n_slots, d), F32),
        scratch_shapes=[pltpu.VMEM((t, d), F32), pltpu.SemaphoreType.DMA(())],
        input_output_aliases={3: 0},
        compiler_params=_cparams(("arbitrary", "arbitrary")),
        name="dispatch",
    )(dest_tiles, x, mod_l, xs0)


def _expert_kernel(bexp_ref, nused_ref, xs_ref, wgu_ref, bgu_ref, wdn_ref, bdn_ref, ys_ref, wgu_bf, wdn_bf):
    i = pl.program_id(0)
    active = i < nused_ref[0]
    new_expert = (i == 0) | (bexp_ref[i] != bexp_ref[jnp.maximum(i - 1, 0)])

    @pl.when(active & new_expert)
    def _():
        wgu_bf[...] = wgu_ref[0, 0].astype(BF16)
        wdn_bf[...] = wdn_ref[0, 0].astype(BF16)

    @pl.when(active)
    def _():
        gu = jnp.dot(xs_ref[...].astype(BF16), wgu_bf[...], preferred_element_type=F32) + bgu_ref[0, 0]
        gate = jnp.minimum(gu[:, :D_EXPERT], SWIGLU_LIMIT)
        up = jnp.clip(gu[:, D_EXPERT:], -SWIGLU_LIMIT, SWIGLU_LIMIT)
        glu = gate * _sigmoid(SWIGLU_ALPHA * gate)
        h = ((up + 1.0) * glu).astype(BF16)
        ys_ref[...] = jnp.dot(h, wdn_bf[...], preferred_element_type=F32) + bdn_ref[0, 0]

    @pl.when(jnp.logical_not(active))
    def _():
        ys_ref[...] = jnp.zeros(ys_ref.shape, F32)


def _experts(xs, block_exp, n_used, layer, w_gu, b_gu, w_dn, b_dn):
    n_slots, d = xs.shape
    n_blocks = n_slots // BM
    depth = w_gu.shape[0]
    row_map = lambda i, be, nu: (jnp.minimum(i, nu[0] - 1), 0)
    exp_map = lambda i, be, nu: (layer, be[i], 0, 0)
    grid_spec = pltpu.PrefetchScalarGridSpec(
        num_scalar_prefetch=2,
        grid=(n_blocks,),
        in_specs=[
            pl.BlockSpec((BM, d), row_map),
            pl.BlockSpec((1, 1, d, 2 * D_EXPERT), exp_map),
            pl.BlockSpec((1, 1, 1, 2 * D_EXPERT), exp_map),
            pl.BlockSpec((1, 1, D_EXPERT, d), exp_map),
            pl.BlockSpec((1, 1, 1, d), exp_map),
        ],
        out_specs=pl.BlockSpec((BM, d), lambda i, be, nu: (i, 0)),
        scratch_shapes=[pltpu.VMEM((d, 2 * D_EXPERT), BF16), pltpu.VMEM((D_EXPERT, d), BF16)],
    )
    return pl.pallas_call(
        _expert_kernel,
        grid_spec=grid_spec,
        out_shape=jax.ShapeDtypeStruct((n_slots, d), F32),
        compiler_params=_cparams(("arbitrary",)),
        name="experts",
    )(block_exp, n_used, xs, w_gu, b_gu.reshape(depth, N_EXPERTS, 1, 2 * D_EXPERT), w_dn,
      b_dn.reshape(depth, N_EXPERTS, 1, d))


def _combine_kernel(dest_ref, x_ref, mod_ref, gate_ref, ys_ref, lng_ref, lnb_ref, o_ref, buf, sem):
    t_len = x_ref.shape[1]

    def row_copy(r, kk):
        return pltpu.make_async_copy(ys_ref.at[pl.ds(dest_ref[0, 0, kk * t_len + r], 1)],
                                     buf.at[kk, pl.ds(r, 1)], sem)

    def issue(r, c):
        for kk in range(TOP_K):
            row_copy(r, kk).start()
        return c

    lax.fori_loop(0, t_len, issue, 0)

    def drain(r, c):
        for kk in range(TOP_K):
            row_copy(r, kk).wait()
        return c

    lax.fori_loop(0, t_len, drain, 0)

    gates = gate_ref[...]
    y = gates[:, 0:1] * buf[0]
    for kk in range(1, TOP_K):
        y = y + gates[:, kk:kk + 1] * buf[kk]
    x = x_ref[0]
    gate_f = mod_ref[0, 5:6, :]
    r = DEEPNORM_ALPHA * x + (1.0 + gate_f) * y
    o_ref[0] = _layer_norm(r, lng_ref[...], lnb_ref[...])


def _combine(x, mod_l, dest_tiles, gates_t, ys, ln_g, ln_b):
    bsz, seq, d = x.shape
    t = min(T_COMB, seq)
    nt = seq // t
    full = lambda shape: pl.BlockSpec(shape, lambda b, i: (0,) * len(shape))
    return pl.pallas_call(
        _combine_kernel,
        grid=(bsz, nt),
        in_specs=[
            pl.BlockSpec((1, 1, TOP_K * t), lambda b, i: (b * nt + i, 0, 0), memory_space=pltpu.SMEM),
            pl.BlockSpec((1, t, d), lambda b, i: (b, i, 0)),
            pl.BlockSpec((1, 6, d), lambda b, i: (b, 0, 0)),
            pl.BlockSpec((t, TOP_K), lambda b, i: (b * nt + i, 0)),
            pl.BlockSpec(memory_space=pl.ANY),
            full((1, d)), full((1, d)),
        ],
        out_specs=pl.BlockSpec((1, t, d), lambda b, i: (b, i, 0)),
        out_shape=jax.ShapeDtypeStruct((bsz, seq, d), F32),
        scratch_shapes=[pltpu.VMEM((TOP_K, t, d), F32), pltpu.SemaphoreType.DMA(())],
        compiler_params=_cparams(("arbitrary", "arbitrary")),
        name="combine",
    )(dest_tiles, x, mod_l, gates_t, ys, ln_g.reshape(1, d), ln_b.reshape(1, d))


def _tile_major(a, t):
    k, n = a.shape
    return a.reshape(k, n // t, t).transpose(1, 0, 2).reshape(n // t, 1, k * t)


def _slots_kernel(pstart_ref, topi_ref, rank_ref, dest_ref):
    topi = topi_ref[...]
    dest = rank_ref[...]
    for e in range(N_EXPERTS):
        dest = dest + jnp.where(topi == e, pstart_ref[e], 0)
    dest_ref[...] = dest


def _slots(pad_start, topi, rank):
    k, n = topi.shape
    t = min(8192, n)
    spec = pl.BlockSpec((k, t), lambda i, ps: (0, i))
    return pl.pallas_call(
        _slots_kernel,
        grid_spec=pltpu.PrefetchScalarGridSpec(num_scalar_prefetch=1, grid=(n // t,), in_specs=[spec, spec],
                                               out_specs=spec),
        out_shape=jax.ShapeDtypeStruct((k, n), jnp.int32),
        compiler_params=_cparams(("arbitrary",)),
        name="slots",
    )(pad_start, topi, rank)


def _moe_layer(x, mod_l, layer, router_w, router_b, w_gu, b_gu, w_dn, b_dn, ln_g, ln_b):
    bsz, seq, d = x.shape
    n_tok = bsz * seq
    n_slots = n_tok * TOP_K + N_EXPERTS * BM
    n_blocks = n_slots // BM
    topi, gates, rank, cnt = _router(x, mod_l, router_w, router_b)
    counts = cnt[:, 0].astype(jnp.int32)
    padded = (counts + BM - 1) // BM * BM
    pad_end = jnp.cumsum(padded)
    pad_start = pad_end - padded
    n_used = (pad_end[-1:] // BM).astype(jnp.int32)
    block_lo = jnp.arange(n_blocks, dtype=jnp.int32) * BM
    block_exp = jnp.minimum(jnp.sum((pad_end[None, :] <= block_lo[:, None]).astype(jnp.int32), axis=1),
                            N_EXPERTS - 1)
    dest = _slots(pad_start, topi, rank)
    xs = _dispatch(x, mod_l, _tile_major(dest, min(T_DISP, seq)), n_slots)
    ys = _experts(xs, block_exp, n_used, layer, w_gu, b_gu, w_dn, b_dn)
    return _combine(x, mod_l, _tile_major(dest, min(T_COMB, seq)), gates.T, ys, ln_g, ln_b)


def kernel(x, c, positions, ada_w, ada_b, ln_mix_g, ln_mix_b, ln_ffn_g, ln_ffn_b, ev_w_in, ev_conv_w, ev_sg_w, ev_sg_b, ev_vn_g, ev_vn_b, ev_w_out, od_w_in, od_dw_w, od_dw_b, od_cn_g, od_cn_b, od_qn_g, od_w_uq, od_kvn_g, od_w_ukv, od_w_out, moe_router_w, moe_router_b, moe_w_gu, moe_b_gu, moe_w_dn, moe_b_dn):
    bsz, seq, d = x.shape
    depth = ada_w.shape[0]
    mod = _ada(c, ada_w, ada_b).reshape(depth, bsz, 6, d)
    cos_t, sin_t = _rope_tables(positions)
    for layer in range(depth):
        i = layer // 2
        mod_l = mod[layer]
        if layer % 2 == 0:
            x = _even_layer(x, mod_l, ev_w_in[i], ev_conv_w[i], ev_sg_w[i], ev_sg_b[i], ev_vn_g[i], ev_vn_b[i],
                            ev_w_out[i], ln_mix_g[layer], ln_mix_b[layer])
        else:
            y_c, q, k, v = _odd_pre(x, mod_l, cos_t, sin_t, od_w_in[i], od_dw_w[i], od_dw_b[i], od_cn_g[i],
                                    od_cn_b[i], od_qn_g[i], od_w_uq[i], od_kvn_g[i], od_w_ukv[i])
            y_d = _attention(q, k, v)
            x = _odd_post(x, mod_l, y_c, y_d, od_w_out[i], ln_mix_g[layer], ln_mix_b[layer])
        x = _moe_layer(x, mod_l, layer, moe_router_w[layer], moe_router_b[layer], moe_w_gu, moe_b_gu,
                       moe_w_dn, moe_b_dn, ln_ffn_g[layer], ln_ffn_b[layer])
    return x
```

```python
import functools
import math

import jax
import jax.numpy as jnp
from jax import lax
from jax.experimental import pallas as pl
from jax.experimental.pallas import tpu as pltpu

F32 = jnp.float32
BF16 = jnp.bfloat16
HIGHEST = lax.Precision.HIGHEST

D_MODEL = 1024
DEPTH = 4
W_A = 512
W_B = 512
SG_HEADS = 8
SG_HEAD_DIM = 64
CHUNK = 128
CONV_A = 3
W_C = 512
CONV_C = 31
MLA_HEADS = 8
QK_NOPE = 64
QK_ROPE = 32
V_DIM = 64
Q_RANK = 256
KV_RANK = 128
ROPE_THETA = 10000.0
N_EXPERTS = 32
TOP_K = 4
D_EXPERT = 1024
SWIGLU_LIMIT = 7.0
SWIGLU_ALPHA = 1.702
DEEPNORM_ALPHA = (2.0 * DEPTH) ** 0.25
LN_EPS = 1e-5
RMS_EPS = 1e-6

LANES = 128
SUBLANES = 8
HEAD_PAD = 128
CONV_C_HALO = 32
CONV_A_HALO = 8
CONV_ROWS = 128

T_EVEN = 512
T_ODD = 256
T_POST = 512
T_ATT = 512
ATT_ROWS = 512
T_MOE = 512
BM = 512
SEG = 8
PERM_ROWS = 256
VMEM_LIMIT = 56 * 1024 * 1024


def _cparams(sem):
    return pltpu.CompilerParams(dimension_semantics=sem, vmem_limit_bytes=VMEM_LIMIT)


def _layer_norm(v, g, b):
    mu = jnp.mean(v, axis=-1, keepdims=True)
    d = v - mu
    var = jnp.mean(d * d, axis=-1, keepdims=True)
    return d * lax.rsqrt(var + LN_EPS) * g + b


def _gelu(v):
    return 0.5 * v * (1.0 + lax.erf(v * (1.0 / math.sqrt(2.0))))


def _sigmoid(v):
    return 1.0 / (1.0 + jnp.exp(-v))


def _ada_kernel(c_ref, w_ref, b_ref, o_ref):
    c = c_ref[...]
    cond = c * _sigmoid(c)
    o_ref[0] = jnp.dot(cond, w_ref[0], precision=HIGHEST, preferred_element_type=F32) + b_ref[0]


def _ada(c, ada_w, ada_b):
    bsz, d = c.shape
    depth = ada_w.shape[0]
    n_chunk = ada_w.shape[2] // d
    return pl.pallas_call(
        _ada_kernel,
        grid=(depth, n_chunk),
        in_specs=[
            pl.BlockSpec((bsz, d), lambda l, j: (0, 0)),
            pl.BlockSpec((1, d, d), lambda l, j: (l, 0, j)),
            pl.BlockSpec((1, 1, d), lambda l, j: (l, 0, j)),
        ],
        out_specs=pl.BlockSpec((1, bsz, d), lambda l, j: (l, 0, j)),
        out_shape=jax.ShapeDtypeStruct((depth, bsz, n_chunk * d), F32),
        compiler_params=_cparams(("arbitrary", "arbitrary")),
        name="ada",
    )(c, ada_w, ada_b.reshape(depth, 1, n_chunk * d))


def _rope_kernel(pos_ref, freq_ref, cos_ref, sin_ref):
    ang = pos_ref[...].astype(F32) * freq_ref[...]
    cos_ref[...] = jnp.cos(ang)
    sin_ref[...] = jnp.sin(ang)


def _rope_tables(positions):
    n = positions.size
    t = 1024
    inv_freq = ROPE_THETA ** (-jnp.arange(0, QK_ROPE, 2, dtype=F32) / QK_ROPE)
    half = QK_ROPE // 2
    freq = jnp.zeros((1, LANES), F32).at[0, QK_NOPE:QK_NOPE + QK_ROPE].set(jnp.tile(inv_freq, 2))
    del half
    return pl.pallas_call(
        _rope_kernel,
        grid=(n // t,),
        in_specs=[pl.BlockSpec((t, 1), lambda i: (i, 0)), pl.BlockSpec((1, LANES), lambda i: (0, 0))],
        out_specs=[pl.BlockSpec((t, LANES), lambda i: (i, 0))] * 2,
        out_shape=[jax.ShapeDtypeStruct((n, LANES), F32)] * 2,
        compiler_params=_cparams(("arbitrary",)),
        name="rope",
    )(positions.reshape(n, 1), freq)


def _even_kernel(x_ref, mod_ref, win_ref, cw_ref, sgw_ref, sgb_ref, vng_ref, vnb_ref, wout_ref,
                 lng_ref, lnb_ref, o_ref, gbuf, ybuf):
    t_len = x_ref.shape[1]
    halo = CONV_A_HALO

    @pl.when(pl.program_id(1) == 0)
    def _():
        gbuf[0:halo, :] = jnp.zeros((halo, W_A), F32)

    x = x_ref[0]
    shift, scale, gate = mod_ref[0, 0:1, :], mod_ref[0, 1:2, :], mod_ref[0, 2:3, :]
    u = x * (1.0 + scale) + shift
    proj = jnp.dot(u.astype(BF16), win_ref[...], preferred_element_type=F32)
    b_gate = proj[:, 0:W_A]
    c_gate = proj[:, W_A:2 * W_A]
    xa = proj[:, 2 * W_A:3 * W_A]
    zu = proj[:, 3 * W_A:3 * W_A + W_B]
    zv = proj[:, 3 * W_A + W_B:3 * W_A + 2 * W_B]

    g = c_gate * xa
    gbuf[halo:halo + t_len, :] = g
    conv = (cw_ref[0:1, :] * gbuf[halo - 2:halo - 2 + t_len, :]
            + cw_ref[1:2, :] * gbuf[halo - 1:halo - 1 + t_len, :]
            + cw_ref[2:3, :] * g)
    gbuf[0:halo, :] = g[t_len - halo:t_len, :]
    ybuf[:, 0:W_A] = (b_gate * conv).astype(BF16)

    zu = _gelu(zu)
    zv = _layer_norm(_gelu(zv), vng_ref[...], vnb_ref[...]).astype(BF16)
    row = lax.broadcasted_iota(jnp.int32, (CHUNK, CHUNK), 0)
    col = lax.broadcasted_iota(jnp.int32, (CHUNK, CHUNK), 1)
    w_stack = jnp.concatenate(
        [jnp.where(row >= col, sgw_ref[h], 0.0).astype(BF16) for h in range(SG_HEADS)], axis=0)
    col_head = lax.broadcasted_iota(jnp.int32, (CHUNK, W_B), 1) // SG_HEAD_DIM
    for ci in range(t_len // CHUNK):
        lo = ci * CHUNK
        full = jnp.dot(w_stack, zv[lo:lo + CHUNK, :], preferred_element_type=F32)
        mixed = sgb_ref[...]
        for h in range(SG_HEADS):
            mixed = mixed + jnp.where(col_head == h, full[h * CHUNK:(h + 1) * CHUNK, :], 0.0)
        ybuf[lo:lo + CHUNK, W_A:W_A + W_B] = (zu[lo:lo + CHUNK, :] * mixed).astype(BF16)

    y = jnp.dot(ybuf[...], wout_ref[...], preferred_element_type=F32)
    r = DEEPNORM_ALPHA * x + (1.0 + gate) * y
    o_ref[0] = _layer_norm(r, lng_ref[...], lnb_ref[...])


def _even_layer(x, mod_l, w_in, conv_w, sg_w, sg_b, vn_g, vn_b, w_out, ln_g, ln_b):
    bsz, seq, d = x.shape
    t = min(T_EVEN, seq)
    sgb_full = jnp.repeat(sg_b.T, SG_HEAD_DIM, axis=1)
    full = lambda shape: pl.BlockSpec(shape, lambda b, i: (0,) * len(shape))
    return pl.pallas_call(
        _even_kernel,
        grid=(bsz, seq // t),
        in_specs=[
            pl.BlockSpec((1, t, d), lambda b, i: (b, i, 0)),
            pl.BlockSpec((1, 6, d), lambda b, i: (b, 0, 0)),
            full(w_in.shape), full(conv_w.shape), full(sg_w.shape), full(sgb_full.shape),
            full((1, W_B)), full((1, W_B)), full(w_out.shape), full((1, d)), full((1, d)),
        ],
        out_specs=pl.BlockSpec((1, t, d), lambda b, i: (b, i, 0)),
        out_shape=jax.ShapeDtypeStruct((bsz, seq, d), F32),
        scratch_shapes=[pltpu.VMEM((CONV_A_HALO + t, W_A), F32), pltpu.VMEM((t, W_A + W_B), BF16)],
        compiler_params=_cparams(("arbitrary", "arbitrary")),
        name="even",
    )(x, mod_l, w_in.astype(BF16), conv_w, sg_w, sgb_full, vn_g.reshape(1, W_B), vn_b.reshape(1, W_B),
      w_out.astype(BF16), ln_g.reshape(1, d), ln_b.reshape(1, d))


N_GLU = 2 * W_C
OFF_Q = N_GLU
OFF_KV = OFF_Q + Q_RANK
OFF_KR = OFF_KV + KV_RANK
OFF_KR_ROT = OFF_KR + HEAD_PAD
ODD_IN_PAD = OFF_KR_ROT + HEAD_PAD


def _odd_pre_kernel(x_ref, mod_ref, cos_ref, sin_ref, win_ref, dww_ref, dwb_ref, cng_ref, cnb_ref,
                    qng_ref, wq_ref, wqr_ref, kvng_ref, wk_ref, wv_ref,
                    yc_ref, q_ref, k_ref, v_ref, gbuf, sbuf, hbuf):
    t_len = x_ref.shape[1]
    halo = CONV_C_HALO

    @pl.when(pl.program_id(1) == 0)
    def _():
        gbuf[0:halo, :] = jnp.zeros((halo, W_C), F32)

    x = x_ref[0]
    shift, scale = mod_ref[0, 0:1, :], mod_ref[0, 1:2, :]
    u = x * (1.0 + scale) + shift
    proj = jnp.dot(u.astype(BF16), win_ref[...], preferred_element_type=F32)

    g = proj[:, 0:W_C] * _sigmoid(proj[:, W_C:2 * W_C])
    gbuf[halo:halo + t_len, :] = g
    first = halo - (CONV_C - 1)
    for res in range(1, SUBLANES):
        span = max(first + k - res for k in range(CONV_C) if (first + k) % SUBLANES == res) + t_len
        sbuf[res - 1, 0:span, :] = gbuf[res:res + span, :]
    for rb in range(t_len // CONV_ROWS):
        for cb in range(W_C // LANES):
            cs = slice(cb * LANES, (cb + 1) * LANES)
            acc = jnp.broadcast_to(dwb_ref[:, cs], (CONV_ROWS, LANES))
            for k in range(CONV_C):
                res = (first + k) % SUBLANES
                lo = first + k - res + rb * CONV_ROWS
                tap = gbuf[lo:lo + CONV_ROWS, cs] if res == 0 else sbuf[res - 1, lo:lo + CONV_ROWS, cs]
                acc = acc + dww_ref[k:k + 1, cs] * tap
            hbuf[rb * CONV_ROWS:(rb + 1) * CONV_ROWS, cs] = acc
    acc = hbuf[...]
    gbuf[0:halo, :] = g[t_len - halo:t_len, :]
    hn = _layer_norm(acc, cng_ref[...], cnb_ref[...])
    yc_ref[0] = (hn * _sigmoid(hn)).astype(BF16)

    cos = cos_ref[...]
    sin = sin_ref[...]
    cq = proj[:, OFF_Q:OFF_Q + Q_RANK]
    q_lat = (cq * lax.rsqrt(jnp.mean(cq * cq, axis=-1, keepdims=True) + RMS_EPS) * qng_ref[...]).astype(BF16)
    ckv = proj[:, OFF_KV:OFF_KV + KV_RANK]
    kv_lat = (ckv * lax.rsqrt(jnp.mean(ckv * ckv, axis=-1, keepdims=True) + RMS_EPS) * kvng_ref[...]).astype(BF16)
    q_all = jnp.dot(q_lat, wq_ref[...], preferred_element_type=F32)
    q_rot = jnp.dot(q_lat, wqr_ref[...], preferred_element_type=F32)
    k_all = jnp.dot(kv_lat, wk_ref[...], preferred_element_type=F32)
    v_all = jnp.dot(kv_lat, wv_ref[...], preferred_element_type=F32)
    k_rope = proj[:, OFF_KR:OFF_KR + HEAD_PAD] * cos + proj[:, OFF_KR_ROT:OFF_KR_ROT + HEAD_PAD] * sin
    sm_scale = math.log2(math.e) / math.sqrt(QK_NOPE + QK_ROPE)
    for h in range(MLA_HEADS):
        sl = slice(h * HEAD_PAD, (h + 1) * HEAD_PAD)
        q_ref[0, h] = ((q_all[:, sl] * cos + q_rot[:, sl] * sin) * sm_scale).astype(BF16)
        k_ref[0, h] = (k_all[:, sl] + k_rope).astype(BF16)
    for hp in range(MLA_HEADS // 2):
        v_ref[0, hp] = v_all[:, hp * LANES:(hp + 1) * LANES].astype(BF16)


def _odd_weights(w_in, w_uq, w_ukv):
    d = w_in.shape[0]
    half = QK_ROPE // 2
    kr = w_in[:, OFF_KR:OFF_KR + QK_ROPE]
    z = lambda n: jnp.zeros((d, n), w_in.dtype)
    kr_blk = jnp.concatenate([z(QK_NOPE), kr, z(HEAD_PAD - QK_NOPE - QK_ROPE)], axis=1)
    kr_rot = jnp.concatenate([z(QK_NOPE), -kr[:, half:], kr[:, :half], z(HEAD_PAD - QK_NOPE - QK_ROPE)], axis=1)
    w_in_p = jnp.concatenate([w_in[:, :OFF_KR], kr_blk, kr_rot], axis=1).astype(BF16)

    dq = QK_NOPE + QK_ROPE
    wq = w_uq.reshape(Q_RANK, MLA_HEADS, dq)
    zq = lambda n: jnp.zeros((Q_RANK, MLA_HEADS, n), w_uq.dtype)
    wq_p = jnp.concatenate([wq, zq(HEAD_PAD - dq)], axis=2).reshape(Q_RANK, MLA_HEADS * HEAD_PAD)
    wq_r = jnp.concatenate([zq(QK_NOPE), -wq[:, :, QK_NOPE + half:], wq[:, :, QK_NOPE:QK_NOPE + half],
                            zq(HEAD_PAD - dq)], axis=2).reshape(Q_RANK, MLA_HEADS * HEAD_PAD)

    wkv = w_ukv.reshape(KV_RANK, MLA_HEADS, QK_NOPE + V_DIM)
    wk_p = jnp.concatenate([wkv[:, :, :QK_NOPE], jnp.zeros((KV_RANK, MLA_HEADS, HEAD_PAD - QK_NOPE), w_ukv.dtype)],
                           axis=2).reshape(KV_RANK, MLA_HEADS * HEAD_PAD)
    wv_p = wkv[:, :, QK_NOPE:].reshape(KV_RANK, MLA_HEADS * V_DIM)
    return w_in_p, wq_p.astype(BF16), wq_r.astype(BF16), wk_p.astype(BF16), wv_p.astype(BF16)


def _odd_pre(x, mod_l, cos_t, sin_t, w_in, dw_w, dw_b, cn_g, cn_b, qn_g, w_uq, kvn_g, w_ukv):
    bsz, seq, d = x.shape
    t = min(T_ODD, seq)
    nt = seq // t
    w_in_p, wq_p, wq_r, wk_p, wv_p = _odd_weights(w_in, w_uq, w_ukv)
    full = lambda shape: pl.BlockSpec(shape, lambda b, i: (0,) * len(shape))
    return pl.pallas_call(
        _odd_pre_kernel,
        grid=(bsz, nt),
        in_specs=[
            pl.BlockSpec((1, t, d), lambda b, i: (b, i, 0)),
            pl.BlockSpec((1, 6, d), lambda b, i: (b, 0, 0)),
            pl.BlockSpec((t, LANES), lambda b, i: (b * nt + i, 0)),
            pl.BlockSpec((t, LANES), lambda b, i: (b * nt + i, 0)),
            full(w_in_p.shape), full(dw_w.shape), full((1, W_C)), full((1, W_C)), full((1, W_C)),
            full((1, Q_RANK)), full(wq_p.shape), full(wq_r.shape), full((1, KV_RANK)),
            full(wk_p.shape), full(wv_p.shape),
        ],
        out_specs=[
            pl.BlockSpec((1, t, W_C), lambda b, i: (b, i, 0)),
            pl.BlockSpec((1, MLA_HEADS, t, HEAD_PAD), lambda b, i: (b, 0, i, 0)),
            pl.BlockSpec((1, MLA_HEADS, t, HEAD_PAD), lambda b, i: (b, 0, i, 0)),
            pl.BlockSpec((1, MLA_HEADS // 2, t, LANES), lambda b, i: (b, 0, i, 0)),
        ],
        out_shape=[
            jax.ShapeDtypeStruct((bsz, seq, W_C), BF16),
            jax.ShapeDtypeStruct((bsz, MLA_HEADS, seq, HEAD_PAD), BF16),
            jax.ShapeDtypeStruct((bsz, MLA_HEADS, seq, HEAD_PAD), BF16),
            jax.ShapeDtypeStruct((bsz, MLA_HEADS // 2, seq, LANES), BF16),
        ],
        scratch_shapes=[pltpu.VMEM((CONV_C_HALO + t, W_C), F32),
                        pltpu.VMEM((SUBLANES - 1, CONV_C_HALO + t, W_C), F32), pltpu.VMEM((t, W_C), F32)],
        compiler_params=_cparams(("arbitrary", "arbitrary")),
        name="odd_pre",
    )(x, mod_l, cos_t, sin_t, w_in_p, dw_w, dw_b.reshape(1, W_C), cn_g.reshape(1, W_C), cn_b.reshape(1, W_C),
      qn_g.reshape(1, Q_RANK), wq_p, wq_r, kvn_g.reshape(1, KV_RANK), wk_p, wv_p)


def _attn_kernel(qi_tab, kj_tab, q_ref, k_ref, v_ref, o_ref, m_sc, l_sc, acc_sc):
    step_id = pl.program_id(2)
    qi = qi_tab[step_id]
    kj = kj_tab[step_id]
    tq = q_ref.shape[2]
    tk = k_ref.shape[2]
    n_col = tk // LANES

    @pl.when(kj == 0)
    def _():
        m_sc[...] = jnp.full(m_sc.shape, -jnp.inf, F32)
        l_sc[...] = jnp.zeros(l_sc.shape, F32)
        acc_sc[...] = jnp.zeros(acc_sc.shape, F32)

    first_lanes = lax.broadcasted_iota(jnp.int32, (ATT_ROWS, LANES), 1) < V_DIM

    def step(masked):
        v_pair = v_ref[0, 0]
        m_old = [m_sc[0], m_sc[1]]
        l_old = [l_sc[0], l_sc[1]]
        acc_old = acc_sc[...]
        m_new, l_new, acc_new = [[], []], [[], []], []
        for r in range(tq // ATT_ROWS):
            rs = slice(r * ATT_ROWS, (r + 1) * ATT_ROWS)
            alphas = []
            pvs = []
            for a in range(2):
                s = lax.dot_general(q_ref[0, a, rs, :], k_ref[0, a], (((1,), (1,)), ((), ())),
                                    preferred_element_type=F32)
                if masked:
                    rows = r * ATT_ROWS + lax.broadcasted_iota(jnp.int32, (ATT_ROWS, tk), 0)
                    cols = lax.broadcasted_iota(jnp.int32, (ATT_ROWS, tk), 1)
                    s = jnp.where(cols <= rows, s, -1e30)
                cols_s = [s[:, c * LANES:(c + 1) * LANES] for c in range(n_col)]
                m_lane = cols_s[0]
                for c in range(1, n_col):
                    m_lane = jnp.maximum(m_lane, cols_s[c])
                m_prev = m_old[a][rs, :]
                m_next = jnp.maximum(m_prev, jnp.max(m_lane, axis=1, keepdims=True))
                alpha = jnp.exp2(m_prev - m_next)
                ps = [jnp.exp2(cs - m_next) for cs in cols_s]
                l_lane = ps[0]
                for c in range(1, n_col):
                    l_lane = l_lane + ps[c]
                l_new[a].append(alpha * l_old[a][rs, :] + l_lane)
                m_new[a].append(m_next)
                alphas.append(alpha)
                p = jnp.concatenate([pc.astype(BF16) for pc in ps], axis=1)
                pvs.append(jnp.dot(p, v_pair, preferred_element_type=F32))
            alpha_sel = jnp.where(first_lanes, alphas[0], alphas[1])
            acc_new.append(alpha_sel * acc_old[rs, :] + jnp.where(first_lanes, pvs[0], pvs[1]))
        for a in range(2):
            m_sc[a] = jnp.concatenate(m_new[a], axis=0)
            l_sc[a] = jnp.concatenate(l_new[a], axis=0)
        acc_sc[...] = jnp.concatenate(acc_new, axis=0)

    @pl.when(kj < qi)
    def _():
        step(False)

    @pl.when(kj == qi)
    def _():
        step(True)
        first_all = lax.broadcasted_iota(jnp.int32, (tq, LANES), 1) < V_DIM
        l_sel = jnp.where(first_all, jnp.sum(l_sc[0], axis=1, keepdims=True),
                          jnp.sum(l_sc[1], axis=1, keepdims=True))
        o_ref[0] = (acc_sc[...] / l_sel).astype(o_ref.dtype)


def _attention(q, k, v):
    bsz, heads, seq, _ = q.shape
    t = min(T_ATT, seq)
    nq = seq // t
    pairs = [(i, j) for i in range(nq) for j in range(i + 1)]
    qi_tab = jnp.asarray([p[0] for p in pairs], jnp.int32)
    kj_tab = jnp.asarray([p[1] for p in pairs], jnp.int32)
    grid_spec = pltpu.PrefetchScalarGridSpec(
        num_scalar_prefetch=2,
        grid=(bsz, heads // 2, len(pairs)),
        in_specs=[
            pl.BlockSpec((1, 2, t, HEAD_PAD), lambda b, hp, s, qt, kt: (b, hp, qt[s], 0)),
            pl.BlockSpec((1, 2, t, HEAD_PAD), lambda b, hp, s, qt, kt: (b, hp, kt[s], 0)),
            pl.BlockSpec((1, 1, t, LANES), lambda b, hp, s, qt, kt: (b, hp, kt[s], 0)),
        ],
        out_specs=pl.BlockSpec((1, t, LANES), lambda b, hp, s, qt, kt: (b, qt[s], hp)),
        scratch_shapes=[pltpu.VMEM((2, t, LANES), F32), pltpu.VMEM((2, t, LANES), F32),
                        pltpu.VMEM((t, LANES), F32)],
    )
    return pl.pallas_call(
        _attn_kernel,
        grid_spec=grid_spec,
        out_shape=jax.ShapeDtypeStruct((bsz, seq, heads * V_DIM), BF16),
        compiler_params=_cparams(("arbitrary", "arbitrary", "arbitrary")),
        name="attn",
    )(qi_tab, kj_tab, q, k, v)


def _odd_post_kernel(x_ref, mod_ref, yc_ref, yd_ref, wout_ref, lng_ref, lnb_ref, o_ref):
    x = x_ref[0]
    gate = mod_ref[0, 2:3, :]
    y = (jnp.dot(yc_ref[0], wout_ref[0:W_C, :], preferred_element_type=F32)
         + jnp.dot(yd_ref[0], wout_ref[W_C:, :], preferred_element_type=F32))
    r = DEEPNORM_ALPHA * x + (1.0 + gate) * y
    o_ref[0] = _layer_norm(r, lng_ref[...], lnb_ref[...])


def _odd_post(x, mod_l, y_c, y_d, w_out, ln_g, ln_b):
    bsz, seq, d = x.shape
    t = min(T_POST, seq)
    full = lambda shape: pl.BlockSpec(shape, lambda b, i: (0,) * len(shape))
    return pl.pallas_call(
        _odd_post_kernel,
        grid=(bsz, seq // t),
        in_specs=[
            pl.BlockSpec((1, t, d), lambda b, i: (b, i, 0)),
            pl.BlockSpec((1, 6, d), lambda b, i: (b, 0, 0)),
            pl.BlockSpec((1, t, W_C), lambda b, i: (b, i, 0)),
            pl.BlockSpec((1, t, MLA_HEADS * V_DIM), lambda b, i: (b, i, 0)),
            full(w_out.shape), full((1, d)), full((1, d)),
        ],
        out_specs=pl.BlockSpec((1, t, d), lambda b, i: (b, i, 0)),
        out_shape=jax.ShapeDtypeStruct((bsz, seq, d), F32),
        compiler_params=_cparams(("arbitrary", "arbitrary")),
        name="odd_post",
    )(x, mod_l, y_c, y_d, w_out.astype(BF16), ln_g.reshape(1, d), ln_b.reshape(1, d))


def _router_kernel(x_ref, mod_ref, rw_ref, rb_ref, topi_ref, gate_ref, lrank_ref, cnt_ref):
    t_len = x_ref.shape[1]
    x = x_ref[0]
    shift, scale = mod_ref[0, 3:4, :], mod_ref[0, 4:5, :]
    u = x * (1.0 + scale) + shift
    logits = lax.dot_general(rw_ref[...], u, (((1,), (1,)), ((), ())), precision=HIGHEST,
                             preferred_element_type=F32) + rb_ref[...]
    eidx = lax.broadcasted_iota(jnp.int32, (N_EXPERTS, t_len), 0)
    vals, hots = [], []
    for _ in range(TOP_K):
        m = jnp.max(logits, axis=0, keepdims=True)
        idx = jnp.min(jnp.where(logits == m, eidx, N_EXPERTS), axis=0, keepdims=True)
        hot = eidx == idx
        vals.append(m)
        hots.append(hot)
        logits = jnp.where(hot, -jnp.inf, logits)
        topi_ref[len(vals) - 1:len(vals), :] = idx
    exps = [jnp.exp(v - vals[0]) for v in vals]
    denom = exps[0] + exps[1] + exps[2] + exps[3]
    for kk in range(TOP_K):
        gate_ref[kk:kk + 1, :] = exps[kk] / denom

    chosen = jnp.where(hots[0] | hots[1] | hots[2] | hots[3], 1.0, 0.0)
    r_i = lax.broadcasted_iota(jnp.int32, (t_len, t_len), 0)
    c_i = lax.broadcasted_iota(jnp.int32, (t_len, t_len), 1)
    before = jnp.where(r_i < c_i, 1.0, 0.0).astype(BF16)
    base = jnp.dot(chosen.astype(BF16), before, preferred_element_type=F32)
    for kk in range(TOP_K):
        lrank_ref[kk:kk + 1, :] = jnp.sum(jnp.where(hots[kk], base, 0.0), axis=0, keepdims=True).astype(jnp.int32)
    cnt_ref[0] = jnp.broadcast_to(jnp.sum(chosen, axis=1, keepdims=True), cnt_ref.shape[1:])


def _router(x, mod_l, router_w, router_b):
    bsz, seq, d = x.shape
    t = min(T_MOE, seq)
    nt = seq // t
    n_tok = bsz * seq
    full = lambda shape: pl.BlockSpec(shape, lambda b, i: (0,) * len(shape))
    tok_spec = pl.BlockSpec((TOP_K, t), lambda b, i: (0, b * nt + i))
    return pl.pallas_call(
        _router_kernel,
        grid=(bsz, nt),
        in_specs=[
            pl.BlockSpec((1, t, d), lambda b, i: (b, i, 0)),
            pl.BlockSpec((1, 6, d), lambda b, i: (b, 0, 0)),
            full((N_EXPERTS, d)), full((N_EXPERTS, 1)),
        ],
        out_specs=[tok_spec, tok_spec, tok_spec,
                   pl.BlockSpec((1, N_EXPERTS, LANES), lambda b, i: (b * nt + i, 0, 0))],
        out_shape=[
            jax.ShapeDtypeStruct((TOP_K, n_tok), jnp.int32),
            jax.ShapeDtypeStruct((TOP_K, n_tok), F32),
            jax.ShapeDtypeStruct((TOP_K, n_tok), jnp.int32),
            jax.ShapeDtypeStruct((bsz * nt, N_EXPERTS, LANES), F32),
        ],
        compiler_params=_cparams(("arbitrary", "arbitrary")),
        name="router",
    )(x, mod_l, router_w.T, router_b.reshape(N_EXPERTS, 1))


def _tile_index():
    return pl.program_id(0) * pl.num_programs(1) + pl.program_id(1)


def _for_each_segment_chunk(tile, p_ref, loc0_ref, seg_ref, fn):
    def per_expert(e, carry):
        idx = tile * N_EXPERTS + e
        src0 = loc0_ref[idx]
        dst0 = seg_ref[idx]

        def chunk(cc, c2):
            fn(pl.multiple_of(src0 + cc * SEG, SEG), pl.multiple_of(dst0 + cc * SEG, SEG))
            return c2

        lax.fori_loop(0, p_ref[idx] // SEG, chunk, 0)
        return carry

    lax.fori_loop(0, N_EXPERTS, per_expert, 0)


def _dispatch_kernel(p_ref, loc0_ref, seg_ref, used_ref, tstart_ref, tlen_ref,
                     x_ref, mod_ref, loc_ref, xs_ref, pm_buf, lbuf, zbuf, sem):
    tile = _tile_index()
    n_tiles = pl.num_programs(0) * pl.num_programs(1)
    t_len = x_ref.shape[1]
    n_loc = lbuf.shape[0]
    x = x_ref[0]
    shift, scale = mod_ref[0, 3:4, :], mod_ref[0, 4:5, :]
    u = (x * (1.0 + scale) + shift).astype(BF16)
    loc = loc_ref[...]
    for jc in range(n_loc // PERM_ROWS):
        j = jc * PERM_ROWS + lax.broadcasted_iota(jnp.int32, (PERM_ROWS, t_len), 0)
        pm = jnp.zeros((PERM_ROWS, t_len), F32)
        for kk in range(TOP_K):
            pm = jnp.where(j == loc[kk:kk + 1, :], 1.0, pm)
        pm_buf[jc * PERM_ROWS:(jc + 1) * PERM_ROWS, :] = pm.astype(BF16)
    lbuf[...] = jnp.dot(pm_buf[...], u, preferred_element_type=F32)

    def seg_copy(src_row, dst_row):
        return pltpu.make_async_copy(lbuf.at[pl.ds(src_row, SEG)], xs_ref.at[pl.ds(dst_row, SEG)], sem)

    _for_each_segment_chunk(tile, p_ref, loc0_ref, seg_ref, lambda sr, dr: seg_copy(sr, dr).start())

    def drain(cc, c2):
        seg_copy(0, 0).wait()
        return c2

    lax.fori_loop(0, used_ref[tile] // SEG, drain, 0)

    @pl.when(tile == n_tiles - 1)
    def _():
        zbuf[...] = jnp.zeros(zbuf.shape, F32)

        def fill(e, rows):
            def zero_copy(cc):
                dst = pl.multiple_of(tstart_ref[e] + cc * rows, SEG)
                return pltpu.make_async_copy(zbuf.at[pl.ds(0, rows)], xs_ref.at[pl.ds(dst, rows)], sem)

            n = tlen_ref[e] // rows

            def start(cc, c2):
                zero_copy(cc).start()
                return c2

            def wait(cc, c2):
                zero_copy(cc).wait()
                return c2

            lax.fori_loop(0, n, start, 0)
            lax.fori_loop(0, n, wait, 0)

        def per_expert(e, carry):
            fill(e, SEG)
            return carry

        lax.fori_loop(0, N_EXPERTS, per_expert, 0)
        fill(N_EXPERTS, BM)


def _dispatch(x, mod_l, loc, plan, n_slots):
    bsz, seq, d = x.shape
    t = min(T_MOE, seq)
    nt = seq // t
    n_loc = TOP_K * t + N_EXPERTS * SEG
    grid_spec = pltpu.PrefetchScalarGridSpec(
        num_scalar_prefetch=6,
        grid=(bsz, nt),
        in_specs=[
            pl.BlockSpec((1, t, d), lambda b, i, *_: (b, i, 0)),
            pl.BlockSpec((1, 6, d), lambda b, i, *_: (b, 0, 0)),
            pl.BlockSpec((TOP_K, t), lambda b, i, *_: (0, b * nt + i)),
        ],
        out_specs=pl.BlockSpec(memory_space=pl.ANY),
        scratch_shapes=[pltpu.VMEM((n_loc, t), BF16), pltpu.VMEM((n_loc, d), F32), pltpu.VMEM((BM, d), F32),
                        pltpu.SemaphoreType.DMA(())],
    )
    return pl.pallas_call(
        _dispatch_kernel,
        grid_spec=grid_spec,
        out_shape=jax.ShapeDtypeStruct((n_slots, d), F32),
        compiler_params=_cparams(("arbitrary", "arbitrary")),
        name="dispatch",
    )(plan["p"], plan["loc0"], plan["seg"], plan["used"], plan["tstart"], plan["tlen"], x, mod_l, loc)


def _expert_kernel(bexp_ref, nused_ref, xs_ref, wgu_ref, bgu_ref, wdn_ref, bdn_ref, ys_ref, wgu_bf, wdn_bf):
    i = pl.program_id(0)
    active = i < nused_ref[0]
    new_expert = (i == 0) | (bexp_ref[i] != bexp_ref[jnp.maximum(i - 1, 0)])

    @pl.when(active & new_expert)
    def _():
        wgu_bf[...] = wgu_ref[0, 0].astype(BF16)
        wdn_bf[...] = wdn_ref[0, 0].astype(BF16)

    @pl.when(active)
    def _():
        gu = jnp.dot(xs_ref[...].astype(BF16), wgu_bf[...], preferred_element_type=F32) + bgu_ref[0, 0]
        gate = jnp.minimum(gu[:, :D_EXPERT], SWIGLU_LIMIT)
        up = jnp.clip(gu[:, D_EXPERT:], -SWIGLU_LIMIT, SWIGLU_LIMIT)
        glu = gate * _sigmoid(SWIGLU_ALPHA * gate)
        h = ((up + 1.0) * glu).astype(BF16)
        ys_ref[...] = jnp.dot(h, wdn_bf[...], preferred_element_type=F32) + bdn_ref[0, 0]

    @pl.when(jnp.logical_not(active))
    def _():
        ys_ref[...] = jnp.zeros(ys_ref.shape, F32)


def _experts(xs, block_exp, n_used, layer, w_gu, b_gu, w_dn, b_dn):
    n_slots, d = xs.shape
    n_blocks = n_slots // BM
    depth = w_gu.shape[0]
    row_map = lambda i, be, nu: (jnp.minimum(i, nu[0] - 1), 0)
    exp_map = lambda i, be, nu: (layer, be[i], 0, 0)
    grid_spec = pltpu.PrefetchScalarGridSpec(
        num_scalar_prefetch=2,
        grid=(n_blocks,),
        in_specs=[
            pl.BlockSpec((BM, d), row_map),
            pl.BlockSpec((1, 1, d, 2 * D_EXPERT), exp_map),
            pl.BlockSpec((1, 1, 1, 2 * D_EXPERT), exp_map),
            pl.BlockSpec((1, 1, D_EXPERT, d), exp_map),
            pl.BlockSpec((1, 1, 1, d), exp_map),
        ],
        out_specs=pl.BlockSpec((BM, d), lambda i, be, nu: (i, 0)),
        scratch_shapes=[pltpu.VMEM((d, 2 * D_EXPERT), BF16), pltpu.VMEM((D_EXPERT, d), BF16)],
    )
    return pl.pallas_call(
        _expert_kernel,
        grid_spec=grid_spec,
        out_shape=jax.ShapeDtypeStruct((n_slots, d), F32),
        compiler_params=_cparams(("arbitrary",)),
        name="experts",
    )(block_exp, n_used, xs, w_gu, b_gu.reshape(depth, N_EXPERTS, 1, 2 * D_EXPERT), w_dn,
      b_dn.reshape(depth, N_EXPERTS, 1, d))


def _combine_kernel(p_ref, loc0_ref, seg_ref, used_ref, x_ref, mod_ref, loct_ref, gatet_ref, ys_ref,
                    lng_ref, lnb_ref, o_ref, ybuf, sel_buf, sem):
    tile = _tile_index()
    t_len = x_ref.shape[1]
    n_loc = ybuf.shape[0]
    used = used_ref[tile]

    def seg_copy(src_row, dst_row):
        return pltpu.make_async_copy(ys_ref.at[pl.ds(dst_row, SEG)], ybuf.at[pl.ds(src_row, SEG)], sem)

    _for_each_segment_chunk(tile, p_ref, loc0_ref, seg_ref, lambda sr, dr: seg_copy(sr, dr).start())

    def zero_rows(cc, c2):
        ybuf[pl.ds(pl.multiple_of(used + cc * SEG, SEG), SEG), :] = jnp.zeros((SEG, ybuf.shape[1]), F32)
        return c2

    lax.fori_loop(0, (n_loc - used) // SEG, zero_rows, 0)

    loct = loct_ref[...]
    gatet = gatet_ref[...]
    for jc in range(n_loc // LANES):
        j = jc * LANES + lax.broadcasted_iota(jnp.int32, (t_len, LANES), 1)
        sel = jnp.zeros((t_len, LANES), F32)
        for kk in range(TOP_K):
            sel = jnp.where(j == loct[:, kk:kk + 1], gatet[:, kk:kk + 1], sel)
        sel_buf[:, jc * LANES:(jc + 1) * LANES] = sel.astype(BF16)

    def drain(cc, c2):
        seg_copy(0, 0).wait()
        return c2

    lax.fori_loop(0, used // SEG, drain, 0)

    y = jnp.dot(sel_buf[...], ybuf[...].astype(BF16), preferred_element_type=F32)
    x = x_ref[0]
    gate_f = mod_ref[0, 5:6, :]
    r = DEEPNORM_ALPHA * x + (1.0 + gate_f) * y
    o_ref[0] = _layer_norm(r, lng_ref[...], lnb_ref[...])


def _combine(x, mod_l, loc_t, gates_t, plan, ys, ln_g, ln_b):
    bsz, seq, d = x.shape
    t = min(T_MOE, seq)
    nt = seq // t
    n_loc = TOP_K * t + N_EXPERTS * SEG
    full = lambda shape: pl.BlockSpec(shape, lambda b, i, *_: (0,) * len(shape))
    grid_spec = pltpu.PrefetchScalarGridSpec(
        num_scalar_prefetch=4,
        grid=(bsz, nt),
        in_specs=[
            pl.BlockSpec((1, t, d), lambda b, i, *_: (b, i, 0)),
            pl.BlockSpec((1, 6, d), lambda b, i, *_: (b, 0, 0)),
            pl.BlockSpec((t, TOP_K), lambda b, i, *_: (b * nt + i, 0)),
            pl.BlockSpec((t, TOP_K), lambda b, i, *_: (b * nt + i, 0)),
            pl.BlockSpec(memory_space=pl.ANY),
            full((1, d)), full((1, d)),
        ],
        out_specs=pl.BlockSpec((1, t, d), lambda b, i, *_: (b, i, 0)),
        scratch_shapes=[pltpu.VMEM((n_loc, d), F32), pltpu.VMEM((t, n_loc), BF16), pltpu.SemaphoreType.DMA(())],
    )
    return pl.pallas_call(
        _combine_kernel,
        grid_spec=grid_spec,
        out_shape=jax.ShapeDtypeStruct((bsz, seq, d), F32),
        compiler_params=_cparams(("arbitrary", "arbitrary")),
        name="combine",
    )(plan["p"], plan["loc0"], plan["seg"], plan["used"], x, mod_l, loc_t, gates_t, ys,
      ln_g.reshape(1, d), ln_b.reshape(1, d))


def _slots_kernel(loc0_ref, topi_ref, lrank_ref, loc_ref):
    tile = pl.program_id(0)
    topi = topi_ref[...]
    loc = lrank_ref[...]
    for e in range(N_EXPERTS):
        loc = loc + jnp.where(topi == e, loc0_ref[tile * N_EXPERTS + e], 0)
    loc_ref[...] = loc


def _slots(loc0, topi, lrank, t):
    k, n = topi.shape
    spec = pl.BlockSpec((k, t), lambda i, l0: (0, i))
    return pl.pallas_call(
        _slots_kernel,
        grid_spec=pltpu.PrefetchScalarGridSpec(num_scalar_prefetch=1, grid=(n // t,), in_specs=[spec, spec],
                                               out_specs=spec),
        out_shape=jax.ShapeDtypeStruct((k, n), jnp.int32),
        compiler_params=_cparams(("arbitrary",)),
        name="slots",
    )(loc0, topi, lrank)


def _moe_plan(cnt, n_slots):
    c = cnt[:, :, 0].astype(jnp.int32)
    p = (c + SEG - 1) // SEG * SEG
    tot = jnp.sum(p, axis=0)
    padded = (tot + BM - 1) // BM * BM
    pad_end = jnp.cumsum(padded)
    pad_start = pad_end - padded
    seg = pad_start[None, :] + jnp.cumsum(p, axis=0) - p
    loc0 = jnp.cumsum(p, axis=1) - p
    n_blocks = n_slots // BM
    block_lo = jnp.arange(n_blocks, dtype=jnp.int32) * BM
    block_exp = jnp.minimum(jnp.sum((pad_end[None, :] <= block_lo[:, None]).astype(jnp.int32), axis=1),
                            N_EXPERTS - 1)
    return {
        "p": p.reshape(-1), "loc0": loc0.reshape(-1), "seg": seg.reshape(-1), "used": jnp.sum(p, axis=1),
        "tstart": jnp.concatenate([pad_start + tot, pad_end[-1:]]),
        "tlen": jnp.concatenate([padded - tot, n_slots - pad_end[-1:]]),
        "n_used": pad_end[-1:] // BM, "block_exp": block_exp,
    }


def _moe_layer(x, mod_l, layer, router_w, router_b, w_gu, b_gu, w_dn, b_dn, ln_g, ln_b):
    bsz, seq, d = x.shape
    n_tok = bsz * seq
    t = min(T_MOE, seq)
    n_slots = n_tok * TOP_K + (n_tok // t) * N_EXPERTS * SEG + N_EXPERTS * BM
    topi, gates, lrank, cnt = _router(x, mod_l, router_w, router_b)
    plan = _moe_plan(cnt, n_slots)
    loc = _slots(plan["loc0"], topi, lrank, t)
    xs = _dispatch(x, mod_l, loc, plan, n_slots)
    ys = _experts(xs, plan["block_exp"], plan["n_used"], layer, w_gu, b_gu, w_dn, b_dn)
    return _combine(x, mod_l, loc.T, gates.T, plan, ys, ln_g, ln_b)


def kernel(x, c, positions, ada_w, ada_b, ln_mix_g, ln_mix_b, ln_ffn_g, ln_ffn_b, ev_w_in, ev_conv_w, ev_sg_w, ev_sg_b, ev_vn_g, ev_vn_b, ev_w_out, od_w_in, od_dw_w, od_dw_b, od_cn_g, od_cn_b, od_qn_g, od_w_uq, od_kvn_g, od_w_ukv, od_w_out, moe_router_w, moe_router_b, moe_w_gu, moe_b_gu, moe_w_dn, moe_b_dn):
    bsz, seq, d = x.shape
    depth = ada_w.shape[0]
    mod = _ada(c, ada_w, ada_b).reshape(depth, bsz, 6, d)
    cos_t, sin_t = _rope_tables(positions)
    for layer in range(depth):
        i = layer // 2
        mod_l = mod[layer]
        if layer % 2 == 0:
            x = _even_layer(x, mod_l, ev_w_in[i], ev_conv_w[i], ev_sg_w[i], ev_sg_b[i], ev_vn_g[i], ev_vn_b[i],
                            ev_w_out[i], ln_mix_g[layer], ln_mix_b[layer])
        else:
            y_c, q, k, v = _odd_pre(x, mod_l, cos_t, sin_t, od_w_in[i], od_dw_w[i], od_dw_b[i], od_cn_g[i],
                                    od_cn_b[i], od_qn_g[i], od_w_uq[i], od_kvn_g[i], od_w_ukv[i])
            y_d = _attention(q, k, v)
            x = _odd_post(x, mod_l, y_c, y_d, od_w_out[i], ln_mix_g[layer], ln_mix_b[layer])
        x = _moe_layer(x, mod_l, layer, moe_router_w[layer], moe_router_b[layer], moe_w_gu, moe_b_gu,
                       moe_w_dn, moe_b_dn, ln_ffn_g[layer], ln_ffn_b[layer])
    return x
```

```python
import functools
import math

import jax
import jax.numpy as jnp
from jax import lax
from jax.experimental import pallas as pl
from jax.experimental.pallas import tpu as pltpu

F32 = jnp.float32
BF16 = jnp.bfloat16
HIGHEST = lax.Precision.HIGHEST

D_MODEL = 1024
DEPTH = 4
W_A = 512
W_B = 512
SG_HEADS = 8
SG_HEAD_DIM = 64
CHUNK = 128
CONV_A = 3
W_C = 512
CONV_C = 31
MLA_HEADS = 8
QK_NOPE = 64
QK_ROPE = 32
V_DIM = 64
Q_RANK = 256
KV_RANK = 128
ROPE_THETA = 10000.0
N_EXPERTS = 32
TOP_K = 4
D_EXPERT = 1024
SWIGLU_LIMIT = 7.0
SWIGLU_ALPHA = 1.702
DEEPNORM_ALPHA = (2.0 * DEPTH) ** 0.25
LN_EPS = 1e-5
RMS_EPS = 1e-6

LANES = 128
SUBLANES = 8
HEAD_PAD = 128
CONV_C_HALO = 32
CONV_A_HALO = 8
CONV_ROWS = 128

T_EVEN = 512
T_ODD = 256
T_POST = 512
T_ATT = 512
T_MOE = 512
BM = 512
SEG = 8
PERM_ROWS = 256
VMEM_LIMIT = 56 * 1024 * 1024


def _cparams(sem):
    return pltpu.CompilerParams(dimension_semantics=sem, vmem_limit_bytes=VMEM_LIMIT)


def _layer_norm(v, g, b):
    mu = jnp.mean(v, axis=-1, keepdims=True)
    d = v - mu
    var = jnp.mean(d * d, axis=-1, keepdims=True)
    return d * lax.rsqrt(var + LN_EPS) * g + b


def _gelu(v):
    return 0.5 * v * (1.0 + lax.erf(v * (1.0 / math.sqrt(2.0))))


def _sigmoid(v):
    return 1.0 / (1.0 + jnp.exp(-v))


def _ada_kernel(c_ref, w_ref, b_ref, o_ref):
    c = c_ref[...]
    cond = c * _sigmoid(c)
    o_ref[0] = jnp.dot(cond, w_ref[0], precision=HIGHEST, preferred_element_type=F32) + b_ref[0]


def _ada(c, ada_w, ada_b):
    bsz, d = c.shape
    depth = ada_w.shape[0]
    n_chunk = ada_w.shape[2] // d
    return pl.pallas_call(
        _ada_kernel,
        grid=(depth, n_chunk),
        in_specs=[
            pl.BlockSpec((bsz, d), lambda l, j: (0, 0)),
            pl.BlockSpec((1, d, d), lambda l, j: (l, 0, j)),
            pl.BlockSpec((1, 1, d), lambda l, j: (l, 0, j)),
        ],
        out_specs=pl.BlockSpec((1, bsz, d), lambda l, j: (l, 0, j)),
        out_shape=jax.ShapeDtypeStruct((depth, bsz, n_chunk * d), F32),
        compiler_params=_cparams(("arbitrary", "arbitrary")),
        name="ada",
    )(c, ada_w, ada_b.reshape(depth, 1, n_chunk * d))


def _rope_kernel(pos_ref, freq_ref, cos_ref, sin_ref):
    ang = pos_ref[...].astype(F32) * freq_ref[...]
    cos_ref[...] = jnp.cos(ang)
    sin_ref[...] = jnp.sin(ang)


def _rope_tables(positions):
    n = positions.size
    t = 1024
    inv_freq = ROPE_THETA ** (-jnp.arange(0, QK_ROPE, 2, dtype=F32) / QK_ROPE)
    half = QK_ROPE // 2
    freq = jnp.zeros((1, LANES), F32).at[0, QK_NOPE:QK_NOPE + QK_ROPE].set(jnp.tile(inv_freq, 2))
    del half
    return pl.pallas_call(
        _rope_kernel,
        grid=(n // t,),
        in_specs=[pl.BlockSpec((t, 1), lambda i: (i, 0)), pl.BlockSpec((1, LANES), lambda i: (0, 0))],
        out_specs=[pl.BlockSpec((t, LANES), lambda i: (i, 0))] * 2,
        out_shape=[jax.ShapeDtypeStruct((n, LANES), F32)] * 2,
        compiler_params=_cparams(("arbitrary",)),
        name="rope",
    )(positions.reshape(n, 1), freq)


def _even_kernel(x_ref, mod_ref, win_ref, cw_ref, sgw_ref, sgb_ref, vng_ref, vnb_ref, wout_ref,
                 lng_ref, lnb_ref, o_ref, gbuf, ybuf):
    t_len = x_ref.shape[1]
    halo = CONV_A_HALO

    @pl.when(pl.program_id(1) == 0)
    def _():
        gbuf[0:halo, :] = jnp.zeros((halo, W_A), F32)

    x = x_ref[0]
    shift, scale, gate = mod_ref[0, 0:1, :], mod_ref[0, 1:2, :], mod_ref[0, 2:3, :]
    u = x * (1.0 + scale) + shift
    proj = jnp.dot(u.astype(BF16), win_ref[...], preferred_element_type=F32)
    b_gate = proj[:, 0:W_A]
    c_gate = proj[:, W_A:2 * W_A]
    xa = proj[:, 2 * W_A:3 * W_A]
    zu = proj[:, 3 * W_A:3 * W_A + W_B]
    zv = proj[:, 3 * W_A + W_B:3 * W_A + 2 * W_B]

    g = c_gate * xa
    gbuf[halo:halo + t_len, :] = g
    conv = (cw_ref[0:1, :] * gbuf[halo - 2:halo - 2 + t_len, :]
            + cw_ref[1:2, :] * gbuf[halo - 1:halo - 1 + t_len, :]
            + cw_ref[2:3, :] * g)
    gbuf[0:halo, :] = g[t_len - halo:t_len, :]
    ybuf[:, 0:W_A] = (b_gate * conv).astype(BF16)

    zu = _gelu(zu)
    zv = _layer_norm(_gelu(zv), vng_ref[...], vnb_ref[...]).astype(BF16)
    row = lax.broadcasted_iota(jnp.int32, (CHUNK, CHUNK), 0)
    col = lax.broadcasted_iota(jnp.int32, (CHUNK, CHUNK), 1)
    w_stack = jnp.concatenate(
        [jnp.where(row >= col, sgw_ref[h], 0.0).astype(BF16) for h in range(SG_HEADS)], axis=0)
    col_head = lax.broadcasted_iota(jnp.int32, (CHUNK, W_B), 1) // SG_HEAD_DIM
    for ci in range(t_len // CHUNK):
        lo = ci * CHUNK
        full = jnp.dot(w_stack, zv[lo:lo + CHUNK, :], preferred_element_type=F32)
        mixed = sgb_ref[...]
        for h in range(SG_HEADS):
            mixed = mixed + jnp.where(col_head == h, full[h * CHUNK:(h + 1) * CHUNK, :], 0.0)
        ybuf[lo:lo + CHUNK, W_A:W_A + W_B] = (zu[lo:lo + CHUNK, :] * mixed).astype(BF16)

    y = jnp.dot(ybuf[...], wout_ref[...], preferred_element_type=F32)
    r = DEEPNORM_ALPHA * x + (1.0 + gate) * y
    o_ref[0] = _layer_norm(r, lng_ref[...], lnb_ref[...])


def _even_layer(x, mod_l, w_in, conv_w, sg_w, sg_b, vn_g, vn_b, w_out, ln_g, ln_b):
    bsz, seq, d = x.shape
    t = min(T_EVEN, seq)
    sgb_full = jnp.repeat(sg_b.T, SG_HEAD_DIM, axis=1)
    full = lambda shape: pl.BlockSpec(shape, lambda b, i: (0,) * len(shape))
    return pl.pallas_call(
        _even_kernel,
        grid=(bsz, seq // t),
        in_specs=[
            pl.BlockSpec((1, t, d), lambda b, i: (b, i, 0)),
            pl.BlockSpec((1, 6, d), lambda b, i: (b, 0, 0)),
            full(w_in.shape), full(conv_w.shape), full(sg_w.shape), full(sgb_full.shape),
            full((1, W_B)), full((1, W_B)), full(w_out.shape), full((1, d)), full((1, d)),
        ],
        out_specs=pl.BlockSpec((1, t, d), lambda b, i: (b, i, 0)),
        out_shape=jax.ShapeDtypeStruct((bsz, seq, d), F32),
        scratch_shapes=[pltpu.VMEM((CONV_A_HALO + t, W_A), F32), pltpu.VMEM((t, W_A + W_B), BF16)],
        compiler_params=_cparams(("arbitrary", "arbitrary")),
        name="even",
    )(x, mod_l, w_in.astype(BF16), conv_w, sg_w, sgb_full, vn_g.reshape(1, W_B), vn_b.reshape(1, W_B),
      w_out.astype(BF16), ln_g.reshape(1, d), ln_b.reshape(1, d))


N_GLU = 2 * W_C
OFF_Q = N_GLU
OFF_KV = OFF_Q + Q_RANK
OFF_KR = OFF_KV + KV_RANK
OFF_KR_ROT = OFF_KR + HEAD_PAD
ODD_IN_PAD = OFF_KR_ROT + HEAD_PAD


def _odd_pre_kernel(x_ref, mod_ref, cos_ref, sin_ref, win_ref, dww_ref, dwb_ref, cng_ref, cnb_ref,
                    qng_ref, wq_ref, wqr_ref, kvng_ref, wk_ref, wv_ref,
                    yc_ref, q_ref, k_ref, v_ref, gbuf, sbuf, hbuf):
    t_len = x_ref.shape[1]
    halo = CONV_C_HALO

    @pl.when(pl.program_id(1) == 0)
    def _():
        gbuf[0:halo, :] = jnp.zeros((halo, W_C), F32)

    x = x_ref[0]
    shift, scale = mod_ref[0, 0:1, :], mod_ref[0, 1:2, :]
    u = x * (1.0 + scale) + shift
    proj = jnp.dot(u.astype(BF16), win_ref[...], preferred_element_type=F32)

    g = proj[:, 0:W_C] * _sigmoid(proj[:, W_C:2 * W_C])
    gbuf[halo:halo + t_len, :] = g
    first = halo - (CONV_C - 1)
    for res in range(1, SUBLANES):
        span = max(first + k - res for k in range(CONV_C) if (first + k) % SUBLANES == res) + t_len
        sbuf[res - 1, 0:span, :] = gbuf[res:res + span, :]
    for rb in range(t_len // CONV_ROWS):
        for cb in range(W_C // LANES):
            cs = slice(cb * LANES, (cb + 1) * LANES)
            acc = jnp.broadcast_to(dwb_ref[:, cs], (CONV_ROWS, LANES))
            for k in range(CONV_C):
                res = (first + k) % SUBLANES
                lo = first + k - res + rb * CONV_ROWS
                tap = gbuf[lo:lo + CONV_ROWS, cs] if res == 0 else sbuf[res - 1, lo:lo + CONV_ROWS, cs]
                acc = acc + dww_ref[k:k + 1, cs] * tap
            hbuf[rb * CONV_ROWS:(rb + 1) * CONV_ROWS, cs] = acc
    acc = hbuf[...]
    gbuf[0:halo, :] = g[t_len - halo:t_len, :]
    hn = _layer_norm(acc, cng_ref[...], cnb_ref[...])
    yc_ref[0] = (hn * _sigmoid(hn)).astype(BF16)

    cos = cos_ref[...]
    sin = sin_ref[...]
    cq = proj[:, OFF_Q:OFF_Q + Q_RANK]
    q_lat = (cq * lax.rsqrt(jnp.mean(cq * cq, axis=-1, keepdims=True) + RMS_EPS) * qng_ref[...]).astype(BF16)
    ckv = proj[:, OFF_KV:OFF_KV + KV_RANK]
    kv_lat = (ckv * lax.rsqrt(jnp.mean(ckv * ckv, axis=-1, keepdims=True) + RMS_EPS) * kvng_ref[...]).astype(BF16)
    q_all = jnp.dot(q_lat, wq_ref[...], preferred_element_type=F32)
    q_rot = jnp.dot(q_lat, wqr_ref[...], preferred_element_type=F32)
    k_all = jnp.dot(kv_lat, wk_ref[...], preferred_element_type=F32)
    v_all = jnp.dot(kv_lat, wv_ref[...], preferred_element_type=F32)
    k_rope = proj[:, OFF_KR:OFF_KR + HEAD_PAD] * cos + proj[:, OFF_KR_ROT:OFF_KR_ROT + HEAD_PAD] * sin
    sm_scale = math.log2(math.e) / math.sqrt(QK_NOPE + QK_ROPE)
    for h in range(MLA_HEADS):
        sl = slice(h * HEAD_PAD, (h + 1) * HEAD_PAD)
        q_ref[0, h] = ((q_all[:, sl] * cos + q_rot[:, sl] * sin) * sm_scale).astype(BF16)
        k_ref[0, h] = (k_all[:, sl] + k_rope).astype(BF16)
    for hp in range(MLA_HEADS // 2):
        v_ref[0, hp] = v_all[:, hp * LANES:(hp + 1) * LANES].astype(BF16)


def _odd_weights(w_in, w_uq, w_ukv):
    d = w_in.shape[0]
    half = QK_ROPE // 2
    kr = w_in[:, OFF_KR:OFF_KR + QK_ROPE]
    z = lambda n: jnp.zeros((d, n), w_in.dtype)
    kr_blk = jnp.concatenate([z(QK_NOPE), kr, z(HEAD_PAD - QK_NOPE - QK_ROPE)], axis=1)
    kr_rot = jnp.concatenate([z(QK_NOPE), -kr[:, half:], kr[:, :half], z(HEAD_PAD - QK_NOPE - QK_ROPE)], axis=1)
    w_in_p = jnp.concatenate([w_in[:, :OFF_KR], kr_blk, kr_rot], axis=1).astype(BF16)

    dq = QK_NOPE + QK_ROPE
    wq = w_uq.reshape(Q_RANK, MLA_HEADS, dq)
    zq = lambda n: jnp.zeros((Q_RANK, MLA_HEADS, n), w_uq.dtype)
    wq_p = jnp.concatenate([wq, zq(HEAD_PAD - dq)], axis=2).reshape(Q_RANK, MLA_HEADS * HEAD_PAD)
    wq_r = jnp.concatenate([zq(QK_NOPE), -wq[:, :, QK_NOPE + half:], wq[:, :, QK_NOPE:QK_NOPE + half],
                            zq(HEAD_PAD - dq)], axis=2).reshape(Q_RANK, MLA_HEADS * HEAD_PAD)

    wkv = w_ukv.reshape(KV_RANK, MLA_HEADS, QK_NOPE + V_DIM)
    wk_p = jnp.concatenate([wkv[:, :, :QK_NOPE], jnp.zeros((KV_RANK, MLA_HEADS, HEAD_PAD - QK_NOPE), w_ukv.dtype)],
                           axis=2).reshape(KV_RANK, MLA_HEADS * HEAD_PAD)
    wv_p = wkv[:, :, QK_NOPE:].reshape(KV_RANK, MLA_HEADS * V_DIM)
    return w_in_p, wq_p.astype(BF16), wq_r.astype(BF16), wk_p.astype(BF16), wv_p.astype(BF16)


def _odd_pre(x, mod_l, cos_t, sin_t, w_in, dw_w, dw_b, cn_g, cn_b, qn_g, w_uq, kvn_g, w_ukv):
    bsz, seq, d = x.shape
    t = min(T_ODD, seq)
    nt = seq // t
    w_in_p, wq_p, wq_r, wk_p, wv_p = _odd_weights(w_in, w_uq, w_ukv)
    full = lambda shape: pl.BlockSpec(shape, lambda b, i: (0,) * len(shape))
    return pl.pallas_call(
        _odd_pre_kernel,
        grid=(bsz, nt),
        in_specs=[
            pl.BlockSpec((1, t, d), lambda b, i: (b, i, 0)),
            pl.BlockSpec((1, 6, d), lambda b, i: (b, 0, 0)),
            pl.BlockSpec((t, LANES), lambda b, i: (b * nt + i, 0)),
            pl.BlockSpec((t, LANES), lambda b, i: (b * nt + i, 0)),
            full(w_in_p.shape), full(dw_w.shape), full((1, W_C)), full((1, W_C)), full((1, W_C)),
            full((1, Q_RANK)), full(wq_p.shape), full(wq_r.shape), full((1, KV_RANK)),
            full(wk_p.shape), full(wv_p.shape),
        ],
        out_specs=[
            pl.BlockSpec((1, t, W_C), lambda b, i: (b, i, 0)),
            pl.BlockSpec((1, MLA_HEADS, t, HEAD_PAD), lambda b, i: (b, 0, i, 0)),
            pl.BlockSpec((1, MLA_HEADS, t, HEAD_PAD), lambda b, i: (b, 0, i, 0)),
            pl.BlockSpec((1, MLA_HEADS // 2, t, LANES), lambda b, i: (b, 0, i, 0)),
        ],
        out_shape=[
            jax.ShapeDtypeStruct((bsz, seq, W_C), BF16),
            jax.ShapeDtypeStruct((bsz, MLA_HEADS, seq, HEAD_PAD), BF16),
            jax.ShapeDtypeStruct((bsz, MLA_HEADS, seq, HEAD_PAD), BF16),
            jax.ShapeDtypeStruct((bsz, MLA_HEADS // 2, seq, LANES), BF16),
        ],
        scratch_shapes=[pltpu.VMEM((CONV_C_HALO + t, W_C), F32),
                        pltpu.VMEM((SUBLANES - 1, CONV_C_HALO + t, W_C), F32), pltpu.VMEM((t, W_C), F32)],
        compiler_params=_cparams(("arbitrary", "arbitrary")),
        name="odd_pre",
    )(x, mod_l, cos_t, sin_t, w_in_p, dw_w, dw_b.reshape(1, W_C), cn_g.reshape(1, W_C), cn_b.reshape(1, W_C),
      qn_g.reshape(1, Q_RANK), wq_p, wq_r, kvn_g.reshape(1, KV_RANK), wk_p, wv_p)


ATT_FULL, ATT_DIAG_LATE, ATT_DIAG_EARLY = 0, 1, 2


def _attn_kernel(kj_tab, kind_tab, q_ref, k_ref, v_ref, o_ref, m_sc, l_sc, acc_sc):
    step_id = pl.program_id(2)
    kind = kind_tab[step_id]
    tq = q_ref.shape[2]
    tk = k_ref.shape[2]

    @pl.when(kj_tab[step_id] == 0)
    def _():
        m_sc[...] = jnp.full(m_sc.shape, -jnp.inf, F32)
        l_sc[...] = jnp.zeros(l_sc.shape, F32)
        acc_sc[...] = jnp.zeros(acc_sc.shape, F32)

    first_lanes = lax.broadcasted_iota(jnp.int32, (tq, LANES), 1) < V_DIM

    def step(n_keys, row_shift):
        n_col = n_keys // LANES
        v_pair = v_ref[0, 0, 0:n_keys, :]
        alphas = []
        pvs = []
        for a in range(2):
            s = lax.dot_general(q_ref[0, a], k_ref[0, a, 0:n_keys, :], (((1,), (1,)), ((), ())),
                                preferred_element_type=F32)
            if row_shift is not None:
                rows = row_shift + lax.broadcasted_iota(jnp.int32, (tq, n_keys), 0)
                cols = lax.broadcasted_iota(jnp.int32, (tq, n_keys), 1)
                s = jnp.where(cols <= rows, s, -1e30)
            cols_s = [s[:, c * LANES:(c + 1) * LANES] for c in range(n_col)]
            m_lane = cols_s[0]
            for c in range(1, n_col):
                m_lane = jnp.maximum(m_lane, cols_s[c])
            m_prev = m_sc[a]
            m_next = jnp.maximum(m_prev, jnp.max(m_lane, axis=1, keepdims=True))
            alpha = jnp.exp2(m_prev - m_next)
            ps = [jnp.exp2(cs - m_next) for cs in cols_s]
            l_lane = ps[0]
            for c in range(1, n_col):
                l_lane = l_lane + ps[c]
            l_sc[a] = alpha * l_sc[a] + l_lane
            m_sc[a] = m_next
            alphas.append(alpha)
            p = jnp.concatenate([pc.astype(BF16) for pc in ps], axis=1)
            pvs.append(jnp.dot(p, v_pair, preferred_element_type=F32))
        alpha_sel = jnp.where(first_lanes, alphas[0], alphas[1])
        acc_sc[...] = alpha_sel * acc_sc[...] + jnp.where(first_lanes, pvs[0], pvs[1])

    @pl.when(kind == ATT_FULL)
    def _():
        step(tk, None)

    @pl.when(kind == ATT_DIAG_LATE)
    def _():
        step(tk, tq)

    @pl.when(kind == ATT_DIAG_EARLY)
    def _():
        step(tq, 0)

    @pl.when(kind != ATT_FULL)
    def _():
        l_sel = jnp.where(first_lanes, jnp.sum(l_sc[0], axis=1, keepdims=True),
                          jnp.sum(l_sc[1], axis=1, keepdims=True))
        o_ref[0] = (acc_sc[...] / l_sel).astype(o_ref.dtype)


def _attention(q, k, v):
    bsz, heads, seq, _ = q.shape
    t = min(T_ATT, seq // 2)
    nq = seq // t
    steps = [(i, j, ATT_FULL if j < i // 2 else (ATT_DIAG_LATE if i % 2 else ATT_DIAG_EARLY))
             for i in range(nq) for j in range(i // 2 + 1)]
    qi_tab = jnp.asarray([st[0] for st in steps], jnp.int32)
    kj_tab = jnp.asarray([st[1] for st in steps], jnp.int32)
    kind_tab = jnp.asarray([st[2] for st in steps], jnp.int32)
    grid_spec = pltpu.PrefetchScalarGridSpec(
        num_scalar_prefetch=3,
        grid=(bsz, heads // 2, len(steps)),
        in_specs=[
            pl.BlockSpec((1, 2, t, HEAD_PAD), lambda b, hp, s, qt, kt, kd: (b, hp, qt[s], 0)),
            pl.BlockSpec((1, 2, 2 * t, HEAD_PAD), lambda b, hp, s, qt, kt, kd: (b, hp, kt[s], 0)),
            pl.BlockSpec((1, 1, 2 * t, LANES), lambda b, hp, s, qt, kt, kd: (b, hp, kt[s], 0)),
        ],
        out_specs=pl.BlockSpec((1, t, LANES), lambda b, hp, s, qt, kt, kd: (b, qt[s], hp)),
        scratch_shapes=[pltpu.VMEM((2, t, LANES), F32), pltpu.VMEM((2, t, LANES), F32),
                        pltpu.VMEM((t, LANES), F32)],
    )
    return pl.pallas_call(
        _attn_kernel_entry,
        grid_spec=grid_spec,
        out_shape=jax.ShapeDtypeStruct((bsz, seq, heads * V_DIM), BF16),
        compiler_params=_cparams(("arbitrary", "arbitrary", "arbitrary")),
        name="attn",
    )(qi_tab, kj_tab, kind_tab, q, k, v)


def _attn_kernel_entry(qi_tab, kj_tab, kind_tab, *refs):
    del qi_tab
    _attn_kernel(kj_tab, kind_tab, *refs)


def _odd_post_kernel(x_ref, mod_ref, yc_ref, yd_ref, wout_ref, lng_ref, lnb_ref, o_ref):
    x = x_ref[0]
    gate = mod_ref[0, 2:3, :]
    y = (jnp.dot(yc_ref[0], wout_ref[0:W_C, :], preferred_element_type=F32)
         + jnp.dot(yd_ref[0], wout_ref[W_C:, :], preferred_element_type=F32))
    r = DEEPNORM_ALPHA * x + (1.0 + gate) * y
    o_ref[0] = _layer_norm(r, lng_ref[...], lnb_ref[...])


def _odd_post(x, mod_l, y_c, y_d, w_out, ln_g, ln_b):
    bsz, seq, d = x.shape
    t = min(T_POST, seq)
    full = lambda shape: pl.BlockSpec(shape, lambda b, i: (0,) * len(shape))
    return pl.pallas_call(
        _odd_post_kernel,
        grid=(bsz, seq // t),
        in_specs=[
            pl.BlockSpec((1, t, d), lambda b, i: (b, i, 0)),
            pl.BlockSpec((1, 6, d), lambda b, i: (b, 0, 0)),
            pl.BlockSpec((1, t, W_C), lambda b, i: (b, i, 0)),
            pl.BlockSpec((1, t, MLA_HEADS * V_DIM), lambda b, i: (b, i, 0)),
            full(w_out.shape), full((1, d)), full((1, d)),
        ],
        out_specs=pl.BlockSpec((1, t, d), lambda b, i: (b, i, 0)),
        out_shape=jax.ShapeDtypeStruct((bsz, seq, d), F32),
        compiler_params=_cparams(("arbitrary", "arbitrary")),
        name="odd_post",
    )(x, mod_l, y_c, y_d, w_out.astype(BF16), ln_g.reshape(1, d), ln_b.reshape(1, d))


def _router_kernel(x_ref, mod_ref, rw_ref, rb_ref, topi_ref, gate_ref, lrank_ref, cnt_ref):
    t_len = x_ref.shape[1]
    x = x_ref[0]
    shift, scale = mod_ref[0, 3:4, :], mod_ref[0, 4:5, :]
    u = x * (1.0 + scale) + shift
    logits = lax.dot_general(rw_ref[...], u, (((1,), (1,)), ((), ())), precision=HIGHEST,
                             preferred_element_type=F32) + rb_ref[...]
    eidx = lax.broadcasted_iota(jnp.int32, (N_EXPERTS, t_len), 0)
    vals, hots = [], []
    for _ in range(TOP_K):
        m = jnp.max(logits, axis=0, keepdims=True)
        idx = jnp.min(jnp.where(logits == m, eidx, N_EXPERTS), axis=0, keepdims=True)
        hot = eidx == idx
        vals.append(m)
        hots.append(hot)
        logits = jnp.where(hot, -jnp.inf, logits)
        topi_ref[len(vals) - 1:len(vals), :] = idx
    exps = [jnp.exp(v - vals[0]) for v in vals]
    denom = exps[0] + exps[1] + exps[2] + exps[3]
    for kk in range(TOP_K):
        gate_ref[kk:kk + 1, :] = exps[kk] / denom

    chosen = jnp.where(hots[0] | hots[1] | hots[2] | hots[3], 1.0, 0.0)
    r_i = lax.broadcasted_iota(jnp.int32, (t_len, t_len), 0)
    c_i = lax.broadcasted_iota(jnp.int32, (t_len, t_len), 1)
    before = jnp.where(r_i < c_i, 1.0, 0.0).astype(BF16)
    base = jnp.dot(chosen.astype(BF16), before, preferred_element_type=F32)
    for kk in range(TOP_K):
        lrank_ref[kk:kk + 1, :] = jnp.sum(jnp.where(hots[kk], base, 0.0), axis=0, keepdims=True).astype(jnp.int32)
    cnt_ref[0] = jnp.broadcast_to(jnp.sum(chosen, axis=1, keepdims=True), cnt_ref.shape[1:])


def _router(x, mod_l, router_w, router_b):
    bsz, seq, d = x.shape
    t = min(T_MOE, seq)
    nt = seq // t
    n_tok = bsz * seq
    full = lambda shape: pl.BlockSpec(shape, lambda b, i: (0,) * len(shape))
    tok_spec = pl.BlockSpec((TOP_K, t), lambda b, i: (0, b * nt + i))
    return pl.pallas_call(
        _router_kernel,
        grid=(bsz, nt),
        in_specs=[
            pl.BlockSpec((1, t, d), lambda b, i: (b, i, 0)),
            pl.BlockSpec((1, 6, d), lambda b, i: (b, 0, 0)),
            full((N_EXPERTS, d)), full((N_EXPERTS, 1)),
        ],
        out_specs=[tok_spec, tok_spec, tok_spec,
                   pl.BlockSpec((1, N_EXPERTS, LANES), lambda b, i: (b * nt + i, 0, 0))],
        out_shape=[
            jax.ShapeDtypeStruct((TOP_K, n_tok), jnp.int32),
            jax.ShapeDtypeStruct((TOP_K, n_tok), F32),
            jax.ShapeDtypeStruct((TOP_K, n_tok), jnp.int32),
            jax.ShapeDtypeStruct((bsz * nt, N_EXPERTS, LANES), F32),
        ],
        compiler_params=_cparams(("arbitrary", "arbitrary")),
        name="router",
    )(x, mod_l, router_w.T, router_b.reshape(N_EXPERTS, 1))


def _tile_index():
    return pl.program_id(0) * pl.num_programs(1) + pl.program_id(1)


def _for_each_segment_chunk(tile, p_ref, loc0_ref, seg_ref, fn):
    def per_expert(e, carry):
        idx = tile * N_EXPERTS + e
        src0 = loc0_ref[idx]
        dst0 = seg_ref[idx]

        def chunk(cc, c2):
            fn(pl.multiple_of(src0 + cc * SEG, SEG), pl.multiple_of(dst0 + cc * SEG, SEG))
            return c2

        lax.fori_loop(0, p_ref[idx] // SEG, chunk, 0)
        return carry

    lax.fori_loop(0, N_EXPERTS, per_expert, 0)


def _dispatch_kernel(p_ref, loc0_ref, seg_ref, used_ref, tstart_ref, tlen_ref,
                     x_ref, mod_ref, loc_ref, xs_ref, pm_buf, lbuf, zbuf, sem):
    tile = _tile_index()
    n_tiles = pl.num_programs(0) * pl.num_programs(1)
    t_len = x_ref.shape[1]
    n_loc = lbuf.shape[1]
    x = x_ref[0]
    shift, scale = mod_ref[0, 3:4, :], mod_ref[0, 4:5, :]
    u = (x * (1.0 + scale) + shift).astype(BF16)
    loc = loc_ref[...]
    for jc in range(n_loc // PERM_ROWS):
        j = jc * PERM_ROWS + lax.broadcasted_iota(jnp.int32, (PERM_ROWS, t_len), 0)
        pm = jnp.zeros((PERM_ROWS, t_len), F32)
        for kk in range(TOP_K):
            pm = jnp.where(j == loc[kk:kk + 1, :], 1.0, pm)
        pm_buf[jc * PERM_ROWS:(jc + 1) * PERM_ROWS, :] = pm.astype(BF16)
    slot = tile % 2
    lbuf[slot] = jnp.dot(pm_buf[...], u, preferred_element_type=F32)

    def seg_copy(s, src_row, dst_row):
        return pltpu.make_async_copy(lbuf.at[s, pl.ds(src_row, SEG)], xs_ref.at[pl.ds(dst_row, SEG)], sem)

    def drain(n_rows):
        def wait_one(cc, c2):
            seg_copy(0, 0, 0).wait()
            return c2

        lax.fori_loop(0, n_rows // SEG, wait_one, 0)

    @pl.when(tile > 0)
    def _():
        drain(used_ref[tile - 1])

    _for_each_segment_chunk(tile, p_ref, loc0_ref, seg_ref, lambda sr, dr: seg_copy(slot, sr, dr).start())

    @pl.when(tile == n_tiles - 1)
    def _():
        drain(used_ref[tile])
        zbuf[...] = jnp.zeros(zbuf.shape, F32)

        def fill(e, rows):
            def zero_copy(cc):
                dst = pl.multiple_of(tstart_ref[e] + cc * rows, SEG)
                return pltpu.make_async_copy(zbuf.at[pl.ds(0, rows)], xs_ref.at[pl.ds(dst, rows)], sem)

            n = tlen_ref[e] // rows

            def start(cc, c2):
                zero_copy(cc).start()
                return c2

            def wait(cc, c2):
                zero_copy(cc).wait()
                return c2

            lax.fori_loop(0, n, start, 0)
            lax.fori_loop(0, n, wait, 0)

        def per_expert(e, carry):
            fill(e, SEG)
            return carry

        lax.fori_loop(0, N_EXPERTS, per_expert, 0)
        fill(N_EXPERTS, BM)


def _dispatch(x, mod_l, loc, plan, n_slots):
    bsz, seq, d = x.shape
    t = min(T_MOE, seq)
    nt = seq // t
    n_loc = TOP_K * t + N_EXPERTS * SEG
    grid_spec = pltpu.PrefetchScalarGridSpec(
        num_scalar_prefetch=6,
        grid=(bsz, nt),
        in_specs=[
            pl.BlockSpec((1, t, d), lambda b, i, *_: (b, i, 0)),
            pl.BlockSpec((1, 6, d), lambda b, i, *_: (b, 0, 0)),
            pl.BlockSpec((TOP_K, t), lambda b, i, *_: (0, b * nt + i)),
        ],
        out_specs=pl.BlockSpec(memory_space=pl.ANY),
        scratch_shapes=[pltpu.VMEM((n_loc, t), BF16), pltpu.VMEM((2, n_loc, d), F32), pltpu.VMEM((BM, d), F32),
                        pltpu.SemaphoreType.DMA(())],
    )
    return pl.pallas_call(
        _dispatch_kernel,
        grid_spec=grid_spec,
        out_shape=jax.ShapeDtypeStruct((n_slots, d), F32),
        compiler_params=_cparams(("arbitrary", "arbitrary")),
        name="dispatch",
    )(plan["p"], plan["loc0"], plan["seg"], plan["used"], plan["tstart"], plan["tlen"], x, mod_l, loc)


def _expert_kernel(bexp_ref, nused_ref, xs_ref, wgu_ref, bgu_ref, wdn_ref, bdn_ref, ys_ref, wgu_bf, wdn_bf):
    i = pl.program_id(0)
    active = i < nused_ref[0]
    new_expert = (i == 0) | (bexp_ref[i] != bexp_ref[jnp.maximum(i - 1, 0)])

    @pl.when(active & new_expert)
    def _():
        wgu_bf[...] = wgu_ref[0, 0].astype(BF16)
        wdn_bf[...] = wdn_ref[0, 0].astype(BF16)

    @pl.when(active)
    def _():
        gu = jnp.dot(xs_ref[...].astype(BF16), wgu_bf[...], preferred_element_type=F32) + bgu_ref[0, 0]
        gate = jnp.minimum(gu[:, :D_EXPERT], SWIGLU_LIMIT)
        up = jnp.clip(gu[:, D_EXPERT:], -SWIGLU_LIMIT, SWIGLU_LIMIT)
        glu = gate * _sigmoid(SWIGLU_ALPHA * gate)
        h = ((up + 1.0) * glu).astype(BF16)
        ys_ref[...] = jnp.dot(h, wdn_bf[...], preferred_element_type=F32) + bdn_ref[0, 0]

    @pl.when(jnp.logical_not(active))
    def _():
        ys_ref[...] = jnp.zeros(ys_ref.shape, F32)


def _experts(xs, block_exp, n_used, layer, w_gu, b_gu, w_dn, b_dn):
    n_slots, d = xs.shape
    n_blocks = n_slots // BM
    depth = w_gu.shape[0]
    row_map = lambda i, be, nu: (jnp.minimum(i, nu[0] - 1), 0)
    exp_map = lambda i, be, nu: (layer, be[i], 0, 0)
    grid_spec = pltpu.PrefetchScalarGridSpec(
        num_scalar_prefetch=2,
        grid=(n_blocks,),
        in_specs=[
            pl.BlockSpec((BM, d), row_map),
            pl.BlockSpec((1, 1, d, 2 * D_EXPERT), exp_map),
            pl.BlockSpec((1, 1, 1, 2 * D_EXPERT), exp_map),
            pl.BlockSpec((1, 1, D_EXPERT, d), exp_map),
            pl.BlockSpec((1, 1, 1, d), exp_map),
        ],
        out_specs=pl.BlockSpec((BM, d), lambda i, be, nu: (i, 0)),
        scratch_shapes=[pltpu.VMEM((d, 2 * D_EXPERT), BF16), pltpu.VMEM((D_EXPERT, d), BF16)],
    )
    return pl.pallas_call(
        _expert_kernel,
        grid_spec=grid_spec,
        out_shape=jax.ShapeDtypeStruct((n_slots, d), F32),
        compiler_params=_cparams(("arbitrary",)),
        name="experts",
    )(block_exp, n_used, xs, w_gu, b_gu.reshape(depth, N_EXPERTS, 1, 2 * D_EXPERT), w_dn,
      b_dn.reshape(depth, N_EXPERTS, 1, d))


def _combine_kernel(p_ref, loc0_ref, seg_ref, used_ref, x_ref, mod_ref, loct_ref, gatet_ref, ys_ref,
                    lng_ref, lnb_ref, o_ref, ybuf, sel_buf, sem):
    tile = _tile_index()
    n_tiles = pl.num_programs(0) * pl.num_programs(1)
    t_len = x_ref.shape[1]
    n_loc = ybuf.shape[1]
    slot = tile % 2

    def seg_copy(s, src_row, dst_row):
        return pltpu.make_async_copy(ys_ref.at[pl.ds(dst_row, SEG)], ybuf.at[s, pl.ds(src_row, SEG)], sem.at[s])

    def fetch(t, s):
        _for_each_segment_chunk(t, p_ref, loc0_ref, seg_ref, lambda sr, dr: seg_copy(s, sr, dr).start())
        used_t = used_ref[t]

        def zero_rows(cc, c2):
            ybuf[s, pl.ds(pl.multiple_of(used_t + cc * SEG, SEG), SEG), :] = jnp.zeros((SEG, ybuf.shape[2]), F32)
            return c2

        lax.fori_loop(0, (n_loc - used_t) // SEG, zero_rows, 0)

    @pl.when(tile == 0)
    def _():
        fetch(0, 0)

    @pl.when(tile + 1 < n_tiles)
    def _():
        fetch(tile + 1, 1 - slot)

    loct = loct_ref[...]
    gatet = gatet_ref[...]
    for jc in range(n_loc // LANES):
        j = jc * LANES + lax.broadcasted_iota(jnp.int32, (t_len, LANES), 1)
        sel = jnp.zeros((t_len, LANES), F32)
        for kk in range(TOP_K):
            sel = jnp.where(j == loct[:, kk:kk + 1], gatet[:, kk:kk + 1], sel)
        sel_buf[:, jc * LANES:(jc + 1) * LANES] = sel.astype(BF16)

    def drain(cc, c2):
        seg_copy(slot, 0, 0).wait()
        return c2

    lax.fori_loop(0, used_ref[tile] // SEG, drain, 0)

    y = jnp.dot(sel_buf[...], ybuf[slot].astype(BF16), preferred_element_type=F32)
    x = x_ref[0]
    gate_f = mod_ref[0, 5:6, :]
    r = DEEPNORM_ALPHA * x + (1.0 + gate_f) * y
    o_ref[0] = _layer_norm(r, lng_ref[...], lnb_ref[...])


def _combine(x, mod_l, loc_t, gates_t, plan, ys, ln_g, ln_b):
    bsz, seq, d = x.shape
    t = min(T_MOE, seq)
    nt = seq // t
    n_loc = TOP_K * t + N_EXPERTS * SEG
    full = lambda shape: pl.BlockSpec(shape, lambda b, i, *_: (0,) * len(shape))
    grid_spec = pltpu.PrefetchScalarGridSpec(
        num_scalar_prefetch=4,
        grid=(bsz, nt),
        in_specs=[
            pl.BlockSpec((1, t, d), lambda b, i, *_: (b, i, 0)),
            pl.BlockSpec((1, 6, d), lambda b, i, *_: (b, 0, 0)),
            pl.BlockSpec((t, TOP_K), lambda b, i, *_: (b * nt + i, 0)),
            pl.BlockSpec((t, TOP_K), lambda b, i, *_: (b * nt + i, 0)),
            pl.BlockSpec(memory_space=pl.ANY),
            full((1, d)), full((1, d)),
        ],
        out_specs=pl.BlockSpec((1, t, d), lambda b, i, *_: (b, i, 0)),
        scratch_shapes=[pltpu.VMEM((2, n_loc, d), F32), pltpu.VMEM((t, n_loc), BF16),
                        pltpu.SemaphoreType.DMA((2,))],
    )
    return pl.pallas_call(
        _combine_kernel,
        grid_spec=grid_spec,
        out_shape=jax.ShapeDtypeStruct((bsz, seq, d), F32),
        compiler_params=_cparams(("arbitrary", "arbitrary")),
        name="combine",
    )(plan["p"], plan["loc0"], plan["seg"], plan["used"], x, mod_l, loc_t, gates_t, ys,
      ln_g.reshape(1, d), ln_b.reshape(1, d))


def _slots_kernel(loc0_ref, topi_ref, lrank_ref, loc_ref):
    tile = pl.program_id(0)
    topi = topi_ref[...]
    loc = lrank_ref[...]
    for e in range(N_EXPERTS):
        loc = loc + jnp.where(topi == e, loc0_ref[tile * N_EXPERTS + e], 0)
    loc_ref[...] = loc


def _slots(loc0, topi, lrank, t):
    k, n = topi.shape
    spec = pl.BlockSpec((k, t), lambda i, l0: (0, i))
    return pl.pallas_call(
        _slots_kernel,
        grid_spec=pltpu.PrefetchScalarGridSpec(num_scalar_prefetch=1, grid=(n // t,), in_specs=[spec, spec],
                                               out_specs=spec),
        out_shape=jax.ShapeDtypeStruct((k, n), jnp.int32),
        compiler_params=_cparams(("arbitrary",)),
        name="slots",
    )(loc0, topi, lrank)


def _moe_plan(cnt, n_slots):
    c = cnt[:, :, 0].astype(jnp.int32)
    p = (c + SEG - 1) // SEG * SEG
    tot = jnp.sum(p, axis=0)
    padded = (tot + BM - 1) // BM * BM
    pad_end = jnp.cumsum(padded)
    pad_start = pad_end - padded
    seg = pad_start[None, :] + jnp.cumsum(p, axis=0) - p
    loc0 = jnp.cumsum(p, axis=1) - p
    n_blocks = n_slots // BM
    block_lo = jnp.arange(n_blocks, dtype=jnp.int32) * BM
    block_exp = jnp.minimum(jnp.sum((pad_end[None, :] <= block_lo[:, None]).astype(jnp.int32), axis=1),
                            N_EXPERTS - 1)
    return {
        "p": p.reshape(-1), "loc0": loc0.reshape(-1), "seg": seg.reshape(-1), "used": jnp.sum(p, axis=1),
        "tstart": jnp.concatenate([pad_start + tot, pad_end[-1:]]),
        "tlen": jnp.concatenate([padded - tot, n_slots - pad_end[-1:]]),
        "n_used": pad_end[-1:] // BM, "block_exp": block_exp,
    }


def _moe_layer(x, mod_l, layer, router_w, router_b, w_gu, b_gu, w_dn, b_dn, ln_g, ln_b):
    bsz, seq, d = x.shape
    n_tok = bsz * seq
    t = min(T_MOE, seq)
    n_slots = n_tok * TOP_K + (n_tok // t) * N_EXPERTS * SEG + N_EXPERTS * BM
    topi, gates, lrank, cnt = _router(x, mod_l, router_w, router_b)
    plan = _moe_plan(cnt, n_slots)
    loc = _slots(plan["loc0"], topi, lrank, t)
    xs = _dispatch(x, mod_l, loc, plan, n_slots)
    ys = _experts(xs, plan["block_exp"], plan["n_used"], layer, w_gu, b_gu, w_dn, b_dn)
    return _combine(x, mod_l, loc.T, gates.T, plan, ys, ln_g, ln_b)


def kernel(x, c, positions, ada_w, ada_b, ln_mix_g, ln_mix_b, ln_ffn_g, ln_ffn_b, ev_w_in, ev_conv_w, ev_sg_w, ev_sg_b, ev_vn_g, ev_vn_b, ev_w_out, od_w_in, od_dw_w, od_dw_b, od_cn_g, od_cn_b, od_qn_g, od_w_uq, od_kvn_g, od_w_ukv, od_w_out, moe_router_w, moe_router_b, moe_w_gu, moe_b_gu, moe_w_dn, moe_b_dn):
    bsz, seq, d = x.shape
    depth = ada_w.shape[0]
    mod = _ada(c, ada_w, ada_b).reshape(depth, bsz, 6, d)
    cos_t, sin_t = _rope_tables(positions)
    for layer in range(depth):
        i = layer // 2
        mod_l = mod[layer]
        if layer % 2 == 0:
            x = _even_layer(x, mod_l, ev_w_in[i], ev_conv_w[i], ev_sg_w[i], ev_sg_b[i], ev_vn_g[i], ev_vn_b[i],
                            ev_w_out[i], ln_mix_g[layer], ln_mix_b[layer])
        else:
            y_c, q, k, v = _odd_pre(x, mod_l, cos_t, sin_t, od_w_in[i], od_dw_w[i], od_dw_b[i], od_cn_g[i],
                                    od_cn_b[i], od_qn_g[i], od_w_uq[i], od_kvn_g[i], od_w_ukv[i])
            y_d = _attention(q, k, v)
            x = _odd_post(x, mod_l, y_c, y_d, od_w_out[i], ln_mix_g[layer], ln_mix_b[layer])
        x = _moe_layer(x, mod_l, layer, moe_router_w[layer], moe_router_b[layer], moe_w_gu, moe_b_gu,
                       moe_w_dn, moe_b_dn, ln_ffn_g[layer], ln_ffn_b[layer])
    return x
```

```python
import functools
import math

import jax
import jax.numpy as jnp
from jax import lax
from jax.experimental import pallas as pl
from jax.experimental.pallas import tpu as pltpu

F32 = jnp.float32
BF16 = jnp.bfloat16
HIGHEST = lax.Precision.HIGHEST

D_MODEL = 1024
DEPTH = 4
W_A = 512
W_B = 512
SG_HEADS = 8
SG_HEAD_DIM = 64
CHUNK = 128
CONV_A = 3
W_C = 512
CONV_C = 31
MLA_HEADS = 8
QK_NOPE = 64
QK_ROPE = 32
V_DIM = 64
Q_RANK = 256
KV_RANK = 128
ROPE_THETA = 10000.0
N_EXPERTS = 32
TOP_K = 4
D_EXPERT = 1024
SWIGLU_LIMIT = 7.0
SWIGLU_ALPHA = 1.702
DEEPNORM_ALPHA = (2.0 * DEPTH) ** 0.25
LN_EPS = 1e-5
RMS_EPS = 1e-6

LANES = 128
SUBLANES = 8
HEAD_PAD = 128
CONV_C_HALO = 32
CONV_A_HALO = 8
CONV_ROWS = 128

T_EVEN = 512
T_ODD = 256
T_POST = 512
T_ATT = 512
ATT_HEADS = 4
T_MOE = 512
BM = 512
SEG = 8
PERM_ROWS = 256
VMEM_LIMIT = 56 * 1024 * 1024


def _cparams(sem):
    return pltpu.CompilerParams(dimension_semantics=sem, vmem_limit_bytes=VMEM_LIMIT)


def _layer_norm(v, g, b):
    mu = jnp.mean(v, axis=-1, keepdims=True)
    d = v - mu
    var = jnp.mean(d * d, axis=-1, keepdims=True)
    return d * lax.rsqrt(var + LN_EPS) * g + b


def _gelu(v):
    return 0.5 * v * (1.0 + lax.erf(v * (1.0 / math.sqrt(2.0))))


def _sigmoid(v):
    return 1.0 / (1.0 + jnp.exp(-v))


def _ada_kernel(c_ref, w_ref, b_ref, o_ref):
    c = c_ref[...]
    cond = c * _sigmoid(c)
    o_ref[0] = jnp.dot(cond, w_ref[0], precision=HIGHEST, preferred_element_type=F32) + b_ref[0]


def _ada(c, ada_w, ada_b):
    bsz, d = c.shape
    depth = ada_w.shape[0]
    n_chunk = ada_w.shape[2] // d
    return pl.pallas_call(
        _ada_kernel,
        grid=(depth, n_chunk),
        in_specs=[
            pl.BlockSpec((bsz, d), lambda l, j: (0, 0)),
            pl.BlockSpec((1, d, d), lambda l, j: (l, 0, j)),
            pl.BlockSpec((1, 1, d), lambda l, j: (l, 0, j)),
        ],
        out_specs=pl.BlockSpec((1, bsz, d), lambda l, j: (l, 0, j)),
        out_shape=jax.ShapeDtypeStruct((depth, bsz, n_chunk * d), F32),
        compiler_params=_cparams(("arbitrary", "arbitrary")),
        name="ada",
    )(c, ada_w, ada_b.reshape(depth, 1, n_chunk * d))


def _rope_kernel(pos_ref, freq_ref, cos_ref, sin_ref):
    ang = pos_ref[...].astype(F32) * freq_ref[...]
    cos_ref[...] = jnp.cos(ang)
    sin_ref[...] = jnp.sin(ang)


def _rope_tables(positions):
    n = positions.size
    t = 1024
    inv_freq = ROPE_THETA ** (-jnp.arange(0, QK_ROPE, 2, dtype=F32) / QK_ROPE)
    half = QK_ROPE // 2
    freq = jnp.zeros((1, LANES), F32).at[0, QK_NOPE:QK_NOPE + QK_ROPE].set(jnp.tile(inv_freq, 2))
    del half
    return pl.pallas_call(
        _rope_kernel,
        grid=(n // t,),
        in_specs=[pl.BlockSpec((t, 1), lambda i: (i, 0)), pl.BlockSpec((1, LANES), lambda i: (0, 0))],
        out_specs=[pl.BlockSpec((t, LANES), lambda i: (i, 0))] * 2,
        out_shape=[jax.ShapeDtypeStruct((n, LANES), F32)] * 2,
        compiler_params=_cparams(("arbitrary",)),
        name="rope",
    )(positions.reshape(n, 1), freq)


def _even_kernel(x_ref, mod_ref, win_ref, cw_ref, sgw_ref, sgb_ref, vng_ref, vnb_ref, wout_ref,
                 lng_ref, lnb_ref, o_ref, gbuf, ybuf):
    t_len = x_ref.shape[1]
    halo = CONV_A_HALO

    @pl.when(pl.program_id(1) == 0)
    def _():
        gbuf[0:halo, :] = jnp.zeros((halo, W_A), F32)

    x = x_ref[0]
    shift, scale, gate = mod_ref[0, 0:1, :], mod_ref[0, 1:2, :], mod_ref[0, 2:3, :]
    u = x * (1.0 + scale) + shift
    proj = jnp.dot(u.astype(BF16), win_ref[...], preferred_element_type=F32)
    b_gate = proj[:, 0:W_A]
    c_gate = proj[:, W_A:2 * W_A]
    xa = proj[:, 2 * W_A:3 * W_A]
    zu = proj[:, 3 * W_A:3 * W_A + W_B]
    zv = proj[:, 3 * W_A + W_B:3 * W_A + 2 * W_B]

    g = c_gate * xa
    gbuf[halo:halo + t_len, :] = g
    conv = (cw_ref[0:1, :] * gbuf[halo - 2:halo - 2 + t_len, :]
            + cw_ref[1:2, :] * gbuf[halo - 1:halo - 1 + t_len, :]
            + cw_ref[2:3, :] * g)
    gbuf[0:halo, :] = g[t_len - halo:t_len, :]
    ybuf[:, 0:W_A] = (b_gate * conv).astype(BF16)

    zu = _gelu(zu)
    zv = _layer_norm(_gelu(zv), vng_ref[...], vnb_ref[...]).astype(BF16)
    row = lax.broadcasted_iota(jnp.int32, (CHUNK, CHUNK), 0)
    col = lax.broadcasted_iota(jnp.int32, (CHUNK, CHUNK), 1)
    w_stack = jnp.concatenate(
        [jnp.where(row >= col, sgw_ref[h], 0.0).astype(BF16) for h in range(SG_HEADS)], axis=0)
    col_head = lax.broadcasted_iota(jnp.int32, (CHUNK, W_B), 1) // SG_HEAD_DIM
    for ci in range(t_len // CHUNK):
        lo = ci * CHUNK
        full = jnp.dot(w_stack, zv[lo:lo + CHUNK, :], preferred_element_type=F32)
        mixed = sgb_ref[...]
        for h in range(SG_HEADS):
            mixed = mixed + jnp.where(col_head == h, full[h * CHUNK:(h + 1) * CHUNK, :], 0.0)
        ybuf[lo:lo + CHUNK, W_A:W_A + W_B] = (zu[lo:lo + CHUNK, :] * mixed).astype(BF16)

    y = jnp.dot(ybuf[...], wout_ref[...], preferred_element_type=F32)
    r = DEEPNORM_ALPHA * x + (1.0 + gate) * y
    o_ref[0] = _layer_norm(r, lng_ref[...], lnb_ref[...])


def _even_layer(x, mod_l, w_in, conv_w, sg_w, sg_b, vn_g, vn_b, w_out, ln_g, ln_b):
    bsz, seq, d = x.shape
    t = min(T_EVEN, seq)
    sgb_full = jnp.repeat(sg_b.T, SG_HEAD_DIM, axis=1)
    full = lambda shape: pl.BlockSpec(shape, lambda b, i: (0,) * len(shape))
    return pl.pallas_call(
        _even_kernel,
        grid=(bsz, seq // t),
        in_specs=[
            pl.BlockSpec((1, t, d), lambda b, i: (b, i, 0)),
            pl.BlockSpec((1, 6, d), lambda b, i: (b, 0, 0)),
            full(w_in.shape), full(conv_w.shape), full(sg_w.shape), full(sgb_full.shape),
            full((1, W_B)), full((1, W_B)), full(w_out.shape), full((1, d)), full((1, d)),
        ],
        out_specs=pl.BlockSpec((1, t, d), lambda b, i: (b, i, 0)),
        out_shape=jax.ShapeDtypeStruct((bsz, seq, d), F32),
        scratch_shapes=[pltpu.VMEM((CONV_A_HALO + t, W_A), F32), pltpu.VMEM((t, W_A + W_B), BF16)],
        compiler_params=_cparams(("arbitrary", "arbitrary")),
        name="even",
    )(x, mod_l, w_in.astype(BF16), conv_w, sg_w, sgb_full, vn_g.reshape(1, W_B), vn_b.reshape(1, W_B),
      w_out.astype(BF16), ln_g.reshape(1, d), ln_b.reshape(1, d))


N_GLU = 2 * W_C
OFF_Q = N_GLU
OFF_KV = OFF_Q + Q_RANK
OFF_KR = OFF_KV + KV_RANK
OFF_KR_ROT = OFF_KR + HEAD_PAD
ODD_IN_PAD = OFF_KR_ROT + HEAD_PAD


def _odd_pre_kernel(x_ref, mod_ref, cos_ref, sin_ref, win_ref, dww_ref, dwb_ref, cng_ref, cnb_ref,
                    qng_ref, wq_ref, wqr_ref, kvng_ref, wk_ref, wv_ref,
                    yc_ref, q_ref, k_ref, v_ref, gbuf, sbuf, hbuf):
    t_len = x_ref.shape[1]
    halo = CONV_C_HALO

    @pl.when(pl.program_id(1) == 0)
    def _():
        gbuf[0:halo, :] = jnp.zeros((halo, W_C), F32)

    x = x_ref[0]
    shift, scale = mod_ref[0, 0:1, :], mod_ref[0, 1:2, :]
    u = x * (1.0 + scale) + shift
    proj = jnp.dot(u.astype(BF16), win_ref[...], preferred_element_type=F32)

    g = proj[:, 0:W_C] * _sigmoid(proj[:, W_C:2 * W_C])
    gbuf[halo:halo + t_len, :] = g
    first = halo - (CONV_C - 1)
    for res in range(1, SUBLANES):
        span = max(first + k - res for k in range(CONV_C) if (first + k) % SUBLANES == res) + t_len
        sbuf[res - 1, 0:span, :] = gbuf[res:res + span, :]
    for rb in range(t_len // CONV_ROWS):
        for cb in range(W_C // LANES):
            cs = slice(cb * LANES, (cb + 1) * LANES)
            acc = jnp.broadcast_to(dwb_ref[:, cs], (CONV_ROWS, LANES))
            for k in range(CONV_C):
                res = (first + k) % SUBLANES
                lo = first + k - res + rb * CONV_ROWS
                tap = gbuf[lo:lo + CONV_ROWS, cs] if res == 0 else sbuf[res - 1, lo:lo + CONV_ROWS, cs]
                acc = acc + dww_ref[k:k + 1, cs] * tap
            hbuf[rb * CONV_ROWS:(rb + 1) * CONV_ROWS, cs] = acc
    acc = hbuf[...]
    gbuf[0:halo, :] = g[t_len - halo:t_len, :]
    hn = _layer_norm(acc, cng_ref[...], cnb_ref[...])
    yc_ref[0] = (hn * _sigmoid(hn)).astype(BF16)

    cos = cos_ref[...]
    sin = sin_ref[...]
    cq = proj[:, OFF_Q:OFF_Q + Q_RANK]
    q_lat = (cq * lax.rsqrt(jnp.mean(cq * cq, axis=-1, keepdims=True) + RMS_EPS) * qng_ref[...]).astype(BF16)
    ckv = proj[:, OFF_KV:OFF_KV + KV_RANK]
    kv_lat = (ckv * lax.rsqrt(jnp.mean(ckv * ckv, axis=-1, keepdims=True) + RMS_EPS) * kvng_ref[...]).astype(BF16)
    q_all = jnp.dot(q_lat, wq_ref[...], preferred_element_type=F32)
    q_rot = jnp.dot(q_lat, wqr_ref[...], preferred_element_type=F32)
    k_all = jnp.dot(kv_lat, wk_ref[...], preferred_element_type=F32)
    v_all = jnp.dot(kv_lat, wv_ref[...], preferred_element_type=F32)
    k_rope = proj[:, OFF_KR:OFF_KR + HEAD_PAD] * cos + proj[:, OFF_KR_ROT:OFF_KR_ROT + HEAD_PAD] * sin
    sm_scale = math.log2(math.e) / math.sqrt(QK_NOPE + QK_ROPE)
    for h in range(MLA_HEADS):
        sl = slice(h * HEAD_PAD, (h + 1) * HEAD_PAD)
        q_ref[0, h] = ((q_all[:, sl] * cos + q_rot[:, sl] * sin) * sm_scale).astype(BF16)
        k_ref[0, h] = (k_all[:, sl] + k_rope).astype(BF16)
    for hp in range(MLA_HEADS // 2):
        v_ref[0, hp] = v_all[:, hp * LANES:(hp + 1) * LANES].astype(BF16)


def _odd_weights(w_in, w_uq, w_ukv):
    d = w_in.shape[0]
    half = QK_ROPE // 2
    kr = w_in[:, OFF_KR:OFF_KR + QK_ROPE]
    z = lambda n: jnp.zeros((d, n), w_in.dtype)
    kr_blk = jnp.concatenate([z(QK_NOPE), kr, z(HEAD_PAD - QK_NOPE - QK_ROPE)], axis=1)
    kr_rot = jnp.concatenate([z(QK_NOPE), -kr[:, half:], kr[:, :half], z(HEAD_PAD - QK_NOPE - QK_ROPE)], axis=1)
    w_in_p = jnp.concatenate([w_in[:, :OFF_KR], kr_blk, kr_rot], axis=1).astype(BF16)

    dq = QK_NOPE + QK_ROPE
    wq = w_uq.reshape(Q_RANK, MLA_HEADS, dq)
    zq = lambda n: jnp.zeros((Q_RANK, MLA_HEADS, n), w_uq.dtype)
    wq_p = jnp.concatenate([wq, zq(HEAD_PAD - dq)], axis=2).reshape(Q_RANK, MLA_HEADS * HEAD_PAD)
    wq_r = jnp.concatenate([zq(QK_NOPE), -wq[:, :, QK_NOPE + half:], wq[:, :, QK_NOPE:QK_NOPE + half],
                            zq(HEAD_PAD - dq)], axis=2).reshape(Q_RANK, MLA_HEADS * HEAD_PAD)

    wkv = w_ukv.reshape(KV_RANK, MLA_HEADS, QK_NOPE + V_DIM)
    wk_p = jnp.concatenate([wkv[:, :, :QK_NOPE], jnp.zeros((KV_RANK, MLA_HEADS, HEAD_PAD - QK_NOPE), w_ukv.dtype)],
                           axis=2).reshape(KV_RANK, MLA_HEADS * HEAD_PAD)
    wv_p = wkv[:, :, QK_NOPE:].reshape(KV_RANK, MLA_HEADS * V_DIM)
    return w_in_p, wq_p.astype(BF16), wq_r.astype(BF16), wk_p.astype(BF16), wv_p.astype(BF16)


def _odd_pre(x, mod_l, cos_t, sin_t, w_in, dw_w, dw_b, cn_g, cn_b, qn_g, w_uq, kvn_g, w_ukv):
    bsz, seq, d = x.shape
    t = min(T_ODD, seq)
    nt = seq // t
    w_in_p, wq_p, wq_r, wk_p, wv_p = _odd_weights(w_in, w_uq, w_ukv)
    full = lambda shape: pl.BlockSpec(shape, lambda b, i: (0,) * len(shape))
    return pl.pallas_call(
        _odd_pre_kernel,
        grid=(bsz, nt),
        in_specs=[
            pl.BlockSpec((1, t, d), lambda b, i: (b, i, 0)),
            pl.BlockSpec((1, 6, d), lambda b, i: (b, 0, 0)),
            pl.BlockSpec((t, LANES), lambda b, i: (b * nt + i, 0)),
            pl.BlockSpec((t, LANES), lambda b, i: (b * nt + i, 0)),
            full(w_in_p.shape), full(dw_w.shape), full((1, W_C)), full((1, W_C)), full((1, W_C)),
            full((1, Q_RANK)), full(wq_p.shape), full(wq_r.shape), full((1, KV_RANK)),
            full(wk_p.shape), full(wv_p.shape),
        ],
        out_specs=[
            pl.BlockSpec((1, t, W_C), lambda b, i: (b, i, 0)),
            pl.BlockSpec((1, MLA_HEADS, t, HEAD_PAD), lambda b, i: (b, 0, i, 0)),
            pl.BlockSpec((1, MLA_HEADS, t, HEAD_PAD), lambda b, i: (b, 0, i, 0)),
            pl.BlockSpec((1, MLA_HEADS // 2, t, LANES), lambda b, i: (b, 0, i, 0)),
        ],
        out_shape=[
            jax.ShapeDtypeStruct((bsz, seq, W_C), BF16),
            jax.ShapeDtypeStruct((bsz, MLA_HEADS, seq, HEAD_PAD), BF16),
            jax.ShapeDtypeStruct((bsz, MLA_HEADS, seq, HEAD_PAD), BF16),
            jax.ShapeDtypeStruct((bsz, MLA_HEADS // 2, seq, LANES), BF16),
        ],
        scratch_shapes=[pltpu.VMEM((CONV_C_HALO + t, W_C), F32),
                        pltpu.VMEM((SUBLANES - 1, CONV_C_HALO + t, W_C), F32), pltpu.VMEM((t, W_C), F32)],
        compiler_params=_cparams(("arbitrary", "arbitrary")),
        name="odd_pre",
    )(x, mod_l, cos_t, sin_t, w_in_p, dw_w, dw_b.reshape(1, W_C), cn_g.reshape(1, W_C), cn_b.reshape(1, W_C),
      qn_g.reshape(1, Q_RANK), wq_p, wq_r, kvn_g.reshape(1, KV_RANK), wk_p, wv_p)


ATT_FULL, ATT_DIAG_LATE, ATT_DIAG_EARLY = 0, 1, 2


def _attn_kernel(kj_tab, kind_tab, q_ref, k_ref, v_ref, o_ref, m_sc, l_sc, acc_sc):
    step_id = pl.program_id(2)
    kind = kind_tab[step_id]
    tq = q_ref.shape[2]
    tk = k_ref.shape[2]

    @pl.when(kj_tab[step_id] == 0)
    def _():
        m_sc[...] = jnp.full(m_sc.shape, -jnp.inf, F32)
        l_sc[...] = jnp.zeros(l_sc.shape, F32)
        acc_sc[...] = jnp.zeros(acc_sc.shape, F32)

    first_lanes = lax.broadcasted_iota(jnp.int32, (tq, LANES), 1) < V_DIM
    n_pairs = v_ref.shape[1]

    def step(n_keys, row_shift):
        n_col = n_keys // LANES
        for pp in range(n_pairs):
            v_pair = v_ref[0, pp, 0:n_keys, :]
            alphas = []
            pvs = []
            for a in range(2 * pp, 2 * pp + 2):
                s = lax.dot_general(q_ref[0, a], k_ref[0, a, 0:n_keys, :], (((1,), (1,)), ((), ())),
                                    preferred_element_type=F32)
                if row_shift is not None:
                    rows = row_shift + lax.broadcasted_iota(jnp.int32, (tq, n_keys), 0)
                    cols = lax.broadcasted_iota(jnp.int32, (tq, n_keys), 1)
                    s = jnp.where(cols <= rows, s, -1e30)
                cols_s = [s[:, c * LANES:(c + 1) * LANES] for c in range(n_col)]
                m_lane = cols_s[0]
                for c in range(1, n_col):
                    m_lane = jnp.maximum(m_lane, cols_s[c])
                m_prev = m_sc[a]
                m_next = jnp.maximum(m_prev, jnp.max(m_lane, axis=1, keepdims=True))
                alpha = jnp.exp2(m_prev - m_next)
                ps = [jnp.exp2(cs - m_next) for cs in cols_s]
                l_lane = ps[0]
                for c in range(1, n_col):
                    l_lane = l_lane + ps[c]
                l_sc[a] = alpha * l_sc[a] + l_lane
                m_sc[a] = m_next
                alphas.append(alpha)
                p = jnp.concatenate([pc.astype(BF16) for pc in ps], axis=1)
                pvs.append(jnp.dot(p, v_pair, preferred_element_type=F32))
            alpha_sel = jnp.where(first_lanes, alphas[0], alphas[1])
            acc_sc[pp] = alpha_sel * acc_sc[pp] + jnp.where(first_lanes, pvs[0], pvs[1])

    @pl.when(kind == ATT_FULL)
    def _():
        step(tk, None)

    @pl.when(kind == ATT_DIAG_LATE)
    def _():
        step(tk, tq)

    @pl.when(kind == ATT_DIAG_EARLY)
    def _():
        step(tq, 0)

    @pl.when(kind != ATT_FULL)
    def _():
        for pp in range(n_pairs):
            l_sel = jnp.where(first_lanes, jnp.sum(l_sc[2 * pp], axis=1, keepdims=True),
                              jnp.sum(l_sc[2 * pp + 1], axis=1, keepdims=True))
            o_ref[0, :, pp * LANES:(pp + 1) * LANES] = (acc_sc[pp] / l_sel).astype(o_ref.dtype)


def _attention(q, k, v):
    bsz, heads, seq, _ = q.shape
    t = min(T_ATT, seq // 2)
    nq = seq // t
    steps = [(i, j, ATT_FULL if j < i // 2 else (ATT_DIAG_LATE if i % 2 else ATT_DIAG_EARLY))
             for i in range(nq) for j in range(i // 2 + 1)]
    qi_tab = jnp.asarray([st[0] for st in steps], jnp.int32)
    kj_tab = jnp.asarray([st[1] for st in steps], jnp.int32)
    kind_tab = jnp.asarray([st[2] for st in steps], jnp.int32)
    hs = ATT_HEADS
    grid_spec = pltpu.PrefetchScalarGridSpec(
        num_scalar_prefetch=3,
        grid=(bsz, heads // hs, len(steps)),
        in_specs=[
            pl.BlockSpec((1, hs, t, HEAD_PAD), lambda b, hg, s, qt, kt, kd: (b, hg, qt[s], 0)),
            pl.BlockSpec((1, hs, 2 * t, HEAD_PAD), lambda b, hg, s, qt, kt, kd: (b, hg, kt[s], 0)),
            pl.BlockSpec((1, hs // 2, 2 * t, LANES), lambda b, hg, s, qt, kt, kd: (b, hg, kt[s], 0)),
        ],
        out_specs=pl.BlockSpec((1, t, hs * V_DIM), lambda b, hg, s, qt, kt, kd: (b, qt[s], hg)),
        scratch_shapes=[pltpu.VMEM((hs, t, LANES), F32), pltpu.VMEM((hs, t, LANES), F32),
                        pltpu.VMEM((hs // 2, t, LANES), F32)],
    )
    return pl.pallas_call(
        _attn_kernel_entry,
        grid_spec=grid_spec,
        out_shape=jax.ShapeDtypeStruct((bsz, seq, heads * V_DIM), BF16),
        compiler_params=_cparams(("arbitrary", "arbitrary", "arbitrary")),
        name="attn",
    )(qi_tab, kj_tab, kind_tab, q, k, v)


def _attn_kernel_entry(qi_tab, kj_tab, kind_tab, *refs):
    del qi_tab
    _attn_kernel(kj_tab, kind_tab, *refs)


def _odd_post_kernel(x_ref, mod_ref, yc_ref, yd_ref, wout_ref, lng_ref, lnb_ref, o_ref):
    x = x_ref[0]
    gate = mod_ref[0, 2:3, :]
    y = (jnp.dot(yc_ref[0], wout_ref[0:W_C, :], preferred_element_type=F32)
         + jnp.dot(yd_ref[0], wout_ref[W_C:, :], preferred_element_type=F32))
    r = DEEPNORM_ALPHA * x + (1.0 + gate) * y
    o_ref[0] = _layer_norm(r, lng_ref[...], lnb_ref[...])


def _odd_post(x, mod_l, y_c, y_d, w_out, ln_g, ln_b):
    bsz, seq, d = x.shape
    t = min(T_POST, seq)
    full = lambda shape: pl.BlockSpec(shape, lambda b, i: (0,) * len(shape))
    return pl.pallas_call(
        _odd_post_kernel,
        grid=(bsz, seq // t),
        in_specs=[
            pl.BlockSpec((1, t, d), lambda b, i: (b, i, 0)),
            pl.BlockSpec((1, 6, d), lambda b, i: (b, 0, 0)),
            pl.BlockSpec((1, t, W_C), lambda b, i: (b, i, 0)),
            pl.BlockSpec((1, t, MLA_HEADS * V_DIM), lambda b, i: (b, i, 0)),
            full(w_out.shape), full((1, d)), full((1, d)),
        ],
        out_specs=pl.BlockSpec((1, t, d), lambda b, i: (b, i, 0)),
        out_shape=jax.ShapeDtypeStruct((bsz, seq, d), F32),
        compiler_params=_cparams(("arbitrary", "arbitrary")),
        name="odd_post",
    )(x, mod_l, y_c, y_d, w_out.astype(BF16), ln_g.reshape(1, d), ln_b.reshape(1, d))


def _router_kernel(x_ref, mod_ref, rw_ref, rb_ref, topi_ref, gate_ref, lrank_ref, cnt_ref):
    t_len = x_ref.shape[1]
    x = x_ref[0]
    shift, scale = mod_ref[0, 3:4, :], mod_ref[0, 4:5, :]
    u = x * (1.0 + scale) + shift
    logits = lax.dot_general(rw_ref[...], u, (((1,), (1,)), ((), ())), precision=HIGHEST,
                             preferred_element_type=F32) + rb_ref[...]
    eidx = lax.broadcasted_iota(jnp.int32, (N_EXPERTS, t_len), 0)
    vals, hots = [], []
    for _ in range(TOP_K):
        m = jnp.max(logits, axis=0, keepdims=True)
        idx = jnp.min(jnp.where(logits == m, eidx, N_EXPERTS), axis=0, keepdims=True)
        hot = eidx == idx
        vals.append(m)
        hots.append(hot)
        logits = jnp.where(hot, -jnp.inf, logits)
        topi_ref[len(vals) - 1:len(vals), :] = idx
    exps = [jnp.exp(v - vals[0]) for v in vals]
    denom = exps[0] + exps[1] + exps[2] + exps[3]
    for kk in range(TOP_K):
        gate_ref[kk:kk + 1, :] = exps[kk] / denom

    chosen = jnp.where(hots[0] | hots[1] | hots[2] | hots[3], 1.0, 0.0)
    r_i = lax.broadcasted_iota(jnp.int32, (t_len, t_len), 0)
    c_i = lax.broadcasted_iota(jnp.int32, (t_len, t_len), 1)
    before = jnp.where(r_i < c_i, 1.0, 0.0).astype(BF16)
    base = jnp.dot(chosen.astype(BF16), before, preferred_element_type=F32)
    for kk in range(TOP_K):
        lrank_ref[kk:kk + 1, :] = jnp.sum(jnp.where(hots[kk], base, 0.0), axis=0, keepdims=True).astype(jnp.int32)
    cnt_ref[0] = jnp.broadcast_to(jnp.sum(chosen, axis=1, keepdims=True), cnt_ref.shape[1:])


def _router(x, mod_l, router_w, router_b):
    bsz, seq, d = x.shape
    t = min(T_MOE, seq)
    nt = seq // t
    n_tok = bsz * seq
    full = lambda shape: pl.BlockSpec(shape, lambda b, i: (0,) * len(shape))
    tok_spec = pl.BlockSpec((TOP_K, t), lambda b, i: (0, b * nt + i))
    return pl.pallas_call(
        _router_kernel,
        grid=(bsz, nt),
        in_specs=[
            pl.BlockSpec((1, t, d), lambda b, i: (b, i, 0)),
            pl.BlockSpec((1, 6, d), lambda b, i: (b, 0, 0)),
            full((N_EXPERTS, d)), full((N_EXPERTS, 1)),
        ],
        out_specs=[tok_spec, tok_spec, tok_spec,
                   pl.BlockSpec((1, N_EXPERTS, LANES), lambda b, i: (b * nt + i, 0, 0))],
        out_shape=[
            jax.ShapeDtypeStruct((TOP_K, n_tok), jnp.int32),
            jax.ShapeDtypeStruct((TOP_K, n_tok), F32),
            jax.ShapeDtypeStruct((TOP_K, n_tok), jnp.int32),
            jax.ShapeDtypeStruct((bsz * nt, N_EXPERTS, LANES), F32),
        ],
        compiler_params=_cparams(("arbitrary", "arbitrary")),
        name="router",
    )(x, mod_l, router_w.T, router_b.reshape(N_EXPERTS, 1))


def _tile_index():
    return pl.program_id(0) * pl.num_programs(1) + pl.program_id(1)


def _for_each_chunk(tile, n_rows, chunks_per_tile, dst_ref, fn):
    def chunk(cc, carry):
        fn(pl.multiple_of(cc * SEG, SEG), pl.multiple_of(dst_ref[tile * chunks_per_tile + cc], SEG))
        return carry

    lax.fori_loop(0, n_rows // SEG, chunk, 0)


def _dispatch_kernel(dst_ref, used_ref, tstart_ref, tlen_ref,
                     x_ref, mod_ref, loc_ref, xs_ref, pm_buf, lbuf, zbuf, sem):
    tile = _tile_index()
    n_tiles = pl.num_programs(0) * pl.num_programs(1)
    t_len = x_ref.shape[1]
    n_loc = lbuf.shape[1]
    x = x_ref[0]
    shift, scale = mod_ref[0, 3:4, :], mod_ref[0, 4:5, :]
    u = (x * (1.0 + scale) + shift).astype(BF16)
    loc = loc_ref[...]
    for jc in range(n_loc // PERM_ROWS):
        j = jc * PERM_ROWS + lax.broadcasted_iota(jnp.int32, (PERM_ROWS, t_len), 0)
        pm = jnp.zeros((PERM_ROWS, t_len), F32)
        for kk in range(TOP_K):
            pm = jnp.where(j == loc[kk:kk + 1, :], 1.0, pm)
        pm_buf[jc * PERM_ROWS:(jc + 1) * PERM_ROWS, :] = pm.astype(BF16)
    slot = tile % 2
    lbuf[slot] = jnp.dot(pm_buf[...], u, preferred_element_type=F32)

    def seg_copy(s, src_row, dst_row):
        return pltpu.make_async_copy(lbuf.at[s, pl.ds(src_row, SEG)], xs_ref.at[pl.ds(dst_row, SEG)], sem)

    def drain(n_rows):
        def wait_one(cc, c2):
            seg_copy(0, 0, 0).wait()
            return c2

        lax.fori_loop(0, n_rows // SEG, wait_one, 0)

    @pl.when(tile > 0)
    def _():
        drain(used_ref[tile - 1])

    _for_each_chunk(tile, used_ref[tile], n_loc // SEG, dst_ref, lambda sr, dr: seg_copy(slot, sr, dr).start())

    @pl.when(tile == n_tiles - 1)
    def _():
        drain(used_ref[tile])
        zbuf[...] = jnp.zeros(zbuf.shape, F32)

        def fill(e, rows):
            def zero_copy(cc):
                dst = pl.multiple_of(tstart_ref[e] + cc * rows, SEG)
                return pltpu.make_async_copy(zbuf.at[pl.ds(0, rows)], xs_ref.at[pl.ds(dst, rows)], sem)

            n = tlen_ref[e] // rows

            def start(cc, c2):
                zero_copy(cc).start()
                return c2

            def wait(cc, c2):
                zero_copy(cc).wait()
                return c2

            lax.fori_loop(0, n, start, 0)
            lax.fori_loop(0, n, wait, 0)

        def per_expert(e, carry):
            fill(e, SEG)
            return carry

        lax.fori_loop(0, N_EXPERTS, per_expert, 0)
        fill(N_EXPERTS, BM)


def _dispatch(x, mod_l, loc, plan, n_slots):
    bsz, seq, d = x.shape
    t = min(T_MOE, seq)
    nt = seq // t
    n_loc = TOP_K * t + N_EXPERTS * SEG
    grid_spec = pltpu.PrefetchScalarGridSpec(
        num_scalar_prefetch=4,
        grid=(bsz, nt),
        in_specs=[
            pl.BlockSpec((1, t, d), lambda b, i, *_: (b, i, 0)),
            pl.BlockSpec((1, 6, d), lambda b, i, *_: (b, 0, 0)),
            pl.BlockSpec((TOP_K, t), lambda b, i, *_: (0, b * nt + i)),
        ],
        out_specs=pl.BlockSpec(memory_space=pl.ANY),
        scratch_shapes=[pltpu.VMEM((n_loc, t), BF16), pltpu.VMEM((2, n_loc, d), F32), pltpu.VMEM((BM, d), F32),
                        pltpu.SemaphoreType.DMA(())],
    )
    return pl.pallas_call(
        _dispatch_kernel,
        grid_spec=grid_spec,
        out_shape=jax.ShapeDtypeStruct((n_slots, d), F32),
        compiler_params=_cparams(("arbitrary", "arbitrary")),
        name="dispatch",
    )(plan["dst"], plan["used"], plan["tstart"], plan["tlen"], x, mod_l, loc)


def _expert_kernel(bexp_ref, nused_ref, xs_ref, wgu_ref, bgu_ref, wdn_ref, bdn_ref, ys_ref, wgu_bf, wdn_bf):
    i = pl.program_id(0)
    active = i < nused_ref[0]
    new_expert = (i == 0) | (bexp_ref[i] != bexp_ref[jnp.maximum(i - 1, 0)])

    @pl.when(active & new_expert)
    def _():
        wgu_bf[...] = wgu_ref[0, 0].astype(BF16)
        wdn_bf[...] = wdn_ref[0, 0].astype(BF16)

    @pl.when(active)
    def _():
        gu = jnp.dot(xs_ref[...].astype(BF16), wgu_bf[...], preferred_element_type=F32) + bgu_ref[0, 0]
        gate = jnp.minimum(gu[:, :D_EXPERT], SWIGLU_LIMIT)
        up = jnp.clip(gu[:, D_EXPERT:], -SWIGLU_LIMIT, SWIGLU_LIMIT)
        glu = gate * _sigmoid(SWIGLU_ALPHA * gate)
        h = ((up + 1.0) * glu).astype(BF16)
        ys_ref[...] = jnp.dot(h, wdn_bf[...], preferred_element_type=F32) + bdn_ref[0, 0]

    @pl.when(jnp.logical_not(active))
    def _():
        ys_ref[...] = jnp.zeros(ys_ref.shape, F32)


def _experts(xs, block_exp, n_used, layer, w_gu, b_gu, w_dn, b_dn):
    n_slots, d = xs.shape
    n_blocks = n_slots // BM
    depth = w_gu.shape[0]
    row_map = lambda i, be, nu: (jnp.maximum(jnp.minimum(i, nu[0] - 1), 0), 0)
    exp_map = lambda i, be, nu: (layer, be[i], 0, 0)
    grid_spec = pltpu.PrefetchScalarGridSpec(
        num_scalar_prefetch=2,
        grid=(n_blocks,),
        in_specs=[
            pl.BlockSpec((BM, d), row_map),
            pl.BlockSpec((1, 1, d, 2 * D_EXPERT), exp_map),
            pl.BlockSpec((1, 1, 1, 2 * D_EXPERT), exp_map),
            pl.BlockSpec((1, 1, D_EXPERT, d), exp_map),
            pl.BlockSpec((1, 1, 1, d), exp_map),
        ],
        out_specs=pl.BlockSpec((BM, d), lambda i, be, nu: (i, 0)),
        scratch_shapes=[pltpu.VMEM((d, 2 * D_EXPERT), BF16), pltpu.VMEM((D_EXPERT, d), BF16)],
    )
    return pl.pallas_call(
        _expert_kernel,
        grid_spec=grid_spec,
        out_shape=jax.ShapeDtypeStruct((n_slots, d), F32),
        compiler_params=_cparams(("arbitrary",)),
        name="experts",
    )(block_exp, n_used, xs, w_gu, b_gu.reshape(depth, N_EXPERTS, 1, 2 * D_EXPERT), w_dn,
      b_dn.reshape(depth, N_EXPERTS, 1, d))


def _combine_kernel(dst_ref, used_ref, x_ref, mod_ref, loct_ref, gatet_ref, ys_ref,
                    lng_ref, lnb_ref, o_ref, ybuf, sel_buf, sem):
    tile = _tile_index()
    n_tiles = pl.num_programs(0) * pl.num_programs(1)
    t_len = x_ref.shape[1]
    n_loc = ybuf.shape[1]
    slot = tile % 2

    def seg_copy(s, src_row, dst_row):
        return pltpu.make_async_copy(ys_ref.at[pl.ds(dst_row, SEG)], ybuf.at[s, pl.ds(src_row, SEG)], sem.at[s])

    def fetch(t, s):
        used_t = used_ref[t]
        _for_each_chunk(t, used_t, n_loc // SEG, dst_ref, lambda sr, dr: seg_copy(s, sr, dr).start())

        def zero_rows(cc, c2):
            ybuf[s, pl.ds(pl.multiple_of(used_t + cc * SEG, SEG), SEG), :] = jnp.zeros((SEG, ybuf.shape[2]), F32)
            return c2

        lax.fori_loop(0, (n_loc - used_t) // SEG, zero_rows, 0)

    @pl.when(tile == 0)
    def _():
        fetch(0, 0)

    @pl.when(tile + 1 < n_tiles)
    def _():
        fetch(tile + 1, 1 - slot)

    loct = loct_ref[...]
    gatet = gatet_ref[...]
    for jc in range(n_loc // LANES):
        j = jc * LANES + lax.broadcasted_iota(jnp.int32, (t_len, LANES), 1)
        sel = jnp.zeros((t_len, LANES), F32)
        for kk in range(TOP_K):
            sel = jnp.where(j == loct[:, kk:kk + 1], gatet[:, kk:kk + 1], sel)
        sel_buf[:, jc * LANES:(jc + 1) * LANES] = sel.astype(BF16)

    def drain(cc, c2):
        seg_copy(slot, 0, 0).wait()
        return c2

    lax.fori_loop(0, used_ref[tile] // SEG, drain, 0)

    y = jnp.dot(sel_buf[...], ybuf[slot].astype(BF16), preferred_element_type=F32)
    x = x_ref[0]
    gate_f = mod_ref[0, 5:6, :]
    r = DEEPNORM_ALPHA * x + (1.0 + gate_f) * y
    o_ref[0] = _layer_norm(r, lng_ref[...], lnb_ref[...])


def _combine(x, mod_l, loc_t, gates_t, plan, ys, ln_g, ln_b):
    bsz, seq, d = x.shape
    t = min(T_MOE, seq)
    nt = seq // t
    n_loc = TOP_K * t + N_EXPERTS * SEG
    full = lambda shape: pl.BlockSpec(shape, lambda b, i, *_: (0,) * len(shape))
    grid_spec = pltpu.PrefetchScalarGridSpec(
        num_scalar_prefetch=2,
        grid=(bsz, nt),
        in_specs=[
            pl.BlockSpec((1, t, d), lambda b, i, *_: (b, i, 0)),
            pl.BlockSpec((1, 6, d), lambda b, i, *_: (b, 0, 0)),
            pl.BlockSpec((t, TOP_K), lambda b, i, *_: (b * nt + i, 0)),
            pl.BlockSpec((t, TOP_K), lambda b, i, *_: (b * nt + i, 0)),
            pl.BlockSpec(memory_space=pl.ANY),
            full((1, d)), full((1, d)),
        ],
        out_specs=pl.BlockSpec((1, t, d), lambda b, i, *_: (b, i, 0)),
        scratch_shapes=[pltpu.VMEM((2, n_loc, d), F32), pltpu.VMEM((t, n_loc), BF16),
                        pltpu.SemaphoreType.DMA((2,))],
    )
    return pl.pallas_call(
        _combine_kernel,
        grid_spec=grid_spec,
        out_shape=jax.ShapeDtypeStruct((bsz, seq, d), F32),
        compiler_params=_cparams(("arbitrary", "arbitrary")),
        name="combine",
    )(plan["dst"], plan["used"], x, mod_l, loc_t, gates_t, ys,
      ln_g.reshape(1, d), ln_b.reshape(1, d))


def _slots_kernel(loc0_ref, topi_ref, lrank_ref, loc_ref):
    tile = pl.program_id(0)
    topi = topi_ref[...]
    loc = lrank_ref[...]
    for e in range(N_EXPERTS):
        loc = loc + jnp.where(topi == e, loc0_ref[tile * N_EXPERTS + e], 0)
    loc_ref[...] = loc


def _slots(loc0, topi, lrank, t):
    k, n = topi.shape
    spec = pl.BlockSpec((k, t), lambda i, l0: (0, i))
    return pl.pallas_call(
        _slots_kernel,
        grid_spec=pltpu.PrefetchScalarGridSpec(num_scalar_prefetch=1, grid=(n // t,), in_specs=[spec, spec],
                                               out_specs=spec),
        out_shape=jax.ShapeDtypeStruct((k, n), jnp.int32),
        compiler_params=_cparams(("arbitrary",)),
        name="slots",
    )(loc0, topi, lrank)


def _moe_plan(cnt, n_slots, n_loc):
    c = cnt[:, :, 0].astype(jnp.int32)
    p = (c + SEG - 1) // SEG * SEG
    tot = jnp.sum(p, axis=0)
    padded = (tot + BM - 1) // BM * BM
    pad_end = jnp.cumsum(padded)
    pad_start = pad_end - padded
    seg = pad_start[None, :] + jnp.cumsum(p, axis=0) - p
    loc0 = jnp.cumsum(p, axis=1) - p
    n_blocks = n_slots // BM
    block_lo = jnp.arange(n_blocks, dtype=jnp.int32) * BM
    block_exp = jnp.minimum(jnp.sum((pad_end[None, :] <= block_lo[:, None]).astype(jnp.int32), axis=1),
                            N_EXPERTS - 1)
    chunk_lo = jnp.arange(n_loc // SEG, dtype=jnp.int32) * SEG
    chunk_exp = jnp.sum(((loc0 + p)[:, :, None] <= chunk_lo[None, None, :]).astype(jnp.int32), axis=1)
    owner = chunk_exp[:, None, :] == jnp.arange(N_EXPERTS, dtype=jnp.int32)[None, :, None]
    dst = jnp.sum(jnp.where(owner, (seg - loc0)[:, :, None], 0), axis=1) + chunk_lo[None, :]
    return {
        "dst": dst.reshape(-1), "loc0": loc0.reshape(-1), "used": jnp.sum(p, axis=1),
        "tstart": jnp.concatenate([pad_start + tot, pad_end[-1:]]),
        "tlen": jnp.concatenate([padded - tot, n_slots - pad_end[-1:]]),
        "n_used": pad_end[-1:] // BM, "block_exp": block_exp,
    }


def _moe_layer(x, mod_l, layer, router_w, router_b, w_gu, b_gu, w_dn, b_dn, ln_g, ln_b):
    bsz, seq, d = x.shape
    n_tok = bsz * seq
    t = min(T_MOE, seq)
    n_slots = n_tok * TOP_K + (n_tok // t) * N_EXPERTS * SEG + N_EXPERTS * BM
    topi, gates, lrank, cnt = _router(x, mod_l, router_w, router_b)
    plan = _moe_plan(cnt, n_slots, TOP_K * t + N_EXPERTS * SEG)
    loc = _slots(plan["loc0"], topi, lrank, t)
    xs = _dispatch(x, mod_l, loc, plan, n_slots)
    ys = _experts(xs, plan["block_exp"], plan["n_used"], layer, w_gu, b_gu, w_dn, b_dn)
    return _combine(x, mod_l, loc.T, gates.T, plan, ys, ln_g, ln_b)


def kernel(x, c, positions, ada_w, ada_b, ln_mix_g, ln_mix_b, ln_ffn_g, ln_ffn_b, ev_w_in, ev_conv_w, ev_sg_w, ev_sg_b, ev_vn_g, ev_vn_b, ev_w_out, od_w_in, od_dw_w, od_dw_b, od_cn_g, od_cn_b, od_qn_g, od_w_uq, od_kvn_g, od_w_ukv, od_w_out, moe_router_w, moe_router_b, moe_w_gu, moe_b_gu, moe_w_dn, moe_b_dn):
    bsz, seq, d = x.shape
    depth = ada_w.shape[0]
    mod = _ada(c, ada_w, ada_b).reshape(depth, bsz, 6, d)
    cos_t, sin_t = _rope_tables(positions)
    for layer in range(depth):
        i = layer // 2
        mod_l = mod[layer]
        if layer % 2 == 0:
            x = _even_layer(x, mod_l, ev_w_in[i], ev_conv_w[i], ev_sg_w[i], ev_sg_b[i], ev_vn_g[i], ev_vn_b[i],
                            ev_w_out[i], ln_mix_g[layer], ln_mix_b[layer])
        else:
            y_c, q, k, v = _odd_pre(x, mod_l, cos_t, sin_t, od_w_in[i], od_dw_w[i], od_dw_b[i], od_cn_g[i],
                                    od_cn_b[i], od_qn_g[i], od_w_uq[i], od_kvn_g[i], od_w_ukv[i])
            y_d = _attention(q, k, v)
            x = _odd_post(x, mod_l, y_c, y_d, od_w_out[i], ln_mix_g[layer], ln_mix_b[layer])
        x = _moe_layer(x, mod_l, layer, moe_router_w[layer], moe_router_b[layer], moe_w_gu, moe_b_gu,
                       moe_w_dn, moe_b_dn, ln_ffn_g[layer], ln_ffn_b[layer])
    return x
```

```python
import functools
import math

import jax
import jax.numpy as jnp
from jax import lax
from jax.experimental import pallas as pl
from jax.experimental.pallas import tpu as pltpu

F32 = jnp.float32
BF16 = jnp.bfloat16
U32 = jnp.uint32
HIGHEST = lax.Precision.HIGHEST

D_MODEL = 1024
DEPTH = 4
W_A = 512
W_B = 512
SG_HEADS = 8
SG_HEAD_DIM = 64
CHUNK = 128
CONV_A = 3
W_C = 512
CONV_C = 31
MLA_HEADS = 8
QK_NOPE = 64
QK_ROPE = 32
V_DIM = 64
Q_RANK = 256
KV_RANK = 128
ROPE_THETA = 10000.0
N_EXPERTS = 32
TOP_K = 4
D_EXPERT = 1024
SWIGLU_LIMIT = 7.0
SWIGLU_ALPHA = 1.702
DEEPNORM_ALPHA = (2.0 * DEPTH) ** 0.25
LN_EPS = 1e-5
RMS_EPS = 1e-6

LANES = 128
SUBLANES = 8
HEAD_PAD = 128
CONV_C_HALO = 32
CONV_A_HALO = 8
CONV_ROWS = 128

T_EVEN = 512
T_ODD = 256
T_POST = 512
T_ATT = 512
ATT_HEADS = 8
T_MOE = 512
BM = 512
SEG = 8
PERM_ROWS = 256
VMEM_LIMIT = 56 * 1024 * 1024


def _cparams(sem):
    return pltpu.CompilerParams(dimension_semantics=sem, vmem_limit_bytes=VMEM_LIMIT)


def _layer_norm(v, g, b):
    mu = jnp.mean(v, axis=-1, keepdims=True)
    d = v - mu
    var = jnp.mean(d * d, axis=-1, keepdims=True)
    return d * lax.rsqrt(var + LN_EPS) * g + b


def _gelu(v):
    return 0.5 * v * (1.0 + lax.erf(v * (1.0 / math.sqrt(2.0))))


def _sigmoid(v):
    return 1.0 / (1.0 + jnp.exp(-v))


HIGH_HALF = 0xFFFF0000


def _pack_pairs(v):
    n = v.shape[1] // 2
    bits = lax.bitcast_convert_type(v, U32)
    return (bits[:, :n] >> 16) | (bits[:, n:] & jnp.uint32(HIGH_HALF))


def _unpack_pairs(w):
    lo = lax.bitcast_convert_type(w << 16, F32).astype(BF16)
    hi = lax.bitcast_convert_type(w & jnp.uint32(HIGH_HALF), F32).astype(BF16)
    return lo, hi


def _ada_kernel(c_ref, w_ref, b_ref, o_ref):
    c = c_ref[...]
    cond = c * _sigmoid(c)
    o_ref[0] = jnp.dot(cond, w_ref[0], precision=HIGHEST, preferred_element_type=F32) + b_ref[0]


def _ada(c, ada_w, ada_b):
    bsz, d = c.shape
    depth = ada_w.shape[0]
    n_chunk = ada_w.shape[2] // d
    return pl.pallas_call(
        _ada_kernel,
        grid=(depth, n_chunk),
        in_specs=[
            pl.BlockSpec((bsz, d), lambda l, j: (0, 0)),
            pl.BlockSpec((1, d, d), lambda l, j: (l, 0, j)),
            pl.BlockSpec((1, 1, d), lambda l, j: (l, 0, j)),
        ],
        out_specs=pl.BlockSpec((1, bsz, d), lambda l, j: (l, 0, j)),
        out_shape=jax.ShapeDtypeStruct((depth, bsz, n_chunk * d), F32),
        compiler_params=_cparams(("arbitrary", "arbitrary")),
        name="ada",
    )(c, ada_w, ada_b.reshape(depth, 1, n_chunk * d))


def _rope_kernel(pos_ref, freq_ref, cos_ref, sin_ref):
    ang = pos_ref[...].astype(F32) * freq_ref[...]
    cos_ref[...] = jnp.cos(ang)
    sin_ref[...] = jnp.sin(ang)


def _rope_tables(positions):
    n = positions.size
    t = 1024
    inv_freq = ROPE_THETA ** (-jnp.arange(0, QK_ROPE, 2, dtype=F32) / QK_ROPE)
    half = QK_ROPE // 2
    freq = jnp.zeros((1, LANES), F32).at[0, QK_NOPE:QK_NOPE + QK_ROPE].set(jnp.tile(inv_freq, 2))
    del half
    return pl.pallas_call(
        _rope_kernel,
        grid=(n // t,),
        in_specs=[pl.BlockSpec((t, 1), lambda i: (i, 0)), pl.BlockSpec((1, LANES), lambda i: (0, 0))],
        out_specs=[pl.BlockSpec((t, LANES), lambda i: (i, 0))] * 2,
        out_shape=[jax.ShapeDtypeStruct((n, LANES), F32)] * 2,
        compiler_params=_cparams(("arbitrary",)),
        name="rope",
    )(positions.reshape(n, 1), freq)


def _even_kernel(x_ref, mod_ref, win_ref, cw_ref, sgw_ref, sgb_ref, vng_ref, vnb_ref, wout_ref,
                 lng_ref, lnb_ref, o_ref, gbuf, ybuf):
    t_len = x_ref.shape[1]
    halo = CONV_A_HALO

    @pl.when(pl.program_id(1) == 0)
    def _():
        gbuf[0:halo, :] = jnp.zeros((halo, W_A), F32)

    x = x_ref[0]
    shift, scale, gate = mod_ref[0, 0:1, :], mod_ref[0, 1:2, :], mod_ref[0, 2:3, :]
    u = x * (1.0 + scale) + shift
    proj = jnp.dot(u.astype(BF16), win_ref[...], preferred_element_type=F32)
    b_gate = proj[:, 0:W_A]
    c_gate = proj[:, W_A:2 * W_A]
    xa = proj[:, 2 * W_A:3 * W_A]
    zu = proj[:, 3 * W_A:3 * W_A + W_B]
    zv = proj[:, 3 * W_A + W_B:3 * W_A + 2 * W_B]

    g = c_gate * xa
    gbuf[halo:halo + t_len, :] = g
    conv = (cw_ref[0:1, :] * gbuf[halo - 2:halo - 2 + t_len, :]
            + cw_ref[1:2, :] * gbuf[halo - 1:halo - 1 + t_len, :]
            + cw_ref[2:3, :] * g)
    gbuf[0:halo, :] = g[t_len - halo:t_len, :]
    ybuf[:, 0:W_A] = (b_gate * conv).astype(BF16)

    zu = _gelu(zu)
    zv = _layer_norm(_gelu(zv), vng_ref[...], vnb_ref[...]).astype(BF16)
    row = lax.broadcasted_iota(jnp.int32, (CHUNK, CHUNK), 0)
    col = lax.broadcasted_iota(jnp.int32, (CHUNK, CHUNK), 1)
    w_stack = jnp.concatenate(
        [jnp.where(row >= col, sgw_ref[h], 0.0).astype(BF16) for h in range(SG_HEADS)], axis=0)
    col_head = lax.broadcasted_iota(jnp.int32, (CHUNK, W_B), 1) // SG_HEAD_DIM
    for ci in range(t_len // CHUNK):
        lo = ci * CHUNK
        full = jnp.dot(w_stack, zv[lo:lo + CHUNK, :], preferred_element_type=F32)
        mixed = sgb_ref[...]
        for h in range(SG_HEADS):
            mixed = mixed + jnp.where(col_head == h, full[h * CHUNK:(h + 1) * CHUNK, :], 0.0)
        ybuf[lo:lo + CHUNK, W_A:W_A + W_B] = (zu[lo:lo + CHUNK, :] * mixed).astype(BF16)

    y = jnp.dot(ybuf[...], wout_ref[...], preferred_element_type=F32)
    r = DEEPNORM_ALPHA * x + (1.0 + gate) * y
    o_ref[0] = _layer_norm(r, lng_ref[...], lnb_ref[...])


def _even_layer(x, mod_l, w_in, conv_w, sg_w, sg_b, vn_g, vn_b, w_out, ln_g, ln_b):
    bsz, seq, d = x.shape
    t = min(T_EVEN, seq)
    sgb_full = jnp.repeat(sg_b.T, SG_HEAD_DIM, axis=1)
    full = lambda shape: pl.BlockSpec(shape, lambda b, i: (0,) * len(shape))
    return pl.pallas_call(
        _even_kernel,
        grid=(bsz, seq // t),
        in_specs=[
            pl.BlockSpec((1, t, d), lambda b, i: (b, i, 0)),
            pl.BlockSpec((1, 6, d), lambda b, i: (b, 0, 0)),
            full(w_in.shape), full(conv_w.shape), full(sg_w.shape), full(sgb_full.shape),
            full((1, W_B)), full((1, W_B)), full(w_out.shape), full((1, d)), full((1, d)),
        ],
        out_specs=pl.BlockSpec((1, t, d), lambda b, i: (b, i, 0)),
        out_shape=jax.ShapeDtypeStruct((bsz, seq, d), F32),
        scratch_shapes=[pltpu.VMEM((CONV_A_HALO + t, W_A), F32), pltpu.VMEM((t, W_A + W_B), BF16)],
        compiler_params=_cparams(("arbitrary", "arbitrary")),
        name="even",
    )(x, mod_l, w_in.astype(BF16), conv_w, sg_w, sgb_full, vn_g.reshape(1, W_B), vn_b.reshape(1, W_B),
      w_out.astype(BF16), ln_g.reshape(1, d), ln_b.reshape(1, d))


N_GLU = 2 * W_C
OFF_Q = N_GLU
OFF_KV = OFF_Q + Q_RANK
OFF_KR = OFF_KV + KV_RANK
OFF_KR_ROT = OFF_KR + HEAD_PAD
ODD_IN_PAD = OFF_KR_ROT + HEAD_PAD


def _odd_pre_kernel(x_ref, mod_ref, cos_ref, sin_ref, win_ref, dww_ref, dwb_ref, cng_ref, cnb_ref,
                    qng_ref, wq_ref, wqr_ref, kvng_ref, wk_ref, wv_ref,
                    yc_ref, q_ref, k_ref, v_ref, gbuf, sbuf, hbuf):
    t_len = x_ref.shape[1]
    halo = CONV_C_HALO

    @pl.when(pl.program_id(1) == 0)
    def _():
        gbuf[0:halo, :] = jnp.zeros((halo, W_C), F32)

    x = x_ref[0]
    shift, scale = mod_ref[0, 0:1, :], mod_ref[0, 1:2, :]
    u = x * (1.0 + scale) + shift
    proj = jnp.dot(u.astype(BF16), win_ref[...], preferred_element_type=F32)

    g = proj[:, 0:W_C] * _sigmoid(proj[:, W_C:2 * W_C])
    gbuf[halo:halo + t_len, :] = g
    first = halo - (CONV_C - 1)
    for res in range(1, SUBLANES):
        span = max(first + k - res for k in range(CONV_C) if (first + k) % SUBLANES == res) + t_len
        sbuf[res - 1, 0:span, :] = gbuf[res:res + span, :]
    for rb in range(t_len // CONV_ROWS):
        for cb in range(W_C // LANES):
            cs = slice(cb * LANES, (cb + 1) * LANES)
            acc = jnp.broadcast_to(dwb_ref[:, cs], (CONV_ROWS, LANES))
            for k in range(CONV_C):
                res = (first + k) % SUBLANES
                lo = first + k - res + rb * CONV_ROWS
                tap = gbuf[lo:lo + CONV_ROWS, cs] if res == 0 else sbuf[res - 1, lo:lo + CONV_ROWS, cs]
                acc = acc + dww_ref[k:k + 1, cs] * tap
            hbuf[rb * CONV_ROWS:(rb + 1) * CONV_ROWS, cs] = acc
    acc = hbuf[...]
    gbuf[0:halo, :] = g[t_len - halo:t_len, :]
    hn = _layer_norm(acc, cng_ref[...], cnb_ref[...])
    yc_ref[0] = (hn * _sigmoid(hn)).astype(BF16)

    cos = cos_ref[...]
    sin = sin_ref[...]
    cq = proj[:, OFF_Q:OFF_Q + Q_RANK]
    q_lat = (cq * lax.rsqrt(jnp.mean(cq * cq, axis=-1, keepdims=True) + RMS_EPS) * qng_ref[...]).astype(BF16)
    ckv = proj[:, OFF_KV:OFF_KV + KV_RANK]
    kv_lat = (ckv * lax.rsqrt(jnp.mean(ckv * ckv, axis=-1, keepdims=True) + RMS_EPS) * kvng_ref[...]).astype(BF16)
    q_all = jnp.dot(q_lat, wq_ref[...], preferred_element_type=F32)
    q_rot = jnp.dot(q_lat, wqr_ref[...], preferred_element_type=F32)
    k_all = jnp.dot(kv_lat, wk_ref[...], preferred_element_type=F32)
    v_all = jnp.dot(kv_lat, wv_ref[...], preferred_element_type=F32)
    k_rope = proj[:, OFF_KR:OFF_KR + HEAD_PAD] * cos + proj[:, OFF_KR_ROT:OFF_KR_ROT + HEAD_PAD] * sin
    sm_scale = math.log2(math.e) / math.sqrt(QK_NOPE + QK_ROPE)
    for h in range(MLA_HEADS):
        sl = slice(h * HEAD_PAD, (h + 1) * HEAD_PAD)
        q_ref[0, h] = ((q_all[:, sl] * cos + q_rot[:, sl] * sin) * sm_scale).astype(BF16)
        k_ref[0, h] = (k_all[:, sl] + k_rope).astype(BF16)
    for hp in range(MLA_HEADS // 2):
        v_ref[0, hp] = v_all[:, hp * LANES:(hp + 1) * LANES].astype(BF16)


def _odd_weights(w_in, w_uq, w_ukv):
    d = w_in.shape[0]
    half = QK_ROPE // 2
    kr = w_in[:, OFF_KR:OFF_KR + QK_ROPE]
    z = lambda n: jnp.zeros((d, n), w_in.dtype)
    kr_blk = jnp.concatenate([z(QK_NOPE), kr, z(HEAD_PAD - QK_NOPE - QK_ROPE)], axis=1)
    kr_rot = jnp.concatenate([z(QK_NOPE), -kr[:, half:], kr[:, :half], z(HEAD_PAD - QK_NOPE - QK_ROPE)], axis=1)
    w_in_p = jnp.concatenate([w_in[:, :OFF_KR], kr_blk, kr_rot], axis=1).astype(BF16)

    dq = QK_NOPE + QK_ROPE
    wq = w_uq.reshape(Q_RANK, MLA_HEADS, dq)
    zq = lambda n: jnp.zeros((Q_RANK, MLA_HEADS, n), w_uq.dtype)
    wq_p = jnp.concatenate([wq, zq(HEAD_PAD - dq)], axis=2).reshape(Q_RANK, MLA_HEADS * HEAD_PAD)
    wq_r = jnp.concatenate([zq(QK_NOPE), -wq[:, :, QK_NOPE + half:], wq[:, :, QK_NOPE:QK_NOPE + half],
                            zq(HEAD_PAD - dq)], axis=2).reshape(Q_RANK, MLA_HEADS * HEAD_PAD)

    wkv = w_ukv.reshape(KV_RANK, MLA_HEADS, QK_NOPE + V_DIM)
    wk_p = jnp.concatenate([wkv[:, :, :QK_NOPE], jnp.zeros((KV_RANK, MLA_HEADS, HEAD_PAD - QK_NOPE), w_ukv.dtype)],
                           axis=2).reshape(KV_RANK, MLA_HEADS * HEAD_PAD)
    wv_p = wkv[:, :, QK_NOPE:].reshape(KV_RANK, MLA_HEADS * V_DIM)
    return w_in_p, wq_p.astype(BF16), wq_r.astype(BF16), wk_p.astype(BF16), wv_p.astype(BF16)


def _odd_pre(x, mod_l, cos_t, sin_t, w_in, dw_w, dw_b, cn_g, cn_b, qn_g, w_uq, kvn_g, w_ukv):
    bsz, seq, d = x.shape
    t = min(T_ODD, seq)
    nt = seq // t
    w_in_p, wq_p, wq_r, wk_p, wv_p = _odd_weights(w_in, w_uq, w_ukv)
    full = lambda shape: pl.BlockSpec(shape, lambda b, i: (0,) * len(shape))
    return pl.pallas_call(
        _odd_pre_kernel,
        grid=(bsz, nt),
        in_specs=[
            pl.BlockSpec((1, t, d), lambda b, i: (b, i, 0)),
            pl.BlockSpec((1, 6, d), lambda b, i: (b, 0, 0)),
            pl.BlockSpec((t, LANES), lambda b, i: (b * nt + i, 0)),
            pl.BlockSpec((t, LANES), lambda b, i: (b * nt + i, 0)),
            full(w_in_p.shape), full(dw_w.shape), full((1, W_C)), full((1, W_C)), full((1, W_C)),
            full((1, Q_RANK)), full(wq_p.shape), full(wq_r.shape), full((1, KV_RANK)),
            full(wk_p.shape), full(wv_p.shape),
        ],
        out_specs=[
            pl.BlockSpec((1, t, W_C), lambda b, i: (b, i, 0)),
            pl.BlockSpec((1, MLA_HEADS, t, HEAD_PAD), lambda b, i: (b, 0, i, 0)),
            pl.BlockSpec((1, MLA_HEADS, t, HEAD_PAD), lambda b, i: (b, 0, i, 0)),
            pl.BlockSpec((1, MLA_HEADS // 2, t, LANES), lambda b, i: (b, 0, i, 0)),
        ],
        out_shape=[
            jax.ShapeDtypeStruct((bsz, seq, W_C), BF16),
            jax.ShapeDtypeStruct((bsz, MLA_HEADS, seq, HEAD_PAD), BF16),
            jax.ShapeDtypeStruct((bsz, MLA_HEADS, seq, HEAD_PAD), BF16),
            jax.ShapeDtypeStruct((bsz, MLA_HEADS // 2, seq, LANES), BF16),
        ],
        scratch_shapes=[pltpu.VMEM((CONV_C_HALO + t, W_C), F32),
                        pltpu.VMEM((SUBLANES - 1, CONV_C_HALO + t, W_C), F32), pltpu.VMEM((t, W_C), F32)],
        compiler_params=_cparams(("arbitrary", "arbitrary")),
        name="odd_pre",
    )(x, mod_l, cos_t, sin_t, w_in_p, dw_w, dw_b.reshape(1, W_C), cn_g.reshape(1, W_C), cn_b.reshape(1, W_C),
      qn_g.reshape(1, Q_RANK), wq_p, wq_r, kvn_g.reshape(1, KV_RANK), wk_p, wv_p)


ATT_FULL, ATT_DIAG_LATE, ATT_DIAG_EARLY = 0, 1, 2


def _attn_kernel(kj_tab, kind_tab, q_ref, k_ref, v_ref, o_ref, m_sc, l_sc, acc_sc):
    step_id = pl.program_id(2)
    kind = kind_tab[step_id]
    tq = q_ref.shape[2]
    tk = k_ref.shape[2]

    @pl.when(kj_tab[step_id] == 0)
    def _():
        m_sc[...] = jnp.full(m_sc.shape, -jnp.inf, F32)
        l_sc[...] = jnp.zeros(l_sc.shape, F32)
        acc_sc[...] = jnp.zeros(acc_sc.shape, F32)

    first_lanes = lax.broadcasted_iota(jnp.int32, (tq, LANES), 1) < V_DIM
    n_pairs = v_ref.shape[1]

    def step(n_keys, row_shift):
        n_col = n_keys // LANES
        for pp in range(n_pairs):
            v_pair = v_ref[0, pp, 0:n_keys, :]
            alphas = []
            pvs = []
            for a in range(2 * pp, 2 * pp + 2):
                s = lax.dot_general(q_ref[0, a], k_ref[0, a, 0:n_keys, :], (((1,), (1,)), ((), ())),
                                    preferred_element_type=F32)
                if row_shift is not None:
                    rows = row_shift + lax.broadcasted_iota(jnp.int32, (tq, n_keys), 0)
                    cols = lax.broadcasted_iota(jnp.int32, (tq, n_keys), 1)
                    s = jnp.where(cols <= rows, s, -1e30)
                cols_s = [s[:, c * LANES:(c + 1) * LANES] for c in range(n_col)]
                m_lane = cols_s[0]
                for c in range(1, n_col):
                    m_lane = jnp.maximum(m_lane, cols_s[c])
                m_prev = m_sc[a]
                m_next = jnp.maximum(m_prev, jnp.max(m_lane, axis=1, keepdims=True))
                alpha = jnp.exp2(m_prev - m_next)
                ps = [jnp.exp2(cs - m_next) for cs in cols_s]
                l_lane = ps[0]
                for c in range(1, n_col):
                    l_lane = l_lane + ps[c]
                l_sc[a] = alpha * l_sc[a] + l_lane
                m_sc[a] = m_next
                alphas.append(alpha)
                p = jnp.concatenate([pc.astype(BF16) for pc in ps], axis=1)
                pvs.append(jnp.dot(p, v_pair, preferred_element_type=F32))
            alpha_sel = jnp.where(first_lanes, alphas[0], alphas[1])
            acc_sc[pp] = alpha_sel * acc_sc[pp] + jnp.where(first_lanes, pvs[0], pvs[1])

    @pl.when(kind == ATT_FULL)
    def _():
        step(tk, None)

    @pl.when(kind == ATT_DIAG_LATE)
    def _():
        step(tk, tq)

    @pl.when(kind == ATT_DIAG_EARLY)
    def _():
        step(tq, 0)

    @pl.when(kind != ATT_FULL)
    def _():
        for pp in range(n_pairs):
            l_sel = jnp.where(first_lanes, jnp.sum(l_sc[2 * pp], axis=1, keepdims=True),
                              jnp.sum(l_sc[2 * pp + 1], axis=1, keepdims=True))
            o_ref[0, :, pp * LANES:(pp + 1) * LANES] = (acc_sc[pp] / l_sel).astype(o_ref.dtype)


def _attention(q, k, v):
    bsz, heads, seq, _ = q.shape
    t = min(T_ATT, seq // 2)
    nq = seq // t
    steps = [(i, j, ATT_FULL if j < i // 2 else (ATT_DIAG_LATE if i % 2 else ATT_DIAG_EARLY))
             for i in range(nq) for j in range(i // 2 + 1)]
    qi_tab = jnp.asarray([st[0] for st in steps], jnp.int32)
    kj_tab = jnp.asarray([st[1] for st in steps], jnp.int32)
    kind_tab = jnp.asarray([st[2] for st in steps], jnp.int32)
    hs = ATT_HEADS
    grid_spec = pltpu.PrefetchScalarGridSpec(
        num_scalar_prefetch=3,
        grid=(bsz, heads // hs, len(steps)),
        in_specs=[
            pl.BlockSpec((1, hs, t, HEAD_PAD), lambda b, hg, s, qt, kt, kd: (b, hg, qt[s], 0)),
            pl.BlockSpec((1, hs, 2 * t, HEAD_PAD), lambda b, hg, s, qt, kt, kd: (b, hg, kt[s], 0)),
            pl.BlockSpec((1, hs // 2, 2 * t, LANES), lambda b, hg, s, qt, kt, kd: (b, hg, kt[s], 0)),
        ],
        out_specs=pl.BlockSpec((1, t, hs * V_DIM), lambda b, hg, s, qt, kt, kd: (b, qt[s], hg)),
        scratch_shapes=[pltpu.VMEM((hs, t, LANES), F32), pltpu.VMEM((hs, t, LANES), F32),
                        pltpu.VMEM((hs // 2, t, LANES), F32)],
    )
    return pl.pallas_call(
        _attn_kernel_entry,
        grid_spec=grid_spec,
        out_shape=jax.ShapeDtypeStruct((bsz, seq, heads * V_DIM), BF16),
        compiler_params=_cparams(("arbitrary", "arbitrary", "arbitrary")),
        name="attn",
    )(qi_tab, kj_tab, kind_tab, q, k, v)


def _attn_kernel_entry(qi_tab, kj_tab, kind_tab, *refs):
    del qi_tab
    _attn_kernel(kj_tab, kind_tab, *refs)


def _odd_post_kernel(x_ref, mod_ref, yc_ref, yd_ref, wout_ref, lng_ref, lnb_ref, o_ref):
    x = x_ref[0]
    gate = mod_ref[0, 2:3, :]
    y = (jnp.dot(yc_ref[0], wout_ref[0:W_C, :], preferred_element_type=F32)
         + jnp.dot(yd_ref[0], wout_ref[W_C:, :], preferred_element_type=F32))
    r = DEEPNORM_ALPHA * x + (1.0 + gate) * y
    o_ref[0] = _layer_norm(r, lng_ref[...], lnb_ref[...])


def _odd_post(x, mod_l, y_c, y_d, w_out, ln_g, ln_b):
    bsz, seq, d = x.shape
    t = min(T_POST, seq)
    full = lambda shape: pl.BlockSpec(shape, lambda b, i: (0,) * len(shape))
    return pl.pallas_call(
        _odd_post_kernel,
        grid=(bsz, seq // t),
        in_specs=[
            pl.BlockSpec((1, t, d), lambda b, i: (b, i, 0)),
            pl.BlockSpec((1, 6, d), lambda b, i: (b, 0, 0)),
            pl.BlockSpec((1, t, W_C), lambda b, i: (b, i, 0)),
            pl.BlockSpec((1, t, MLA_HEADS * V_DIM), lambda b, i: (b, i, 0)),
            full(w_out.shape), full((1, d)), full((1, d)),
        ],
        out_specs=pl.BlockSpec((1, t, d), lambda b, i: (b, i, 0)),
        out_shape=jax.ShapeDtypeStruct((bsz, seq, d), F32),
        compiler_params=_cparams(("arbitrary", "arbitrary")),
        name="odd_post",
    )(x, mod_l, y_c, y_d, w_out.astype(BF16), ln_g.reshape(1, d), ln_b.reshape(1, d))


def _router_kernel(x_ref, mod_ref, rw_ref, rb_ref, topi_ref, gate_ref, lrank_ref, cnt_ref):
    t_len = x_ref.shape[1]
    x = x_ref[0]
    shift, scale = mod_ref[0, 3:4, :], mod_ref[0, 4:5, :]
    u = x * (1.0 + scale) + shift
    logits = lax.dot_general(rw_ref[...], u, (((1,), (1,)), ((), ())), precision=HIGHEST,
                             preferred_element_type=F32) + rb_ref[...]
    eidx = lax.broadcasted_iota(jnp.int32, (N_EXPERTS, t_len), 0)
    vals, hots = [], []
    for _ in range(TOP_K):
        m = jnp.max(logits, axis=0, keepdims=True)
        idx = jnp.min(jnp.where(logits == m, eidx, N_EXPERTS), axis=0, keepdims=True)
        hot = eidx == idx
        vals.append(m)
        hots.append(hot)
        logits = jnp.where(hot, -jnp.inf, logits)
        topi_ref[len(vals) - 1:len(vals), :] = idx
    exps = [jnp.exp(v - vals[0]) for v in vals]
    denom = exps[0] + exps[1] + exps[2] + exps[3]
    for kk in range(TOP_K):
        gate_ref[kk:kk + 1, :] = exps[kk] / denom

    chosen = jnp.where(hots[0] | hots[1] | hots[2] | hots[3], 1.0, 0.0)
    r_i = lax.broadcasted_iota(jnp.int32, (t_len, t_len), 0)
    c_i = lax.broadcasted_iota(jnp.int32, (t_len, t_len), 1)
    before = jnp.where(r_i < c_i, 1.0, 0.0).astype(BF16)
    base = jnp.dot(chosen.astype(BF16), before, preferred_element_type=F32)
    for kk in range(TOP_K):
        lrank_ref[kk:kk + 1, :] = jnp.sum(jnp.where(hots[kk], base, 0.0), axis=0, keepdims=True).astype(jnp.int32)
    cnt_ref[0] = jnp.broadcast_to(jnp.sum(chosen, axis=1, keepdims=True), cnt_ref.shape[1:])


def _router(x, mod_l, router_w, router_b):
    bsz, seq, d = x.shape
    t = min(T_MOE, seq)
    nt = seq // t
    n_tok = bsz * seq
    full = lambda shape: pl.BlockSpec(shape, lambda b, i: (0,) * len(shape))
    tok_spec = pl.BlockSpec((TOP_K, t), lambda b, i: (0, b * nt + i))
    return pl.pallas_call(
        _router_kernel,
        grid=(bsz, nt),
        in_specs=[
            pl.BlockSpec((1, t, d), lambda b, i: (b, i, 0)),
            pl.BlockSpec((1, 6, d), lambda b, i: (b, 0, 0)),
            full((N_EXPERTS, d)), full((N_EXPERTS, 1)),
        ],
        out_specs=[tok_spec, tok_spec, tok_spec,
                   pl.BlockSpec((1, N_EXPERTS, LANES), lambda b, i: (b * nt + i, 0, 0))],
        out_shape=[
            jax.ShapeDtypeStruct((TOP_K, n_tok), jnp.int32),
            jax.ShapeDtypeStruct((TOP_K, n_tok), F32),
            jax.ShapeDtypeStruct((TOP_K, n_tok), jnp.int32),
            jax.ShapeDtypeStruct((bsz * nt, N_EXPERTS, LANES), F32),
        ],
        compiler_params=_cparams(("arbitrary", "arbitrary")),
        name="router",
    )(x, mod_l, router_w.T, router_b.reshape(N_EXPERTS, 1))


def _tile_index():
    return pl.program_id(0) * pl.num_programs(1) + pl.program_id(1)


def _for_each_chunk(tile, n_rows, chunks_per_tile, dst_ref, fn):
    def chunk(cc, carry):
        fn(pl.multiple_of(cc * SEG, SEG), pl.multiple_of(dst_ref[tile * chunks_per_tile + cc], SEG))
        return carry

    lax.fori_loop(0, n_rows // SEG, chunk, 0)


def _dispatch_kernel(dst_ref, used_ref, tstart_ref, tlen_ref,
                     x_ref, mod_ref, loc_ref, xs_ref, pm_buf, lbuf, zbuf, sem):
    tile = _tile_index()
    n_tiles = pl.num_programs(0) * pl.num_programs(1)
    t_len = x_ref.shape[1]
    n_loc = lbuf.shape[1]
    x = x_ref[0]
    shift, scale = mod_ref[0, 3:4, :], mod_ref[0, 4:5, :]
    u = (x * (1.0 + scale) + shift).astype(BF16)
    loc = loc_ref[...]
    for jc in range(n_loc // PERM_ROWS):
        j = jc * PERM_ROWS + lax.broadcasted_iota(jnp.int32, (PERM_ROWS, t_len), 0)
        pm = jnp.zeros((PERM_ROWS, t_len), F32)
        for kk in range(TOP_K):
            pm = jnp.where(j == loc[kk:kk + 1, :], 1.0, pm)
        pm_buf[jc * PERM_ROWS:(jc + 1) * PERM_ROWS, :] = pm.astype(BF16)
    slot = tile % 2
    lbuf[slot] = _pack_pairs(jnp.dot(pm_buf[...], u, preferred_element_type=F32))

    def seg_copy(s, src_row, dst_row):
        return pltpu.make_async_copy(lbuf.at[s, pl.ds(src_row, SEG)], xs_ref.at[pl.ds(dst_row, SEG)], sem)

    def drain(n_rows):
        def wait_one(cc, c2):
            seg_copy(0, 0, 0).wait()
            return c2

        lax.fori_loop(0, n_rows // SEG, wait_one, 0)

    @pl.when(tile > 0)
    def _():
        drain(used_ref[tile - 1])

    _for_each_chunk(tile, used_ref[tile], n_loc // SEG, dst_ref, lambda sr, dr: seg_copy(slot, sr, dr).start())

    @pl.when(tile == n_tiles - 1)
    def _():
        drain(used_ref[tile])
        zbuf[...] = jnp.zeros(zbuf.shape, U32)

        def fill(e, rows):
            def zero_copy(cc):
                dst = pl.multiple_of(tstart_ref[e] + cc * rows, SEG)
                return pltpu.make_async_copy(zbuf.at[pl.ds(0, rows)], xs_ref.at[pl.ds(dst, rows)], sem)

            n = tlen_ref[e] // rows

            def start(cc, c2):
                zero_copy(cc).start()
                return c2

            def wait(cc, c2):
                zero_copy(cc).wait()
                return c2

            lax.fori_loop(0, n, start, 0)
            lax.fori_loop(0, n, wait, 0)

        def per_expert(e, carry):
            fill(e, SEG)
            return carry

        lax.fori_loop(0, N_EXPERTS, per_expert, 0)
        fill(N_EXPERTS, BM)


def _dispatch(x, mod_l, loc, plan, n_slots):
    bsz, seq, d = x.shape
    t = min(T_MOE, seq)
    nt = seq // t
    n_loc = TOP_K * t + N_EXPERTS * SEG
    grid_spec = pltpu.PrefetchScalarGridSpec(
        num_scalar_prefetch=4,
        grid=(bsz, nt),
        in_specs=[
            pl.BlockSpec((1, t, d), lambda b, i, *_: (b, i, 0)),
            pl.BlockSpec((1, 6, d), lambda b, i, *_: (b, 0, 0)),
            pl.BlockSpec((TOP_K, t), lambda b, i, *_: (0, b * nt + i)),
        ],
        out_specs=pl.BlockSpec(memory_space=pl.ANY),
        scratch_shapes=[pltpu.VMEM((n_loc, t), BF16), pltpu.VMEM((2, n_loc, d // 2), U32),
                        pltpu.VMEM((BM, d // 2), U32), pltpu.SemaphoreType.DMA(())],
    )
    return pl.pallas_call(
        _dispatch_kernel,
        grid_spec=grid_spec,
        out_shape=jax.ShapeDtypeStruct((n_slots, d // 2), U32),
        compiler_params=_cparams(("arbitrary", "arbitrary")),
        name="dispatch",
    )(plan["dst"], plan["used"], plan["tstart"], plan["tlen"], x, mod_l, loc)


def _expert_kernel(bexp_ref, nused_ref, xs_ref, wgu_ref, bgu_ref, wdn_ref, bdn_ref, ys_ref, wgu_bf, wdn_bf):
    i = pl.program_id(0)
    active = i < nused_ref[0]
    new_expert = (i == 0) | (bexp_ref[i] != bexp_ref[jnp.maximum(i - 1, 0)])

    @pl.when(active & new_expert)
    def _():
        wgu_bf[...] = wgu_ref[0, 0].astype(BF16)
        wdn_bf[...] = wdn_ref[0, 0].astype(BF16)

    @pl.when(active)
    def _():
        x_lo, x_hi = _unpack_pairs(xs_ref[...])
        half = x_lo.shape[1]
        gu = (jnp.dot(x_lo, wgu_bf[0:half, :], preferred_element_type=F32)
              + jnp.dot(x_hi, wgu_bf[half:, :], preferred_element_type=F32) + bgu_ref[0, 0])
        gate = jnp.minimum(gu[:, :D_EXPERT], SWIGLU_LIMIT)
        up = jnp.clip(gu[:, D_EXPERT:], -SWIGLU_LIMIT, SWIGLU_LIMIT)
        glu = gate * _sigmoid(SWIGLU_ALPHA * gate)
        h = ((up + 1.0) * glu).astype(BF16)
        y = jnp.dot(h, wdn_bf[...], preferred_element_type=F32) + bdn_ref[0, 0]
        ys_ref[...] = _pack_pairs(y.astype(BF16).astype(F32))

    @pl.when(jnp.logical_not(active))
    def _():
        ys_ref[...] = jnp.zeros(ys_ref.shape, U32)


def _experts(xs, block_exp, n_used, layer, w_gu, b_gu, w_dn, b_dn):
    n_slots = xs.shape[0]
    d = w_dn.shape[-1]
    n_blocks = n_slots // BM
    depth = w_gu.shape[0]
    row_map = lambda i, be, nu: (jnp.maximum(jnp.minimum(i, nu[0] - 1), 0), 0)
    exp_map = lambda i, be, nu: (layer, be[i], 0, 0)
    grid_spec = pltpu.PrefetchScalarGridSpec(
        num_scalar_prefetch=2,
        grid=(n_blocks,),
        in_specs=[
            pl.BlockSpec((BM, d // 2), row_map),
            pl.BlockSpec((1, 1, d, 2 * D_EXPERT), exp_map),
            pl.BlockSpec((1, 1, 1, 2 * D_EXPERT), exp_map),
            pl.BlockSpec((1, 1, D_EXPERT, d), exp_map),
            pl.BlockSpec((1, 1, 1, d), exp_map),
        ],
        out_specs=pl.BlockSpec((BM, d // 2), lambda i, be, nu: (i, 0)),
        scratch_shapes=[pltpu.VMEM((d, 2 * D_EXPERT), BF16), pltpu.VMEM((D_EXPERT, d), BF16)],
    )
    return pl.pallas_call(
        _expert_kernel,
        grid_spec=grid_spec,
        out_shape=jax.ShapeDtypeStruct((n_slots, d // 2), U32),
        compiler_params=_cparams(("arbitrary",)),
        name="experts",
    )(block_exp, n_used, xs, w_gu, b_gu.reshape(depth, N_EXPERTS, 1, 2 * D_EXPERT), w_dn,
      b_dn.reshape(depth, N_EXPERTS, 1, d))


def _combine_kernel(dst_ref, used_ref, x_ref, mod_ref, loct_ref, gatet_ref, ys_ref,
                    lng_ref, lnb_ref, o_ref, ybuf, sel_buf, sem):
    tile = _tile_index()
    n_tiles = pl.num_programs(0) * pl.num_programs(1)
    t_len = x_ref.shape[1]
    n_loc = ybuf.shape[1]
    slot = tile % 2

    def seg_copy(s, src_row, dst_row):
        return pltpu.make_async_copy(ys_ref.at[pl.ds(dst_row, SEG)], ybuf.at[s, pl.ds(src_row, SEG)], sem.at[s])

    def fetch(t, s):
        used_t = used_ref[t]
        _for_each_chunk(t, used_t, n_loc // SEG, dst_ref, lambda sr, dr: seg_copy(s, sr, dr).start())

        def zero_rows(cc, c2):
            ybuf[s, pl.ds(pl.multiple_of(used_t + cc * SEG, SEG), SEG), :] = jnp.zeros((SEG, ybuf.shape[2]), U32)
            return c2

        lax.fori_loop(0, (n_loc - used_t) // SEG, zero_rows, 0)

    @pl.when(tile == 0)
    def _():
        fetch(0, 0)

    @pl.when(tile + 1 < n_tiles)
    def _():
        fetch(tile + 1, 1 - slot)

    loct = loct_ref[...]
    gatet = gatet_ref[...]
    for jc in range(n_loc // LANES):
        j = jc * LANES + lax.broadcasted_iota(jnp.int32, (t_len, LANES), 1)
        sel = jnp.zeros((t_len, LANES), F32)
        for kk in range(TOP_K):
            sel = jnp.where(j == loct[:, kk:kk + 1], gatet[:, kk:kk + 1], sel)
        sel_buf[:, jc * LANES:(jc + 1) * LANES] = sel.astype(BF16)

    def drain(cc, c2):
        seg_copy(slot, 0, 0).wait()
        return c2

    lax.fori_loop(0, used_ref[tile] // SEG, drain, 0)

    y_lo, y_hi = _unpack_pairs(ybuf[slot])
    sel = sel_buf[...]
    y = jnp.concatenate([jnp.dot(sel, y_lo, preferred_element_type=F32),
                         jnp.dot(sel, y_hi, preferred_element_type=F32)], axis=1)
    x = x_ref[0]
    gate_f = mod_ref[0, 5:6, :]
    r = DEEPNORM_ALPHA * x + (1.0 + gate_f) * y
    o_ref[0] = _layer_norm(r, lng_ref[...], lnb_ref[...])


def _combine(x, mod_l, loc_t, gates_t, plan, ys, ln_g, ln_b):
    bsz, seq, d = x.shape
    t = min(T_MOE, seq)
    nt = seq // t
    n_loc = TOP_K * t + N_EXPERTS * SEG
    full = lambda shape: pl.BlockSpec(shape, lambda b, i, *_: (0,) * len(shape))
    grid_spec = pltpu.PrefetchScalarGridSpec(
        num_scalar_prefetch=2,
        grid=(bsz, nt),
        in_specs=[
            pl.BlockSpec((1, t, d), lambda b, i, *_: (b, i, 0)),
            pl.BlockSpec((1, 6, d), lambda b, i, *_: (b, 0, 0)),
            pl.BlockSpec((t, TOP_K), lambda b, i, *_: (b * nt + i, 0)),
            pl.BlockSpec((t, TOP_K), lambda b, i, *_: (b * nt + i, 0)),
            pl.BlockSpec(memory_space=pl.ANY),
            full((1, d)), full((1, d)),
        ],
        out_specs=pl.BlockSpec((1, t, d), lambda b, i, *_: (b, i, 0)),
        scratch_shapes=[pltpu.VMEM((2, n_loc, d // 2), U32), pltpu.VMEM((t, n_loc), BF16),
                        pltpu.SemaphoreType.DMA((2,))],
    )
    return pl.pallas_call(
        _combine_kernel,
        grid_spec=grid_spec,
        out_shape=jax.ShapeDtypeStruct((bsz, seq, d), F32),
        compiler_params=_cparams(("arbitrary", "arbitrary")),
        name="combine",
    )(plan["dst"], plan["used"], x, mod_l, loc_t, gates_t, ys,
      ln_g.reshape(1, d), ln_b.reshape(1, d))


def _slots_kernel(loc0_ref, topi_ref, lrank_ref, loc_ref):
    tile = pl.program_id(0)
    topi = topi_ref[...]
    loc = lrank_ref[...]
    for e in range(N_EXPERTS):
        loc = loc + jnp.where(topi == e, loc0_ref[tile * N_EXPERTS + e], 0)
    loc_ref[...] = loc


def _slots(loc0, topi, lrank, t):
    k, n = topi.shape
    spec = pl.BlockSpec((k, t), lambda i, l0: (0, i))
    return pl.pallas_call(
        _slots_kernel,
        grid_spec=pltpu.PrefetchScalarGridSpec(num_scalar_prefetch=1, grid=(n // t,), in_specs=[spec, spec],
                                               out_specs=spec),
        out_shape=jax.ShapeDtypeStruct((k, n), jnp.int32),
        compiler_params=_cparams(("arbitrary",)),
        name="slots",
    )(loc0, topi, lrank)


def _moe_plan(cnt, n_slots, n_loc):
    c = cnt[:, :, 0].astype(jnp.int32)
    p = (c + SEG - 1) // SEG * SEG
    tot = jnp.sum(p, axis=0)
    padded = (tot + BM - 1) // BM * BM
    pad_end = jnp.cumsum(padded)
    pad_start = pad_end - padded
    seg = pad_start[None, :] + jnp.cumsum(p, axis=0) - p
    loc0 = jnp.cumsum(p, axis=1) - p
    n_blocks = n_slots // BM
    block_lo = jnp.arange(n_blocks, dtype=jnp.int32) * BM
    block_exp = jnp.minimum(jnp.sum((pad_end[None, :] <= block_lo[:, None]).astype(jnp.int32), axis=1),
                            N_EXPERTS - 1)
    chunk_lo = jnp.arange(n_loc // SEG, dtype=jnp.int32) * SEG
    chunk_exp = jnp.sum(((loc0 + p)[:, :, None] <= chunk_lo[None, None, :]).astype(jnp.int32), axis=1)
    owner = chunk_exp[:, None, :] == jnp.arange(N_EXPERTS, dtype=jnp.int32)[None, :, None]
    dst = jnp.sum(jnp.where(owner, (seg - loc0)[:, :, None], 0), axis=1) + chunk_lo[None, :]
    return {
        "dst": dst.reshape(-1), "loc0": loc0.reshape(-1), "used": jnp.sum(p, axis=1),
        "tstart": jnp.concatenate([pad_start + tot, pad_end[-1:]]),
        "tlen": jnp.concatenate([padded - tot, n_slots - pad_end[-1:]]),
        "n_used": pad_end[-1:] // BM, "block_exp": block_exp,
    }


def _moe_layer(x, mod_l, layer, router_w, router_b, w_gu, b_gu, w_dn, b_dn, ln_g, ln_b):
    bsz, seq, d = x.shape
    n_tok = bsz * seq
    t = min(T_MOE, seq)
    n_slots = n_tok * TOP_K + (n_tok // t) * N_EXPERTS * SEG + N_EXPERTS * BM
    topi, gates, lrank, cnt = _router(x, mod_l, router_w, router_b)
    plan = _moe_plan(cnt, n_slots, TOP_K * t + N_EXPERTS * SEG)
    loc = _slots(plan["loc0"], topi, lrank, t)
    xs = _dispatch(x, mod_l, loc, plan, n_slots)
    ys = _experts(xs, plan["block_exp"], plan["n_used"], layer, w_gu, b_gu, w_dn, b_dn)
    return _combine(x, mod_l, loc.T, gates.T, plan, ys, ln_g, ln_b)


def kernel(x, c, positions, ada_w, ada_b, ln_mix_g, ln_mix_b, ln_ffn_g, ln_ffn_b, ev_w_in, ev_conv_w, ev_sg_w, ev_sg_b, ev_vn_g, ev_vn_b, ev_w_out, od_w_in, od_dw_w, od_dw_b, od_cn_g, od_cn_b, od_qn_g, od_w_uq, od_kvn_g, od_w_ukv, od_w_out, moe_router_w, moe_router_b, moe_w_gu, moe_b_gu, moe_w_dn, moe_b_dn):
    bsz, seq, d = x.shape
    depth = ada_w.shape[0]
    mod = _ada(c, ada_w, ada_b).reshape(depth, bsz, 6, d)
    cos_t, sin_t = _rope_tables(positions)
    for layer in range(depth):
        i = layer // 2
        mod_l = mod[layer]
        if layer % 2 == 0:
            x = _even_layer(x, mod_l, ev_w_in[i], ev_conv_w[i], ev_sg_w[i], ev_sg_b[i], ev_vn_g[i], ev_vn_b[i],
                            ev_w_out[i], ln_mix_g[layer], ln_mix_b[layer])
        else:
            y_c, q, k, v = _odd_pre(x, mod_l, cos_t, sin_t, od_w_in[i], od_dw_w[i], od_dw_b[i], od_cn_g[i],
                                    od_cn_b[i], od_qn_g[i], od_w_uq[i], od_kvn_g[i], od_w_ukv[i])
            y_d = _attention(q, k, v)
            x = _odd_post(x, mod_l, y_c, y_d, od_w_out[i], ln_mix_g[layer], ln_mix_b[layer])
        x = _moe_layer(x, mod_l, layer, moe_router_w[layer], moe_router_b[layer], moe_w_gu, moe_b_gu,
                       moe_w_dn, moe_b_dn, ln_ffn_g[layer], ln_ffn_b[layer])
    return x
```

```python
import functools
import math

import jax
import jax.numpy as jnp
from jax import lax
from jax.experimental import pallas as pl
from jax.experimental.pallas import tpu as pltpu

F32 = jnp.float32
BF16 = jnp.bfloat16
U32 = jnp.uint32
HIGHEST = lax.Precision.HIGHEST

D_MODEL = 1024
DEPTH = 4
W_A = 512
W_B = 512
SG_HEADS = 8
SG_HEAD_DIM = 64
CHUNK = 128
CONV_A = 3
W_C = 512
CONV_C = 31
MLA_HEADS = 8
QK_NOPE = 64
QK_ROPE = 32
V_DIM = 64
Q_RANK = 256
KV_RANK = 128
ROPE_THETA = 10000.0
N_EXPERTS = 32
TOP_K = 4
D_EXPERT = 1024
SWIGLU_LIMIT = 7.0
SWIGLU_ALPHA = 1.702
DEEPNORM_ALPHA = (2.0 * DEPTH) ** 0.25
LN_EPS = 1e-5
RMS_EPS = 1e-6

LANES = 128
SUBLANES = 8
HEAD_PAD = 128
CONV_C_HALO = 32
CONV_A_HALO = 8
CONV_ROWS = 128

T_EVEN = 512
T_ODD = 256
T_POST = 512
T_ATT = 512
ATT_HEADS = 8
T_MOE = 512
BM = 512
SEG = 8
BIG = 32
PERM_ROWS = 256
VMEM_LIMIT = 56 * 1024 * 1024


def _cparams(sem):
    return pltpu.CompilerParams(dimension_semantics=sem, vmem_limit_bytes=VMEM_LIMIT)


def _layer_norm(v, g, b):
    mu = jnp.mean(v, axis=-1, keepdims=True)
    d = v - mu
    var = jnp.mean(d * d, axis=-1, keepdims=True)
    return d * lax.rsqrt(var + LN_EPS) * g + b


def _gelu(v):
    return 0.5 * v * (1.0 + lax.erf(v * (1.0 / math.sqrt(2.0))))


def _sigmoid(v):
    return 1.0 / (1.0 + jnp.exp(-v))


HIGH_HALF = 0xFFFF0000


def _pack_pairs(v):
    n = v.shape[1] // 2
    bits = lax.bitcast_convert_type(v, U32)
    return (bits[:, :n] >> 16) | (bits[:, n:] & jnp.uint32(HIGH_HALF))


def _unpack_pairs(w):
    lo = lax.bitcast_convert_type(w << 16, F32).astype(BF16)
    hi = lax.bitcast_convert_type(w & jnp.uint32(HIGH_HALF), F32).astype(BF16)
    return lo, hi


def _ada_kernel(c_ref, w_ref, b_ref, o_ref):
    c = c_ref[...]
    cond = c * _sigmoid(c)
    o_ref[0] = jnp.dot(cond, w_ref[0], precision=HIGHEST, preferred_element_type=F32) + b_ref[0]


def _ada(c, ada_w, ada_b):
    bsz, d = c.shape
    depth = ada_w.shape[0]
    n_chunk = ada_w.shape[2] // d
    return pl.pallas_call(
        _ada_kernel,
        grid=(depth, n_chunk),
        in_specs=[
            pl.BlockSpec((bsz, d), lambda l, j: (0, 0)),
            pl.BlockSpec((1, d, d), lambda l, j: (l, 0, j)),
            pl.BlockSpec((1, 1, d), lambda l, j: (l, 0, j)),
        ],
        out_specs=pl.BlockSpec((1, bsz, d), lambda l, j: (l, 0, j)),
        out_shape=jax.ShapeDtypeStruct((depth, bsz, n_chunk * d), F32),
        compiler_params=_cparams(("arbitrary", "arbitrary")),
        name="ada",
    )(c, ada_w, ada_b.reshape(depth, 1, n_chunk * d))


def _rope_kernel(pos_ref, freq_ref, cos_ref, sin_ref):
    ang = pos_ref[...].astype(F32) * freq_ref[...]
    cos_ref[...] = jnp.cos(ang)
    sin_ref[...] = jnp.sin(ang)


def _rope_tables(positions):
    n = positions.size
    t = 1024
    inv_freq = ROPE_THETA ** (-jnp.arange(0, QK_ROPE, 2, dtype=F32) / QK_ROPE)
    half = QK_ROPE // 2
    freq = jnp.zeros((1, LANES), F32).at[0, QK_NOPE:QK_NOPE + QK_ROPE].set(jnp.tile(inv_freq, 2))
    del half
    return pl.pallas_call(
        _rope_kernel,
        grid=(n // t,),
        in_specs=[pl.BlockSpec((t, 1), lambda i: (i, 0)), pl.BlockSpec((1, LANES), lambda i: (0, 0))],
        out_specs=[pl.BlockSpec((t, LANES), lambda i: (i, 0))] * 2,
        out_shape=[jax.ShapeDtypeStruct((n, LANES), F32)] * 2,
        compiler_params=_cparams(("arbitrary",)),
        name="rope",
    )(positions.reshape(n, 1), freq)


def _even_kernel(x_ref, mod_ref, win_ref, cw_ref, sgw_ref, sgb_ref, vng_ref, vnb_ref, wout_ref,
                 lng_ref, lnb_ref, o_ref, gbuf, ybuf):
    t_len = x_ref.shape[1]
    halo = CONV_A_HALO

    @pl.when(pl.program_id(1) == 0)
    def _():
        gbuf[0:halo, :] = jnp.zeros((halo, W_A), F32)

    x = x_ref[0]
    shift, scale, gate = mod_ref[0, 0:1, :], mod_ref[0, 1:2, :], mod_ref[0, 2:3, :]
    u = x * (1.0 + scale) + shift
    proj = jnp.dot(u.astype(BF16), win_ref[...], preferred_element_type=F32)
    b_gate = proj[:, 0:W_A]
    c_gate = proj[:, W_A:2 * W_A]
    xa = proj[:, 2 * W_A:3 * W_A]
    zu = proj[:, 3 * W_A:3 * W_A + W_B]
    zv = proj[:, 3 * W_A + W_B:3 * W_A + 2 * W_B]

    g = c_gate * xa
    gbuf[halo:halo + t_len, :] = g
    conv = (cw_ref[0:1, :] * gbuf[halo - 2:halo - 2 + t_len, :]
            + cw_ref[1:2, :] * gbuf[halo - 1:halo - 1 + t_len, :]
            + cw_ref[2:3, :] * g)
    gbuf[0:halo, :] = g[t_len - halo:t_len, :]
    ybuf[:, 0:W_A] = (b_gate * conv).astype(BF16)

    zu = _gelu(zu)
    zv = _layer_norm(_gelu(zv), vng_ref[...], vnb_ref[...]).astype(BF16)
    row = lax.broadcasted_iota(jnp.int32, (CHUNK, CHUNK), 0)
    col = lax.broadcasted_iota(jnp.int32, (CHUNK, CHUNK), 1)
    w_stack = jnp.concatenate(
        [jnp.where(row >= col, sgw_ref[h], 0.0).astype(BF16) for h in range(SG_HEADS)], axis=0)
    col_head = lax.broadcasted_iota(jnp.int32, (CHUNK, W_B), 1) // SG_HEAD_DIM
    for ci in range(t_len // CHUNK):
        lo = ci * CHUNK
        full = jnp.dot(w_stack, zv[lo:lo + CHUNK, :], preferred_element_type=F32)
        mixed = sgb_ref[...]
        for h in range(SG_HEADS):
            mixed = mixed + jnp.where(col_head == h, full[h * CHUNK:(h + 1) * CHUNK, :], 0.0)
        ybuf[lo:lo + CHUNK, W_A:W_A + W_B] = (zu[lo:lo + CHUNK, :] * mixed).astype(BF16)

    y = jnp.dot(ybuf[...], wout_ref[...], preferred_element_type=F32)
    r = DEEPNORM_ALPHA * x + (1.0 + gate) * y
    o_ref[0] = _layer_norm(r, lng_ref[...], lnb_ref[...])


def _even_layer(x, mod_l, w_in, conv_w, sg_w, sg_b, vn_g, vn_b, w_out, ln_g, ln_b):
    bsz, seq, d = x.shape
    t = min(T_EVEN, seq)
    sgb_full = jnp.repeat(sg_b.T, SG_HEAD_DIM, axis=1)
    full = lambda shape: pl.BlockSpec(shape, lambda b, i: (0,) * len(shape))
    return pl.pallas_call(
        _even_kernel,
        grid=(bsz, seq // t),
        in_specs=[
            pl.BlockSpec((1, t, d), lambda b, i: (b, i, 0)),
            pl.BlockSpec((1, 6, d), lambda b, i: (b, 0, 0)),
            full(w_in.shape), full(conv_w.shape), full(sg_w.shape), full(sgb_full.shape),
            full((1, W_B)), full((1, W_B)), full(w_out.shape), full((1, d)), full((1, d)),
        ],
        out_specs=pl.BlockSpec((1, t, d), lambda b, i: (b, i, 0)),
        out_shape=jax.ShapeDtypeStruct((bsz, seq, d), F32),
        scratch_shapes=[pltpu.VMEM((CONV_A_HALO + t, W_A), F32), pltpu.VMEM((t, W_A + W_B), BF16)],
        compiler_params=_cparams(("arbitrary", "arbitrary")),
        name="even",
    )(x, mod_l, w_in.astype(BF16), conv_w, sg_w, sgb_full, vn_g.reshape(1, W_B), vn_b.reshape(1, W_B),
      w_out.astype(BF16), ln_g.reshape(1, d), ln_b.reshape(1, d))


N_GLU = 2 * W_C
OFF_Q = N_GLU
OFF_KV = OFF_Q + Q_RANK
OFF_KR = OFF_KV + KV_RANK
OFF_KR_ROT = OFF_KR + HEAD_PAD
ODD_IN_PAD = OFF_KR_ROT + HEAD_PAD


def _odd_pre_kernel(x_ref, mod_ref, cos_ref, sin_ref, win_ref, dww_ref, dwb_ref, cng_ref, cnb_ref,
                    qng_ref, wq_ref, wqr_ref, kvng_ref, wk_ref, wv_ref,
                    yc_ref, q_ref, k_ref, v_ref, gbuf, sbuf, hbuf):
    t_len = x_ref.shape[1]
    halo = CONV_C_HALO

    @pl.when(pl.program_id(1) == 0)
    def _():
        gbuf[0:halo, :] = jnp.zeros((halo, W_C), F32)

    x = x_ref[0]
    shift, scale = mod_ref[0, 0:1, :], mod_ref[0, 1:2, :]
    u = x * (1.0 + scale) + shift
    proj = jnp.dot(u.astype(BF16), win_ref[...], preferred_element_type=F32)

    g = proj[:, 0:W_C] * _sigmoid(proj[:, W_C:2 * W_C])
    gbuf[halo:halo + t_len, :] = g
    first = halo - (CONV_C - 1)
    for res in range(1, SUBLANES):
        span = max(first + k - res for k in range(CONV_C) if (first + k) % SUBLANES == res) + t_len
        sbuf[res - 1, 0:span, :] = gbuf[res:res + span, :]
    for rb in range(t_len // CONV_ROWS):
        for cb in range(W_C // LANES):
            cs = slice(cb * LANES, (cb + 1) * LANES)
            acc = jnp.broadcast_to(dwb_ref[:, cs], (CONV_ROWS, LANES))
            for k in range(CONV_C):
                res = (first + k) % SUBLANES
                lo = first + k - res + rb * CONV_ROWS
                tap = gbuf[lo:lo + CONV_ROWS, cs] if res == 0 else sbuf[res - 1, lo:lo + CONV_ROWS, cs]
                acc = acc + dww_ref[k:k + 1, cs] * tap
            hbuf[rb * CONV_ROWS:(rb + 1) * CONV_ROWS, cs] = acc
    acc = hbuf[...]
    gbuf[0:halo, :] = g[t_len - halo:t_len, :]
    hn = _layer_norm(acc, cng_ref[...], cnb_ref[...])
    yc_ref[0] = (hn * _sigmoid(hn)).astype(BF16)

    cos = cos_ref[...]
    sin = sin_ref[...]
    cq = proj[:, OFF_Q:OFF_Q + Q_RANK]
    q_lat = (cq * lax.rsqrt(jnp.mean(cq * cq, axis=-1, keepdims=True) + RMS_EPS) * qng_ref[...]).astype(BF16)
    ckv = proj[:, OFF_KV:OFF_KV + KV_RANK]
    kv_lat = (ckv * lax.rsqrt(jnp.mean(ckv * ckv, axis=-1, keepdims=True) + RMS_EPS) * kvng_ref[...]).astype(BF16)
    q_all = jnp.dot(q_lat, wq_ref[...], preferred_element_type=F32)
    q_rot = jnp.dot(q_lat, wqr_ref[...], preferred_element_type=F32)
    k_all = jnp.dot(kv_lat, wk_ref[...], preferred_element_type=F32)
    v_all = jnp.dot(kv_lat, wv_ref[...], preferred_element_type=F32)
    k_rope = proj[:, OFF_KR:OFF_KR + HEAD_PAD] * cos + proj[:, OFF_KR_ROT:OFF_KR_ROT + HEAD_PAD] * sin
    sm_scale = math.log2(math.e) / math.sqrt(QK_NOPE + QK_ROPE)
    for h in range(MLA_HEADS):
        sl = slice(h * HEAD_PAD, (h + 1) * HEAD_PAD)
        q_ref[0, h] = ((q_all[:, sl] * cos + q_rot[:, sl] * sin) * sm_scale).astype(BF16)
        k_ref[0, h] = (k_all[:, sl] + k_rope).astype(BF16)
    for hp in range(MLA_HEADS // 2):
        v_ref[0, hp] = v_all[:, hp * LANES:(hp + 1) * LANES].astype(BF16)


def _odd_weights(w_in, w_uq, w_ukv):
    d = w_in.shape[0]
    half = QK_ROPE // 2
    kr = w_in[:, OFF_KR:OFF_KR + QK_ROPE]
    z = lambda n: jnp.zeros((d, n), w_in.dtype)
    kr_blk = jnp.concatenate([z(QK_NOPE), kr, z(HEAD_PAD - QK_NOPE - QK_ROPE)], axis=1)
    kr_rot = jnp.concatenate([z(QK_NOPE), -kr[:, half:], kr[:, :half], z(HEAD_PAD - QK_NOPE - QK_ROPE)], axis=1)
    w_in_p = jnp.concatenate([w_in[:, :OFF_KR], kr_blk, kr_rot], axis=1).astype(BF16)

    dq = QK_NOPE + QK_ROPE
    wq = w_uq.reshape(Q_RANK, MLA_HEADS, dq)
    zq = lambda n: jnp.zeros((Q_RANK, MLA_HEADS, n), w_uq.dtype)
    wq_p = jnp.concatenate([wq, zq(HEAD_PAD - dq)], axis=2).reshape(Q_RANK, MLA_HEADS * HEAD_PAD)
    wq_r = jnp.concatenate([zq(QK_NOPE), -wq[:, :, QK_NOPE + half:], wq[:, :, QK_NOPE:QK_NOPE + half],
                            zq(HEAD_PAD - dq)], axis=2).reshape(Q_RANK, MLA_HEADS * HEAD_PAD)

    wkv = w_ukv.reshape(KV_RANK, MLA_HEADS, QK_NOPE + V_DIM)
    wk_p = jnp.concatenate([wkv[:, :, :QK_NOPE], jnp.zeros((KV_RANK, MLA_HEADS, HEAD_PAD - QK_NOPE), w_ukv.dtype)],
                           axis=2).reshape(KV_RANK, MLA_HEADS * HEAD_PAD)
    wv_p = wkv[:, :, QK_NOPE:].reshape(KV_RANK, MLA_HEADS * V_DIM)
    return w_in_p, wq_p.astype(BF16), wq_r.astype(BF16), wk_p.astype(BF16), wv_p.astype(BF16)


def _odd_pre(x, mod_l, cos_t, sin_t, w_in, dw_w, dw_b, cn_g, cn_b, qn_g, w_uq, kvn_g, w_ukv):
    bsz, seq, d = x.shape
    t = min(T_ODD, seq)
    nt = seq // t
    w_in_p, wq_p, wq_r, wk_p, wv_p = _odd_weights(w_in, w_uq, w_ukv)
    full = lambda shape: pl.BlockSpec(shape, lambda b, i: (0,) * len(shape))
    return pl.pallas_call(
        _odd_pre_kernel,
        grid=(bsz, nt),
        in_specs=[
            pl.BlockSpec((1, t, d), lambda b, i: (b, i, 0)),
            pl.BlockSpec((1, 6, d), lambda b, i: (b, 0, 0)),
            pl.BlockSpec((t, LANES), lambda b, i: (b * nt + i, 0)),
            pl.BlockSpec((t, LANES), lambda b, i: (b * nt + i, 0)),
            full(w_in_p.shape), full(dw_w.shape), full((1, W_C)), full((1, W_C)), full((1, W_C)),
            full((1, Q_RANK)), full(wq_p.shape), full(wq_r.shape), full((1, KV_RANK)),
            full(wk_p.shape), full(wv_p.shape),
        ],
        out_specs=[
            pl.BlockSpec((1, t, W_C), lambda b, i: (b, i, 0)),
            pl.BlockSpec((1, MLA_HEADS, t, HEAD_PAD), lambda b, i: (b, 0, i, 0)),
            pl.BlockSpec((1, MLA_HEADS, t, HEAD_PAD), lambda b, i: (b, 0, i, 0)),
            pl.BlockSpec((1, MLA_HEADS // 2, t, LANES), lambda b, i: (b, 0, i, 0)),
        ],
        out_shape=[
            jax.ShapeDtypeStruct((bsz, seq, W_C), BF16),
            jax.ShapeDtypeStruct((bsz, MLA_HEADS, seq, HEAD_PAD), BF16),
            jax.ShapeDtypeStruct((bsz, MLA_HEADS, seq, HEAD_PAD), BF16),
            jax.ShapeDtypeStruct((bsz, MLA_HEADS // 2, seq, LANES), BF16),
        ],
        scratch_shapes=[pltpu.VMEM((CONV_C_HALO + t, W_C), F32),
                        pltpu.VMEM((SUBLANES - 1, CONV_C_HALO + t, W_C), F32), pltpu.VMEM((t, W_C), F32)],
        compiler_params=_cparams(("arbitrary", "arbitrary")),
        name="odd_pre",
    )(x, mod_l, cos_t, sin_t, w_in_p, dw_w, dw_b.reshape(1, W_C), cn_g.reshape(1, W_C), cn_b.reshape(1, W_C),
      qn_g.reshape(1, Q_RANK), wq_p, wq_r, kvn_g.reshape(1, KV_RANK), wk_p, wv_p)


ATT_FULL, ATT_DIAG_LATE, ATT_DIAG_EARLY = 0, 1, 2


def _attn_kernel(kj_tab, kind_tab, q_ref, k_ref, v_ref, o_ref, m_sc, l_sc, acc_sc):
    step_id = pl.program_id(2)
    kind = kind_tab[step_id]
    tq = q_ref.shape[2]
    tk = k_ref.shape[2]

    @pl.when(kj_tab[step_id] == 0)
    def _():
        m_sc[...] = jnp.full(m_sc.shape, -jnp.inf, F32)
        l_sc[...] = jnp.zeros(l_sc.shape, F32)
        acc_sc[...] = jnp.zeros(acc_sc.shape, F32)

    first_lanes = lax.broadcasted_iota(jnp.int32, (tq, LANES), 1) < V_DIM
    n_pairs = v_ref.shape[1]

    def step(n_keys, row_shift):
        n_col = n_keys // LANES
        for pp in range(n_pairs):
            v_pair = v_ref[0, pp, 0:n_keys, :]
            alphas = []
            pvs = []
            for a in range(2 * pp, 2 * pp + 2):
                s = lax.dot_general(q_ref[0, a], k_ref[0, a, 0:n_keys, :], (((1,), (1,)), ((), ())),
                                    preferred_element_type=F32)
                if row_shift is not None:
                    rows = row_shift + lax.broadcasted_iota(jnp.int32, (tq, n_keys), 0)
                    cols = lax.broadcasted_iota(jnp.int32, (tq, n_keys), 1)
                    s = jnp.where(cols <= rows, s, -1e30)
                cols_s = [s[:, c * LANES:(c + 1) * LANES] for c in range(n_col)]
                m_lane = cols_s[0]
                for c in range(1, n_col):
                    m_lane = jnp.maximum(m_lane, cols_s[c])
                m_prev = m_sc[a]
                m_next = jnp.maximum(m_prev, jnp.max(m_lane, axis=1, keepdims=True))
                alpha = jnp.exp2(m_prev - m_next)
                ps = [jnp.exp2(cs - m_next) for cs in cols_s]
                l_lane = ps[0]
                for c in range(1, n_col):
                    l_lane = l_lane + ps[c]
                l_sc[a] = alpha * l_sc[a] + l_lane
                m_sc[a] = m_next
                alphas.append(alpha)
                p = jnp.concatenate([pc.astype(BF16) for pc in ps], axis=1)
                pvs.append(jnp.dot(p, v_pair, preferred_element_type=F32))
            alpha_sel = jnp.where(first_lanes, alphas[0], alphas[1])
            acc_sc[pp] = alpha_sel * acc_sc[pp] + jnp.where(first_lanes, pvs[0], pvs[1])

    @pl.when(kind == ATT_FULL)
    def _():
        step(tk, None)

    @pl.when(kind == ATT_DIAG_LATE)
    def _():
        step(tk, tq)

    @pl.when(kind == ATT_DIAG_EARLY)
    def _():
        step(tq, 0)

    @pl.when(kind != ATT_FULL)
    def _():
        for pp in range(n_pairs):
            l_sel = jnp.where(first_lanes, jnp.sum(l_sc[2 * pp], axis=1, keepdims=True),
                              jnp.sum(l_sc[2 * pp + 1], axis=1, keepdims=True))
            o_ref[0, :, pp * LANES:(pp + 1) * LANES] = (acc_sc[pp] / l_sel).astype(o_ref.dtype)


def _attention(q, k, v):
    bsz, heads, seq, _ = q.shape
    t = min(T_ATT, seq // 2)
    nq = seq // t
    steps = [(i, j, ATT_FULL if j < i // 2 else (ATT_DIAG_LATE if i % 2 else ATT_DIAG_EARLY))
             for i in range(nq) for j in range(i // 2 + 1)]
    qi_tab = jnp.asarray([st[0] for st in steps], jnp.int32)
    kj_tab = jnp.asarray([st[1] for st in steps], jnp.int32)
    kind_tab = jnp.asarray([st[2] for st in steps], jnp.int32)
    hs = ATT_HEADS
    grid_spec = pltpu.PrefetchScalarGridSpec(
        num_scalar_prefetch=3,
        grid=(bsz, heads // hs, len(steps)),
        in_specs=[
            pl.BlockSpec((1, hs, t, HEAD_PAD), lambda b, hg, s, qt, kt, kd: (b, hg, qt[s], 0)),
            pl.BlockSpec((1, hs, 2 * t, HEAD_PAD), lambda b, hg, s, qt, kt, kd: (b, hg, kt[s], 0)),
            pl.BlockSpec((1, hs // 2, 2 * t, LANES), lambda b, hg, s, qt, kt, kd: (b, hg, kt[s], 0)),
        ],
        out_specs=pl.BlockSpec((1, t, hs * V_DIM), lambda b, hg, s, qt, kt, kd: (b, qt[s], hg)),
        scratch_shapes=[pltpu.VMEM((hs, t, LANES), F32), pltpu.VMEM((hs, t, LANES), F32),
                        pltpu.VMEM((hs // 2, t, LANES), F32)],
    )
    return pl.pallas_call(
        _attn_kernel_entry,
        grid_spec=grid_spec,
        out_shape=jax.ShapeDtypeStruct((bsz, seq, heads * V_DIM), BF16),
        compiler_params=_cparams(("arbitrary", "arbitrary", "arbitrary")),
        name="attn",
    )(qi_tab, kj_tab, kind_tab, q, k, v)


def _attn_kernel_entry(qi_tab, kj_tab, kind_tab, *refs):
    del qi_tab
    _attn_kernel(kj_tab, kind_tab, *refs)


def _odd_post_kernel(x_ref, mod_ref, yc_ref, yd_ref, wout_ref, lng_ref, lnb_ref, o_ref):
    x = x_ref[0]
    gate = mod_ref[0, 2:3, :]
    y = (jnp.dot(yc_ref[0], wout_ref[0:W_C, :], preferred_element_type=F32)
         + jnp.dot(yd_ref[0], wout_ref[W_C:, :], preferred_element_type=F32))
    r = DEEPNORM_ALPHA * x + (1.0 + gate) * y
    o_ref[0] = _layer_norm(r, lng_ref[...], lnb_ref[...])


def _odd_post(x, mod_l, y_c, y_d, w_out, ln_g, ln_b):
    bsz, seq, d = x.shape
    t = min(T_POST, seq)
    full = lambda shape: pl.BlockSpec(shape, lambda b, i: (0,) * len(shape))
    return pl.pallas_call(
        _odd_post_kernel,
        grid=(bsz, seq // t),
        in_specs=[
            pl.BlockSpec((1, t, d), lambda b, i: (b, i, 0)),
            pl.BlockSpec((1, 6, d), lambda b, i: (b, 0, 0)),
            pl.BlockSpec((1, t, W_C), lambda b, i: (b, i, 0)),
            pl.BlockSpec((1, t, MLA_HEADS * V_DIM), lambda b, i: (b, i, 0)),
            full(w_out.shape), full((1, d)), full((1, d)),
        ],
        out_specs=pl.BlockSpec((1, t, d), lambda b, i: (b, i, 0)),
        out_shape=jax.ShapeDtypeStruct((bsz, seq, d), F32),
        compiler_params=_cparams(("arbitrary", "arbitrary")),
        name="odd_post",
    )(x, mod_l, y_c, y_d, w_out.astype(BF16), ln_g.reshape(1, d), ln_b.reshape(1, d))


def _router_kernel(x_ref, mod_ref, rw_ref, rb_ref, topi_ref, gate_ref, lrank_ref, cnt_ref):
    t_len = x_ref.shape[1]
    x = x_ref[0]
    shift, scale = mod_ref[0, 3:4, :], mod_ref[0, 4:5, :]
    u = x * (1.0 + scale) + shift
    logits = lax.dot_general(rw_ref[...], u, (((1,), (1,)), ((), ())), precision=HIGHEST,
                             preferred_element_type=F32) + rb_ref[...]
    eidx = lax.broadcasted_iota(jnp.int32, (N_EXPERTS, t_len), 0)
    vals, hots = [], []
    for _ in range(TOP_K):
        m = jnp.max(logits, axis=0, keepdims=True)
        idx = jnp.min(jnp.where(logits == m, eidx, N_EXPERTS), axis=0, keepdims=True)
        hot = eidx == idx
        vals.append(m)
        hots.append(hot)
        logits = jnp.where(hot, -jnp.inf, logits)
        topi_ref[len(vals) - 1:len(vals), :] = idx
    exps = [jnp.exp(v - vals[0]) for v in vals]
    denom = exps[0] + exps[1] + exps[2] + exps[3]
    for kk in range(TOP_K):
        gate_ref[kk:kk + 1, :] = exps[kk] / denom

    chosen = jnp.where(hots[0] | hots[1] | hots[2] | hots[3], 1.0, 0.0)
    r_i = lax.broadcasted_iota(jnp.int32, (t_len, t_len), 0)
    c_i = lax.broadcasted_iota(jnp.int32, (t_len, t_len), 1)
    before = jnp.where(r_i < c_i, 1.0, 0.0).astype(BF16)
    base = jnp.dot(chosen.astype(BF16), before, preferred_element_type=F32)
    for kk in range(TOP_K):
        lrank_ref[kk:kk + 1, :] = jnp.sum(jnp.where(hots[kk], base, 0.0), axis=0, keepdims=True).astype(jnp.int32)
    cnt_ref[0] = jnp.broadcast_to(jnp.sum(chosen, axis=1, keepdims=True), cnt_ref.shape[1:])


def _router(x, mod_l, router_w, router_b):
    bsz, seq, d = x.shape
    t = min(T_MOE, seq)
    nt = seq // t
    n_tok = bsz * seq
    full = lambda shape: pl.BlockSpec(shape, lambda b, i: (0,) * len(shape))
    tok_spec = pl.BlockSpec((TOP_K, t), lambda b, i: (0, b * nt + i))
    return pl.pallas_call(
        _router_kernel,
        grid=(bsz, nt),
        in_specs=[
            pl.BlockSpec((1, t, d), lambda b, i: (b, i, 0)),
            pl.BlockSpec((1, 6, d), lambda b, i: (b, 0, 0)),
            full((N_EXPERTS, d)), full((N_EXPERTS, 1)),
        ],
        out_specs=[tok_spec, tok_spec, tok_spec,
                   pl.BlockSpec((1, N_EXPERTS, LANES), lambda b, i: (b * nt + i, 0, 0))],
        out_shape=[
            jax.ShapeDtypeStruct((TOP_K, n_tok), jnp.int32),
            jax.ShapeDtypeStruct((TOP_K, n_tok), F32),
            jax.ShapeDtypeStruct((TOP_K, n_tok), jnp.int32),
            jax.ShapeDtypeStruct((bsz * nt, N_EXPERTS, LANES), F32),
        ],
        compiler_params=_cparams(("arbitrary", "arbitrary")),
        name="router",
    )(x, mod_l, router_w.T, router_b.reshape(N_EXPERTS, 1))


def _tile_index():
    return pl.program_id(0) * pl.num_programs(1) + pl.program_id(1)


def _for_each_copy(tile, tabs, fn):
    for rows, src_ref, dst_ref, n_ref in ((BIG, *tabs[0:3]), (SEG, *tabs[3:6])):
        width = src_ref.shape[0] // n_ref.shape[0]

        def one(cc, carry, rows=rows, src_ref=src_ref, dst_ref=dst_ref, width=width):
            at = tile * width + cc
            fn(rows, pl.multiple_of(src_ref[at], SEG), pl.multiple_of(dst_ref[at], SEG))
            return carry

        lax.fori_loop(0, n_ref[tile], one, 0)


def _dispatch_kernel(bsrc_ref, bdst_ref, nbig_ref, ssrc_ref, sdst_ref, nsmall_ref, tstart_ref, tlen_ref,
                     x_ref, mod_ref, loc_ref, xs_ref, pm_buf, lbuf, zbuf, sem):
    tile = _tile_index()
    n_tiles = pl.num_programs(0) * pl.num_programs(1)
    t_len = x_ref.shape[1]
    n_loc = lbuf.shape[1]
    x = x_ref[0]
    shift, scale = mod_ref[0, 3:4, :], mod_ref[0, 4:5, :]
    u = (x * (1.0 + scale) + shift).astype(BF16)
    loc = loc_ref[...]
    for jc in range(n_loc // PERM_ROWS):
        j = jc * PERM_ROWS + lax.broadcasted_iota(jnp.int32, (PERM_ROWS, t_len), 0)
        pm = jnp.zeros((PERM_ROWS, t_len), F32)
        for kk in range(TOP_K):
            pm = jnp.where(j == loc[kk:kk + 1, :], 1.0, pm)
        pm_buf[jc * PERM_ROWS:(jc + 1) * PERM_ROWS, :] = pm.astype(BF16)
    slot = tile % 2
    lbuf[slot] = _pack_pairs(jnp.dot(pm_buf[...], u, preferred_element_type=F32))

    tabs = (bsrc_ref, bdst_ref, nbig_ref, ssrc_ref, sdst_ref, nsmall_ref)

    def seg_copy(s, rows, src_row, dst_row):
        return pltpu.make_async_copy(lbuf.at[s, pl.ds(src_row, rows)], xs_ref.at[pl.ds(dst_row, rows)], sem)

    def drain(t):
        _for_each_copy(t, tabs, lambda rows, sr, dr: seg_copy(0, rows, 0, 0).wait())

    @pl.when(tile > 0)
    def _():
        drain(tile - 1)

    _for_each_copy(tile, tabs, lambda rows, sr, dr: seg_copy(slot, rows, sr, dr).start())

    @pl.when(tile == n_tiles - 1)
    def _():
        drain(tile)
        zbuf[...] = jnp.zeros(zbuf.shape, U32)

        def fill(e, rows):
            def zero_copy(cc):
                dst = pl.multiple_of(tstart_ref[e] + cc * rows, SEG)
                return pltpu.make_async_copy(zbuf.at[pl.ds(0, rows)], xs_ref.at[pl.ds(dst, rows)], sem)

            n = tlen_ref[e] // rows

            def start(cc, c2):
                zero_copy(cc).start()
                return c2

            def wait(cc, c2):
                zero_copy(cc).wait()
                return c2

            lax.fori_loop(0, n, start, 0)
            lax.fori_loop(0, n, wait, 0)

        def per_expert(e, carry):
            fill(e, SEG)
            return carry

        lax.fori_loop(0, N_EXPERTS, per_expert, 0)
        fill(N_EXPERTS, BM)


def _dispatch(x, mod_l, loc, plan, n_slots):
    bsz, seq, d = x.shape
    t = min(T_MOE, seq)
    nt = seq // t
    n_loc = TOP_K * t + N_EXPERTS * SEG
    grid_spec = pltpu.PrefetchScalarGridSpec(
        num_scalar_prefetch=8,
        grid=(bsz, nt),
        in_specs=[
            pl.BlockSpec((1, t, d), lambda b, i, *_: (b, i, 0)),
            pl.BlockSpec((1, 6, d), lambda b, i, *_: (b, 0, 0)),
            pl.BlockSpec((TOP_K, t), lambda b, i, *_: (0, b * nt + i)),
        ],
        out_specs=pl.BlockSpec(memory_space=pl.ANY),
        scratch_shapes=[pltpu.VMEM((n_loc, t), BF16), pltpu.VMEM((2, n_loc, d // 2), U32),
                        pltpu.VMEM((BM, d // 2), U32), pltpu.SemaphoreType.DMA(())],
    )
    return pl.pallas_call(
        _dispatch_kernel,
        grid_spec=grid_spec,
        out_shape=jax.ShapeDtypeStruct((n_slots, d // 2), U32),
        compiler_params=_cparams(("arbitrary", "arbitrary")),
        name="dispatch",
    )(*plan["copies"], plan["tstart"], plan["tlen"], x, mod_l, loc)


def _expert_kernel(bexp_ref, nused_ref, xs_ref, wgu_ref, bgu_ref, wdn_ref, bdn_ref, ys_ref, wgu_bf, wdn_bf):
    i = pl.program_id(0)
    active = i < nused_ref[0]
    new_expert = (i == 0) | (bexp_ref[i] != bexp_ref[jnp.maximum(i - 1, 0)])

    @pl.when(active & new_expert)
    def _():
        wgu_bf[...] = wgu_ref[0, 0].astype(BF16)
        wdn_bf[...] = wdn_ref[0, 0].astype(BF16)

    @pl.when(active)
    def _():
        x_lo, x_hi = _unpack_pairs(xs_ref[...])
        half = x_lo.shape[1]
        gu = (jnp.dot(x_lo, wgu_bf[0:half, :], preferred_element_type=F32)
              + jnp.dot(x_hi, wgu_bf[half:, :], preferred_element_type=F32) + bgu_ref[0, 0])
        gate = jnp.minimum(gu[:, :D_EXPERT], SWIGLU_LIMIT)
        up = jnp.clip(gu[:, D_EXPERT:], -SWIGLU_LIMIT, SWIGLU_LIMIT)
        glu = gate * _sigmoid(SWIGLU_ALPHA * gate)
        h = ((up + 1.0) * glu).astype(BF16)
        y = jnp.dot(h, wdn_bf[...], preferred_element_type=F32) + bdn_ref[0, 0]
        ys_ref[...] = _pack_pairs(y.astype(BF16).astype(F32))

    @pl.when(jnp.logical_not(active))
    def _():
        ys_ref[...] = jnp.zeros(ys_ref.shape, U32)


def _experts(xs, block_exp, n_used, layer, w_gu, b_gu, w_dn, b_dn):
    n_slots = xs.shape[0]
    d = w_dn.shape[-1]
    n_blocks = n_slots // BM
    depth = w_gu.shape[0]
    row_map = lambda i, be, nu: (jnp.maximum(jnp.minimum(i, nu[0] - 1), 0), 0)
    exp_map = lambda i, be, nu: (layer, be[i], 0, 0)
    grid_spec = pltpu.PrefetchScalarGridSpec(
        num_scalar_prefetch=2,
        grid=(n_blocks,),
        in_specs=[
            pl.BlockSpec((BM, d // 2), row_map),
            pl.BlockSpec((1, 1, d, 2 * D_EXPERT), exp_map),
            pl.BlockSpec((1, 1, 1, 2 * D_EXPERT), exp_map),
            pl.BlockSpec((1, 1, D_EXPERT, d), exp_map),
            pl.BlockSpec((1, 1, 1, d), exp_map),
        ],
        out_specs=pl.BlockSpec((BM, d // 2), lambda i, be, nu: (i, 0)),
        scratch_shapes=[pltpu.VMEM((d, 2 * D_EXPERT), BF16), pltpu.VMEM((D_EXPERT, d), BF16)],
    )
    return pl.pallas_call(
        _expert_kernel,
        grid_spec=grid_spec,
        out_shape=jax.ShapeDtypeStruct((n_slots, d // 2), U32),
        compiler_params=_cparams(("arbitrary",)),
        name="experts",
    )(block_exp, n_used, xs, w_gu, b_gu.reshape(depth, N_EXPERTS, 1, 2 * D_EXPERT), w_dn,
      b_dn.reshape(depth, N_EXPERTS, 1, d))


def _combine_kernel(bsrc_ref, bdst_ref, nbig_ref, ssrc_ref, sdst_ref, nsmall_ref, used_ref, x_ref, mod_ref, loct_ref, gatet_ref, ys_ref,
                    lng_ref, lnb_ref, o_ref, ybuf, sel_buf, sem):
    tile = _tile_index()
    n_tiles = pl.num_programs(0) * pl.num_programs(1)
    t_len = x_ref.shape[1]
    n_loc = ybuf.shape[1]
    slot = tile % 2

    tabs = (bsrc_ref, bdst_ref, nbig_ref, ssrc_ref, sdst_ref, nsmall_ref)

    def seg_copy(s, rows, src_row, dst_row):
        return pltpu.make_async_copy(ys_ref.at[pl.ds(dst_row, rows)], ybuf.at[s, pl.ds(src_row, rows)], sem.at[s])

    def fetch(t, s):
        used_t = used_ref[t]
        _for_each_copy(t, tabs, lambda rows, sr, dr: seg_copy(s, rows, sr, dr).start())

        def zero_rows(cc, c2):
            ybuf[s, pl.ds(pl.multiple_of(used_t + cc * SEG, SEG), SEG), :] = jnp.zeros((SEG, ybuf.shape[2]), U32)
            return c2

        lax.fori_loop(0, (n_loc - used_t) // SEG, zero_rows, 0)

    @pl.when(tile == 0)
    def _():
        fetch(0, 0)

    @pl.when(tile + 1 < n_tiles)
    def _():
        fetch(tile + 1, 1 - slot)

    loct = loct_ref[...]
    gatet = gatet_ref[...]
    for jc in range(n_loc // LANES):
        j = jc * LANES + lax.broadcasted_iota(jnp.int32, (t_len, LANES), 1)
        sel = jnp.zeros((t_len, LANES), F32)
        for kk in range(TOP_K):
            sel = jnp.where(j == loct[:, kk:kk + 1], gatet[:, kk:kk + 1], sel)
        sel_buf[:, jc * LANES:(jc + 1) * LANES] = sel.astype(BF16)

    _for_each_copy(tile, tabs, lambda rows, sr, dr: seg_copy(slot, rows, 0, 0).wait())

    y_lo, y_hi = _unpack_pairs(ybuf[slot])
    sel = sel_buf[...]
    y = jnp.concatenate([jnp.dot(sel, y_lo, preferred_element_type=F32),
                         jnp.dot(sel, y_hi, preferred_element_type=F32)], axis=1)
    x = x_ref[0]
    gate_f = mod_ref[0, 5:6, :]
    r = DEEPNORM_ALPHA * x + (1.0 + gate_f) * y
    o_ref[0] = _layer_norm(r, lng_ref[...], lnb_ref[...])


def _combine(x, mod_l, loc_t, gates_t, plan, ys, ln_g, ln_b):
    bsz, seq, d = x.shape
    t = min(T_MOE, seq)
    nt = seq // t
    n_loc = TOP_K * t + N_EXPERTS * SEG
    full = lambda shape: pl.BlockSpec(shape, lambda b, i, *_: (0,) * len(shape))
    grid_spec = pltpu.PrefetchScalarGridSpec(
        num_scalar_prefetch=7,
        grid=(bsz, nt),
        in_specs=[
            pl.BlockSpec((1, t, d), lambda b, i, *_: (b, i, 0)),
            pl.BlockSpec((1, 6, d), lambda b, i, *_: (b, 0, 0)),
            pl.BlockSpec((t, TOP_K), lambda b, i, *_: (b * nt + i, 0)),
            pl.BlockSpec((t, TOP_K), lambda b, i, *_: (b * nt + i, 0)),
            pl.BlockSpec(memory_space=pl.ANY),
            full((1, d)), full((1, d)),
        ],
        out_specs=pl.BlockSpec((1, t, d), lambda b, i, *_: (b, i, 0)),
        scratch_shapes=[pltpu.VMEM((2, n_loc, d // 2), U32), pltpu.VMEM((t, n_loc), BF16),
                        pltpu.SemaphoreType.DMA((2,))],
    )
    return pl.pallas_call(
        _combine_kernel,
        grid_spec=grid_spec,
        out_shape=jax.ShapeDtypeStruct((bsz, seq, d), F32),
        compiler_params=_cparams(("arbitrary", "arbitrary")),
        name="combine",
    )(*plan["copies"], plan["used"], x, mod_l, loc_t, gates_t, ys,
      ln_g.reshape(1, d), ln_b.reshape(1, d))


def _slots_kernel(loc0_ref, topi_ref, lrank_ref, loc_ref):
    tile = pl.program_id(0)
    topi = topi_ref[...]
    loc = lrank_ref[...]
    for e in range(N_EXPERTS):
        loc = loc + jnp.where(topi == e, loc0_ref[tile * N_EXPERTS + e], 0)
    loc_ref[...] = loc


def _slots(loc0, topi, lrank, t):
    k, n = topi.shape
    spec = pl.BlockSpec((k, t), lambda i, l0: (0, i))
    return pl.pallas_call(
        _slots_kernel,
        grid_spec=pltpu.PrefetchScalarGridSpec(num_scalar_prefetch=1, grid=(n // t,), in_specs=[spec, spec],
                                               out_specs=spec),
        out_shape=jax.ShapeDtypeStruct((k, n), jnp.int32),
        compiler_params=_cparams(("arbitrary",)),
        name="slots",
    )(loc0, topi, lrank)


def _moe_plan(cnt, n_slots, n_loc):
    c = cnt[:, :, 0].astype(jnp.int32)
    p = (c + SEG - 1) // SEG * SEG
    tot = jnp.sum(p, axis=0)
    padded = (tot + BM - 1) // BM * BM
    pad_end = jnp.cumsum(padded)
    pad_start = pad_end - padded
    seg = pad_start[None, :] + jnp.cumsum(p, axis=0) - p
    loc0 = jnp.cumsum(p, axis=1) - p
    n_blocks = n_slots // BM
    block_lo = jnp.arange(n_blocks, dtype=jnp.int32) * BM
    block_exp = jnp.minimum(jnp.sum((pad_end[None, :] <= block_lo[:, None]).astype(jnp.int32), axis=1),
                            N_EXPERTS - 1)
    experts = jnp.arange(N_EXPERTS, dtype=jnp.int32)

    def copy_table(count, first_loc, first_slot, rows, width):
        end = jnp.cumsum(count, axis=1)
        at = jnp.arange(width, dtype=jnp.int32)
        owner = jnp.sum((end[:, :, None] <= at[None, None, :]).astype(jnp.int32), axis=1)
        mine = owner[:, None, :] == experts[None, :, None]
        pick = lambda a: jnp.sum(jnp.where(mine, a[:, :, None], 0), axis=1)
        step = (at[None, :] - pick(end - count)) * rows
        return (pick(first_loc) + step).reshape(-1), (pick(first_slot) + step).reshape(-1), end[:, -1]

    n_big = p // BIG
    copies = (copy_table(n_big, loc0, seg, BIG, n_loc // BIG)
              + copy_table((p - n_big * BIG) // SEG, loc0 + n_big * BIG, seg + n_big * BIG, SEG,
                           N_EXPERTS * (BIG // SEG - 1)))
    return {
        "copies": copies, "loc0": loc0.reshape(-1), "used": jnp.sum(p, axis=1),
        "tstart": jnp.concatenate([pad_start + tot, pad_end[-1:]]),
        "tlen": jnp.concatenate([padded - tot, n_slots - pad_end[-1:]]),
        "n_used": pad_end[-1:] // BM, "block_exp": block_exp,
    }


def _moe_layer(x, mod_l, layer, router_w, router_b, w_gu, b_gu, w_dn, b_dn, ln_g, ln_b):
    bsz, seq, d = x.shape
    n_tok = bsz * seq
    t = min(T_MOE, seq)
    n_slots = n_tok * TOP_K + (n_tok // t) * N_EXPERTS * SEG + N_EXPERTS * BM
    topi, gates, lrank, cnt = _router(x, mod_l, router_w, router_b)
    plan = _moe_plan(cnt, n_slots, TOP_K * t + N_EXPERTS * SEG)
    loc = _slots(plan["loc0"], topi, lrank, t)
    xs = _dispatch(x, mod_l, loc, plan, n_slots)
    ys = _experts(xs, plan["block_exp"], plan["n_used"], layer, w_gu, b_gu, w_dn, b_dn)
    return _combine(x, mod_l, loc.T, gates.T, plan, ys, ln_g, ln_b)


def kernel(x, c, positions, ada_w, ada_b, ln_mix_g, ln_mix_b, ln_ffn_g, ln_ffn_b, ev_w_in, ev_conv_w, ev_sg_w, ev_sg_b, ev_vn_g, ev_vn_b, ev_w_out, od_w_in, od_dw_w, od_dw_b, od_cn_g, od_cn_b, od_qn_g, od_w_uq, od_kvn_g, od_w_ukv, od_w_out, moe_router_w, moe_router_b, moe_w_gu, moe_b_gu, moe_w_dn, moe_b_dn):
    bsz, seq, d = x.shape
    depth = ada_w.shape[0]
    mod = _ada(c, ada_w, ada_b).reshape(depth, bsz, 6, d)
    cos_t, sin_t = _rope_tables(positions)
    for layer in range(depth):
        i = layer // 2
        mod_l = mod[layer]
        if layer % 2 == 0:
            x = _even_layer(x, mod_l, ev_w_in[i], ev_conv_w[i], ev_sg_w[i], ev_sg_b[i], ev_vn_g[i], ev_vn_b[i],
                            ev_w_out[i], ln_mix_g[layer], ln_mix_b[layer])
        else:
            y_c, q, k, v = _odd_pre(x, mod_l, cos_t, sin_t, od_w_in[i], od_dw_w[i], od_dw_b[i], od_cn_g[i],
                                    od_cn_b[i], od_qn_g[i], od_w_uq[i], od_kvn_g[i], od_w_ukv[i])
            y_d = _attention(q, k, v)
            x = _odd_post(x, mod_l, y_c, y_d, od_w_out[i], ln_mix_g[layer], ln_mix_b[layer])
        x = _moe_layer(x, mod_l, layer, moe_router_w[layer], moe_router_b[layer], moe_w_gu, moe_b_gu,
                       moe_w_dn, moe_b_dn, ln_ffn_g[layer], ln_ffn_b[layer])
    return x
```

```python
import functools
import math

import jax
import jax.numpy as jnp
from jax import lax
from jax.experimental import pallas as pl
from jax.experimental.pallas import tpu as pltpu

F32 = jnp.float32
BF16 = jnp.bfloat16
U32 = jnp.uint32
HIGHEST = lax.Precision.HIGHEST

D_MODEL = 1024
DEPTH = 4
W_A = 512
W_B = 512
SG_HEADS = 8
SG_HEAD_DIM = 64
CHUNK = 128
CONV_A = 3
W_C = 512
CONV_C = 31
MLA_HEADS = 8
QK_NOPE = 64
QK_ROPE = 32
V_DIM = 64
Q_RANK = 256
KV_RANK = 128
ROPE_THETA = 10000.0
N_EXPERTS = 32
TOP_K = 4
D_EXPERT = 1024
SWIGLU_LIMIT = 7.0
SWIGLU_ALPHA = 1.702
DEEPNORM_ALPHA = (2.0 * DEPTH) ** 0.25
LN_EPS = 1e-5
RMS_EPS = 1e-6

LANES = 128
SUBLANES = 8
HEAD_PAD = 128
CONV_C_HALO = 32
CONV_A_HALO = 8
CONV_ROWS = 128

T_EVEN = 512
T_ODD = 256
T_POST = 512
T_ATT = 512
ATT_HEADS = 8
T_MOE = 512
BM = 512
SEG = 8
COPY_ROWS = (64, 32, 16, 8)
PERM_ROWS = 256
N_COPY_TABS = 3 * len(COPY_ROWS)
VMEM_LIMIT = 56 * 1024 * 1024


def _cparams(sem):
    return pltpu.CompilerParams(dimension_semantics=sem, vmem_limit_bytes=VMEM_LIMIT)


def _layer_norm(v, g, b):
    mu = jnp.mean(v, axis=-1, keepdims=True)
    d = v - mu
    var = jnp.mean(d * d, axis=-1, keepdims=True)
    return d * lax.rsqrt(var + LN_EPS) * g + b


def _gelu(v):
    return 0.5 * v * (1.0 + lax.erf(v * (1.0 / math.sqrt(2.0))))


def _sigmoid(v):
    return 1.0 / (1.0 + jnp.exp(-v))


HIGH_HALF = 0xFFFF0000


def _pack_pairs(v):
    n = v.shape[1] // 2
    bits = lax.bitcast_convert_type(v, U32)
    return (bits[:, :n] >> 16) | (bits[:, n:] & jnp.uint32(HIGH_HALF))


def _unpack_pairs(w):
    lo = lax.bitcast_convert_type(w << 16, F32).astype(BF16)
    hi = lax.bitcast_convert_type(w & jnp.uint32(HIGH_HALF), F32).astype(BF16)
    return lo, hi


def _ada_kernel(c_ref, w_ref, b_ref, o_ref):
    c = c_ref[...]
    cond = c * _sigmoid(c)
    o_ref[0] = jnp.dot(cond, w_ref[0], precision=HIGHEST, preferred_element_type=F32) + b_ref[0]


def _ada(c, ada_w, ada_b):
    bsz, d = c.shape
    depth = ada_w.shape[0]
    n_chunk = ada_w.shape[2] // d
    return pl.pallas_call(
        _ada_kernel,
        grid=(depth, n_chunk),
        in_specs=[
            pl.BlockSpec((bsz, d), lambda l, j: (0, 0)),
            pl.BlockSpec((1, d, d), lambda l, j: (l, 0, j)),
            pl.BlockSpec((1, 1, d), lambda l, j: (l, 0, j)),
        ],
        out_specs=pl.BlockSpec((1, bsz, d), lambda l, j: (l, 0, j)),
        out_shape=jax.ShapeDtypeStruct((depth, bsz, n_chunk * d), F32),
        compiler_params=_cparams(("arbitrary", "arbitrary")),
        name="ada",
    )(c, ada_w, ada_b.reshape(depth, 1, n_chunk * d))


def _rope_kernel(pos_ref, freq_ref, cos_ref, sin_ref):
    ang = pos_ref[...].astype(F32) * freq_ref[...]
    cos_ref[...] = jnp.cos(ang)
    sin_ref[...] = jnp.sin(ang)


def _rope_tables(positions):
    n = positions.size
    t = 1024
    inv_freq = ROPE_THETA ** (-jnp.arange(0, QK_ROPE, 2, dtype=F32) / QK_ROPE)
    half = QK_ROPE // 2
    freq = jnp.zeros((1, LANES), F32).at[0, QK_NOPE:QK_NOPE + QK_ROPE].set(jnp.tile(inv_freq, 2))
    del half
    return pl.pallas_call(
        _rope_kernel,
        grid=(n // t,),
        in_specs=[pl.BlockSpec((t, 1), lambda i: (i, 0)), pl.BlockSpec((1, LANES), lambda i: (0, 0))],
        out_specs=[pl.BlockSpec((t, LANES), lambda i: (i, 0))] * 2,
        out_shape=[jax.ShapeDtypeStruct((n, LANES), F32)] * 2,
        compiler_params=_cparams(("arbitrary",)),
        name="rope",
    )(positions.reshape(n, 1), freq)


def _even_kernel(x_ref, mod_ref, win_ref, cw_ref, sgw_ref, sgb_ref, vng_ref, vnb_ref, wout_ref,
                 lng_ref, lnb_ref, o_ref, gbuf, ybuf):
    t_len = x_ref.shape[1]
    halo = CONV_A_HALO

    @pl.when(pl.program_id(1) == 0)
    def _():
        gbuf[0:halo, :] = jnp.zeros((halo, W_A), F32)

    x = x_ref[0]
    shift, scale, gate = mod_ref[0, 0:1, :], mod_ref[0, 1:2, :], mod_ref[0, 2:3, :]
    u = x * (1.0 + scale) + shift
    proj = jnp.dot(u.astype(BF16), win_ref[...], preferred_element_type=F32)
    b_gate = proj[:, 0:W_A]
    c_gate = proj[:, W_A:2 * W_A]
    xa = proj[:, 2 * W_A:3 * W_A]
    zu = proj[:, 3 * W_A:3 * W_A + W_B]
    zv = proj[:, 3 * W_A + W_B:3 * W_A + 2 * W_B]

    g = c_gate * xa
    gbuf[halo:halo + t_len, :] = g
    conv = (cw_ref[0:1, :] * gbuf[halo - 2:halo - 2 + t_len, :]
            + cw_ref[1:2, :] * gbuf[halo - 1:halo - 1 + t_len, :]
            + cw_ref[2:3, :] * g)
    gbuf[0:halo, :] = g[t_len - halo:t_len, :]
    ybuf[:, 0:W_A] = (b_gate * conv).astype(BF16)

    zu = _gelu(zu)
    zv = _layer_norm(_gelu(zv), vng_ref[...], vnb_ref[...]).astype(BF16)
    row = lax.broadcasted_iota(jnp.int32, (CHUNK, CHUNK), 0)
    col = lax.broadcasted_iota(jnp.int32, (CHUNK, CHUNK), 1)
    w_stack = jnp.concatenate(
        [jnp.where(row >= col, sgw_ref[h], 0.0).astype(BF16) for h in range(SG_HEADS)], axis=0)
    col_head = lax.broadcasted_iota(jnp.int32, (CHUNK, W_B), 1) // SG_HEAD_DIM
    for ci in range(t_len // CHUNK):
        lo = ci * CHUNK
        full = jnp.dot(w_stack, zv[lo:lo + CHUNK, :], preferred_element_type=F32)
        mixed = sgb_ref[...]
        for h in range(SG_HEADS):
            mixed = mixed + jnp.where(col_head == h, full[h * CHUNK:(h + 1) * CHUNK, :], 0.0)
        ybuf[lo:lo + CHUNK, W_A:W_A + W_B] = (zu[lo:lo + CHUNK, :] * mixed).astype(BF16)

    y = jnp.dot(ybuf[...], wout_ref[...], preferred_element_type=F32)
    r = DEEPNORM_ALPHA * x + (1.0 + gate) * y
    o_ref[0] = _layer_norm(r, lng_ref[...], lnb_ref[...])


def _even_layer(x, mod_l, w_in, conv_w, sg_w, sg_b, vn_g, vn_b, w_out, ln_g, ln_b):
    bsz, seq, d = x.shape
    t = min(T_EVEN, seq)
    sgb_full = jnp.repeat(sg_b.T, SG_HEAD_DIM, axis=1)
    full = lambda shape: pl.BlockSpec(shape, lambda b, i: (0,) * len(shape))
    return pl.pallas_call(
        _even_kernel,
        grid=(bsz, seq // t),
        in_specs=[
            pl.BlockSpec((1, t, d), lambda b, i: (b, i, 0)),
            pl.BlockSpec((1, 6, d), lambda b, i: (b, 0, 0)),
            full(w_in.shape), full(conv_w.shape), full(sg_w.shape), full(sgb_full.shape),
            full((1, W_B)), full((1, W_B)), full(w_out.shape), full((1, d)), full((1, d)),
        ],
        out_specs=pl.BlockSpec((1, t, d), lambda b, i: (b, i, 0)),
        out_shape=jax.ShapeDtypeStruct((bsz, seq, d), F32),
        scratch_shapes=[pltpu.VMEM((CONV_A_HALO + t, W_A), F32), pltpu.VMEM((t, W_A + W_B), BF16)],
        compiler_params=_cparams(("arbitrary", "arbitrary")),
        name="even",
    )(x, mod_l, w_in.astype(BF16), conv_w, sg_w, sgb_full, vn_g.reshape(1, W_B), vn_b.reshape(1, W_B),
      w_out.astype(BF16), ln_g.reshape(1, d), ln_b.reshape(1, d))


N_GLU = 2 * W_C
OFF_Q = N_GLU
OFF_KV = OFF_Q + Q_RANK
OFF_KR = OFF_KV + KV_RANK
OFF_KR_ROT = OFF_KR + HEAD_PAD
ODD_IN_PAD = OFF_KR_ROT + HEAD_PAD


def _odd_pre_kernel(x_ref, mod_ref, cos_ref, sin_ref, win_ref, dww_ref, dwb_ref, cng_ref, cnb_ref,
                    qng_ref, wq_ref, wqr_ref, kvng_ref, wk_ref, wv_ref,
                    yc_ref, q_ref, k_ref, v_ref, gbuf, sbuf, hbuf):
    t_len = x_ref.shape[1]
    halo = CONV_C_HALO

    @pl.when(pl.program_id(1) == 0)
    def _():
        gbuf[0:halo, :] = jnp.zeros((halo, W_C), F32)

    x = x_ref[0]
    shift, scale = mod_ref[0, 0:1, :], mod_ref[0, 1:2, :]
    u = x * (1.0 + scale) + shift
    proj = jnp.dot(u.astype(BF16), win_ref[...], preferred_element_type=F32)

    g = proj[:, 0:W_C] * _sigmoid(proj[:, W_C:2 * W_C])
    gbuf[halo:halo + t_len, :] = g
    first = halo - (CONV_C - 1)
    for res in range(1, SUBLANES):
        span = max(first + k - res for k in range(CONV_C) if (first + k) % SUBLANES == res) + t_len
        sbuf[res - 1, 0:span, :] = gbuf[res:res + span, :]
    for rb in range(t_len // CONV_ROWS):
        for cb in range(W_C // LANES):
            cs = slice(cb * LANES, (cb + 1) * LANES)
            acc = jnp.broadcast_to(dwb_ref[:, cs], (CONV_ROWS, LANES))
            for k in range(CONV_C):
                res = (first + k) % SUBLANES
                lo = first + k - res + rb * CONV_ROWS
                tap = gbuf[lo:lo + CONV_ROWS, cs] if res == 0 else sbuf[res - 1, lo:lo + CONV_ROWS, cs]
                acc = acc + dww_ref[k:k + 1, cs] * tap
            hbuf[rb * CONV_ROWS:(rb + 1) * CONV_ROWS, cs] = acc
    acc = hbuf[...]
    gbuf[0:halo, :] = g[t_len - halo:t_len, :]
    hn = _layer_norm(acc, cng_ref[...], cnb_ref[...])
    yc_ref[0] = (hn * _sigmoid(hn)).astype(BF16)

    cos = cos_ref[...]
    sin = sin_ref[...]
    cq = proj[:, OFF_Q:OFF_Q + Q_RANK]
    q_lat = (cq * lax.rsqrt(jnp.mean(cq * cq, axis=-1, keepdims=True) + RMS_EPS) * qng_ref[...]).astype(BF16)
    ckv = proj[:, OFF_KV:OFF_KV + KV_RANK]
    kv_lat = (ckv * lax.rsqrt(jnp.mean(ckv * ckv, axis=-1, keepdims=True) + RMS_EPS) * kvng_ref[...]).astype(BF16)
    q_all = jnp.dot(q_lat, wq_ref[...], preferred_element_type=F32)
    q_rot = jnp.dot(q_lat, wqr_ref[...], preferred_element_type=F32)
    k_all = jnp.dot(kv_lat, wk_ref[...], preferred_element_type=F32)
    v_all = jnp.dot(kv_lat, wv_ref[...], preferred_element_type=F32)
    k_rope = proj[:, OFF_KR:OFF_KR + HEAD_PAD] * cos + proj[:, OFF_KR_ROT:OFF_KR_ROT + HEAD_PAD] * sin
    sm_scale = math.log2(math.e) / math.sqrt(QK_NOPE + QK_ROPE)
    for h in range(MLA_HEADS):
        sl = slice(h * HEAD_PAD, (h + 1) * HEAD_PAD)
        q_ref[0, h] = ((q_all[:, sl] * cos + q_rot[:, sl] * sin) * sm_scale).astype(BF16)
        k_ref[0, h] = (k_all[:, sl] + k_rope).astype(BF16)
    for hp in range(MLA_HEADS // 2):
        v_ref[0, hp] = v_all[:, hp * LANES:(hp + 1) * LANES].astype(BF16)


def _odd_weights(w_in, w_uq, w_ukv):
    d = w_in.shape[0]
    half = QK_ROPE // 2
    kr = w_in[:, OFF_KR:OFF_KR + QK_ROPE]
    z = lambda n: jnp.zeros((d, n), w_in.dtype)
    kr_blk = jnp.concatenate([z(QK_NOPE), kr, z(HEAD_PAD - QK_NOPE - QK_ROPE)], axis=1)
    kr_rot = jnp.concatenate([z(QK_NOPE), -kr[:, half:], kr[:, :half], z(HEAD_PAD - QK_NOPE - QK_ROPE)], axis=1)
    w_in_p = jnp.concatenate([w_in[:, :OFF_KR], kr_blk, kr_rot], axis=1).astype(BF16)

    dq = QK_NOPE + QK_ROPE
    wq = w_uq.reshape(Q_RANK, MLA_HEADS, dq)
    zq = lambda n: jnp.zeros((Q_RANK, MLA_HEADS, n), w_uq.dtype)
    wq_p = jnp.concatenate([wq, zq(HEAD_PAD - dq)], axis=2).reshape(Q_RANK, MLA_HEADS * HEAD_PAD)
    wq_r = jnp.concatenate([zq(QK_NOPE), -wq[:, :, QK_NOPE + half:], wq[:, :, QK_NOPE:QK_NOPE + half],
                            zq(HEAD_PAD - dq)], axis=2).reshape(Q_RANK, MLA_HEADS * HEAD_PAD)

    wkv = w_ukv.reshape(KV_RANK, MLA_HEADS, QK_NOPE + V_DIM)
    wk_p = jnp.concatenate([wkv[:, :, :QK_NOPE], jnp.zeros((KV_RANK, MLA_HEADS, HEAD_PAD - QK_NOPE), w_ukv.dtype)],
                           axis=2).reshape(KV_RANK, MLA_HEADS * HEAD_PAD)
    wv_p = wkv[:, :, QK_NOPE:].reshape(KV_RANK, MLA_HEADS * V_DIM)
    return w_in_p, wq_p.astype(BF16), wq_r.astype(BF16), wk_p.astype(BF16), wv_p.astype(BF16)


def _odd_pre(x, mod_l, cos_t, sin_t, w_in, dw_w, dw_b, cn_g, cn_b, qn_g, w_uq, kvn_g, w_ukv):
    bsz, seq, d = x.shape
    t = min(T_ODD, seq)
    nt = seq // t
    w_in_p, wq_p, wq_r, wk_p, wv_p = _odd_weights(w_in, w_uq, w_ukv)
    full = lambda shape: pl.BlockSpec(shape, lambda b, i: (0,) * len(shape))
    return pl.pallas_call(
        _odd_pre_kernel,
        grid=(bsz, nt),
        in_specs=[
            pl.BlockSpec((1, t, d), lambda b, i: (b, i, 0)),
            pl.BlockSpec((1, 6, d), lambda b, i: (b, 0, 0)),
            pl.BlockSpec((t, LANES), lambda b, i: (b * nt + i, 0)),
            pl.BlockSpec((t, LANES), lambda b, i: (b * nt + i, 0)),
            full(w_in_p.shape), full(dw_w.shape), full((1, W_C)), full((1, W_C)), full((1, W_C)),
            full((1, Q_RANK)), full(wq_p.shape), full(wq_r.shape), full((1, KV_RANK)),
            full(wk_p.shape), full(wv_p.shape),
        ],
        out_specs=[
            pl.BlockSpec((1, t, W_C), lambda b, i: (b, i, 0)),
            pl.BlockSpec((1, MLA_HEADS, t, HEAD_PAD), lambda b, i: (b, 0, i, 0)),
            pl.BlockSpec((1, MLA_HEADS, t, HEAD_PAD), lambda b, i: (b, 0, i, 0)),
            pl.BlockSpec((1, MLA_HEADS // 2, t, LANES), lambda b, i: (b, 0, i, 0)),
        ],
        out_shape=[
            jax.ShapeDtypeStruct((bsz, seq, W_C), BF16),
            jax.ShapeDtypeStruct((bsz, MLA_HEADS, seq, HEAD_PAD), BF16),
            jax.ShapeDtypeStruct((bsz, MLA_HEADS, seq, HEAD_PAD), BF16),
            jax.ShapeDtypeStruct((bsz, MLA_HEADS // 2, seq, LANES), BF16),
        ],
        scratch_shapes=[pltpu.VMEM((CONV_C_HALO + t, W_C), F32),
                        pltpu.VMEM((SUBLANES - 1, CONV_C_HALO + t, W_C), F32), pltpu.VMEM((t, W_C), F32)],
        compiler_params=_cparams(("arbitrary", "arbitrary")),
        name="odd_pre",
    )(x, mod_l, cos_t, sin_t, w_in_p, dw_w, dw_b.reshape(1, W_C), cn_g.reshape(1, W_C), cn_b.reshape(1, W_C),
      qn_g.reshape(1, Q_RANK), wq_p, wq_r, kvn_g.reshape(1, KV_RANK), wk_p, wv_p)


ATT_FULL, ATT_DIAG_LATE, ATT_DIAG_EARLY = 0, 1, 2


def _attn_kernel(kj_tab, kind_tab, q_ref, k_ref, v_ref, o_ref, m_sc, l_sc, acc_sc):
    step_id = pl.program_id(2)
    kind = kind_tab[step_id]
    tq = q_ref.shape[2]
    tk = k_ref.shape[2]

    @pl.when(kj_tab[step_id] == 0)
    def _():
        m_sc[...] = jnp.full(m_sc.shape, -jnp.inf, F32)
        l_sc[...] = jnp.zeros(l_sc.shape, F32)
        acc_sc[...] = jnp.zeros(acc_sc.shape, F32)

    first_lanes = lax.broadcasted_iota(jnp.int32, (tq, LANES), 1) < V_DIM
    n_pairs = v_ref.shape[1]

    def step(n_keys, row_shift):
        n_col = n_keys // LANES
        for pp in range(n_pairs):
            v_pair = v_ref[0, pp, 0:n_keys, :]
            alphas = []
            pvs = []
            for a in range(2 * pp, 2 * pp + 2):
                s = lax.dot_general(q_ref[0, a], k_ref[0, a, 0:n_keys, :], (((1,), (1,)), ((), ())),
                                    preferred_element_type=F32)
                if row_shift is not None:
                    rows = row_shift + lax.broadcasted_iota(jnp.int32, (tq, n_keys), 0)
                    cols = lax.broadcasted_iota(jnp.int32, (tq, n_keys), 1)
                    s = jnp.where(cols <= rows, s, -1e30)
                cols_s = [s[:, c * LANES:(c + 1) * LANES] for c in range(n_col)]
                m_lane = cols_s[0]
                for c in range(1, n_col):
                    m_lane = jnp.maximum(m_lane, cols_s[c])
                m_prev = m_sc[a]
                m_next = jnp.maximum(m_prev, jnp.max(m_lane, axis=1, keepdims=True))
                alpha = jnp.exp2(m_prev - m_next)
                ps = [jnp.exp2(cs - m_next) for cs in cols_s]
                l_lane = ps[0]
                for c in range(1, n_col):
                    l_lane = l_lane + ps[c]
                l_sc[a] = alpha * l_sc[a] + l_lane
                m_sc[a] = m_next
                alphas.append(alpha)
                p = jnp.concatenate([pc.astype(BF16) for pc in ps], axis=1)
                pvs.append(jnp.dot(p, v_pair, preferred_element_type=F32))
            alpha_sel = jnp.where(first_lanes, alphas[0], alphas[1])
            acc_sc[pp] = alpha_sel * acc_sc[pp] + jnp.where(first_lanes, pvs[0], pvs[1])

    @pl.when(kind == ATT_FULL)
    def _():
        step(tk, None)

    @pl.when(kind == ATT_DIAG_LATE)
    def _():
        step(tk, tq)

    @pl.when(kind == ATT_DIAG_EARLY)
    def _():
        step(tq, 0)

    @pl.when(kind != ATT_FULL)
    def _():
        for pp in range(n_pairs):
            l_sel = jnp.where(first_lanes, jnp.sum(l_sc[2 * pp], axis=1, keepdims=True),
                              jnp.sum(l_sc[2 * pp + 1], axis=1, keepdims=True))
            o_ref[0, :, pp * LANES:(pp + 1) * LANES] = (acc_sc[pp] / l_sel).astype(o_ref.dtype)


def _attention(q, k, v):
    bsz, heads, seq, _ = q.shape
    t = min(T_ATT, seq // 2)
    nq = seq // t
    steps = [(i, j, ATT_FULL if j < i // 2 else (ATT_DIAG_LATE if i % 2 else ATT_DIAG_EARLY))
             for i in range(nq) for j in range(i // 2 + 1)]
    qi_tab = jnp.asarray([st[0] for st in steps], jnp.int32)
    kj_tab = jnp.asarray([st[1] for st in steps], jnp.int32)
    kind_tab = jnp.asarray([st[2] for st in steps], jnp.int32)
    hs = ATT_HEADS
    grid_spec = pltpu.PrefetchScalarGridSpec(
        num_scalar_prefetch=3,
        grid=(bsz, heads // hs, len(steps)),
        in_specs=[
            pl.BlockSpec((1, hs, t, HEAD_PAD), lambda b, hg, s, qt, kt, kd: (b, hg, qt[s], 0)),
            pl.BlockSpec((1, hs, 2 * t, HEAD_PAD), lambda b, hg, s, qt, kt, kd: (b, hg, kt[s], 0)),
            pl.BlockSpec((1, hs // 2, 2 * t, LANES), lambda b, hg, s, qt, kt, kd: (b, hg, kt[s], 0)),
        ],
        out_specs=pl.BlockSpec((1, t, hs * V_DIM), lambda b, hg, s, qt, kt, kd: (b, qt[s], hg)),
        scratch_shapes=[pltpu.VMEM((hs, t, LANES), F32), pltpu.VMEM((hs, t, LANES), F32),
                        pltpu.VMEM((hs // 2, t, LANES), F32)],
    )
    return pl.pallas_call(
        _attn_kernel_entry,
        grid_spec=grid_spec,
        out_shape=jax.ShapeDtypeStruct((bsz, seq, heads * V_DIM), BF16),
        compiler_params=_cparams(("arbitrary", "arbitrary", "arbitrary")),
        name="attn",
    )(qi_tab, kj_tab, kind_tab, q, k, v)


def _attn_kernel_entry(qi_tab, kj_tab, kind_tab, *refs):
    del qi_tab
    _attn_kernel(kj_tab, kind_tab, *refs)


def _odd_post_kernel(x_ref, mod_ref, yc_ref, yd_ref, wout_ref, lng_ref, lnb_ref, o_ref):
    x = x_ref[0]
    gate = mod_ref[0, 2:3, :]
    y = (jnp.dot(yc_ref[0], wout_ref[0:W_C, :], preferred_element_type=F32)
         + jnp.dot(yd_ref[0], wout_ref[W_C:, :], preferred_element_type=F32))
    r = DEEPNORM_ALPHA * x + (1.0 + gate) * y
    o_ref[0] = _layer_norm(r, lng_ref[...], lnb_ref[...])


def _odd_post(x, mod_l, y_c, y_d, w_out, ln_g, ln_b):
    bsz, seq, d = x.shape
    t = min(T_POST, seq)
    full = lambda shape: pl.BlockSpec(shape, lambda b, i: (0,) * len(shape))
    return pl.pallas_call(
        _odd_post_kernel,
        grid=(bsz, seq // t),
        in_specs=[
            pl.BlockSpec((1, t, d), lambda b, i: (b, i, 0)),
            pl.BlockSpec((1, 6, d), lambda b, i: (b, 0, 0)),
            pl.BlockSpec((1, t, W_C), lambda b, i: (b, i, 0)),
            pl.BlockSpec((1, t, MLA_HEADS * V_DIM), lambda b, i: (b, i, 0)),
            full(w_out.shape), full((1, d)), full((1, d)),
        ],
        out_specs=pl.BlockSpec((1, t, d), lambda b, i: (b, i, 0)),
        out_shape=jax.ShapeDtypeStruct((bsz, seq, d), F32),
        compiler_params=_cparams(("arbitrary", "arbitrary")),
        name="odd_post",
    )(x, mod_l, y_c, y_d, w_out.astype(BF16), ln_g.reshape(1, d), ln_b.reshape(1, d))


def _router_kernel(x_ref, mod_ref, rw_ref, rb_ref, topi_ref, gate_ref, lrank_ref, cnt_ref):
    t_len = x_ref.shape[1]
    x = x_ref[0]
    shift, scale = mod_ref[0, 3:4, :], mod_ref[0, 4:5, :]
    u = x * (1.0 + scale) + shift
    logits = lax.dot_general(rw_ref[...], u, (((1,), (1,)), ((), ())), precision=HIGHEST,
                             preferred_element_type=F32) + rb_ref[...]
    eidx = lax.broadcasted_iota(jnp.int32, (N_EXPERTS, t_len), 0)
    vals, hots = [], []
    for _ in range(TOP_K):
        m = jnp.max(logits, axis=0, keepdims=True)
        idx = jnp.min(jnp.where(logits == m, eidx, N_EXPERTS), axis=0, keepdims=True)
        hot = eidx == idx
        vals.append(m)
        hots.append(hot)
        logits = jnp.where(hot, -jnp.inf, logits)
        topi_ref[len(vals) - 1:len(vals), :] = idx
    exps = [jnp.exp(v - vals[0]) for v in vals]
    denom = exps[0] + exps[1] + exps[2] + exps[3]
    for kk in range(TOP_K):
        gate_ref[kk:kk + 1, :] = exps[kk] / denom

    chosen = jnp.where(hots[0] | hots[1] | hots[2] | hots[3], 1.0, 0.0)
    r_i = lax.broadcasted_iota(jnp.int32, (t_len, t_len), 0)
    c_i = lax.broadcasted_iota(jnp.int32, (t_len, t_len), 1)
    before = jnp.where(r_i < c_i, 1.0, 0.0).astype(BF16)
    base = jnp.dot(chosen.astype(BF16), before, preferred_element_type=F32)
    for kk in range(TOP_K):
        lrank_ref[kk:kk + 1, :] = jnp.sum(jnp.where(hots[kk], base, 0.0), axis=0, keepdims=True).astype(jnp.int32)
    cnt_ref[0] = jnp.broadcast_to(jnp.sum(chosen, axis=1, keepdims=True), cnt_ref.shape[1:])


def _router(x, mod_l, router_w, router_b):
    bsz, seq, d = x.shape
    t = min(T_MOE, seq)
    nt = seq // t
    n_tok = bsz * seq
    full = lambda shape: pl.BlockSpec(shape, lambda b, i: (0,) * len(shape))
    tok_spec = pl.BlockSpec((TOP_K, t), lambda b, i: (0, b * nt + i))
    return pl.pallas_call(
        _router_kernel,
        grid=(bsz, nt),
        in_specs=[
            pl.BlockSpec((1, t, d), lambda b, i: (b, i, 0)),
            pl.BlockSpec((1, 6, d), lambda b, i: (b, 0, 0)),
            full((N_EXPERTS, d)), full((N_EXPERTS, 1)),
        ],
        out_specs=[tok_spec, tok_spec, tok_spec,
                   pl.BlockSpec((1, N_EXPERTS, LANES), lambda b, i: (b * nt + i, 0, 0))],
        out_shape=[
            jax.ShapeDtypeStruct((TOP_K, n_tok), jnp.int32),
            jax.ShapeDtypeStruct((TOP_K, n_tok), F32),
            jax.ShapeDtypeStruct((TOP_K, n_tok), jnp.int32),
            jax.ShapeDtypeStruct((bsz * nt, N_EXPERTS, LANES), F32),
        ],
        compiler_params=_cparams(("arbitrary", "arbitrary")),
        name="router",
    )(x, mod_l, router_w.T, router_b.reshape(N_EXPERTS, 1))


def _tile_index():
    return pl.program_id(0) * pl.num_programs(1) + pl.program_id(1)


def _for_each_copy(tile, tabs, fn):
    for idx, rows in enumerate(COPY_ROWS):
        src_ref, dst_ref, n_ref = tabs[3 * idx:3 * idx + 3]
        width = src_ref.shape[0] // n_ref.shape[0]

        def one(cc, carry, rows=rows, src_ref=src_ref, dst_ref=dst_ref, width=width):
            at = tile * width + cc
            fn(rows, pl.multiple_of(src_ref[at], SEG), pl.multiple_of(dst_ref[at], SEG))
            return carry

        lax.fori_loop(0, n_ref[tile], one, 0)


def _dispatch_kernel(*refs):
    tabs = refs[:N_COPY_TABS]
    tstart_ref, tlen_ref, x_ref, mod_ref, loc_ref, xs_ref, pm_buf, lbuf, zbuf, sem = refs[N_COPY_TABS:]
    _dispatch_body(tabs, tstart_ref, tlen_ref, x_ref, mod_ref, loc_ref, xs_ref, pm_buf, lbuf, zbuf, sem)


def _dispatch_body(tabs, tstart_ref, tlen_ref, x_ref, mod_ref, loc_ref, xs_ref, pm_buf, lbuf, zbuf, sem):
    tile = _tile_index()
    n_tiles = pl.num_programs(0) * pl.num_programs(1)
    t_len = x_ref.shape[1]
    n_loc = lbuf.shape[1]
    x = x_ref[0]
    shift, scale = mod_ref[0, 3:4, :], mod_ref[0, 4:5, :]
    u = (x * (1.0 + scale) + shift).astype(BF16)
    loc = loc_ref[...]
    for jc in range(n_loc // PERM_ROWS):
        j = jc * PERM_ROWS + lax.broadcasted_iota(jnp.int32, (PERM_ROWS, t_len), 0)
        pm = jnp.zeros((PERM_ROWS, t_len), F32)
        for kk in range(TOP_K):
            pm = jnp.where(j == loc[kk:kk + 1, :], 1.0, pm)
        pm_buf[jc * PERM_ROWS:(jc + 1) * PERM_ROWS, :] = pm.astype(BF16)
    slot = tile % 2
    lbuf[slot] = _pack_pairs(jnp.dot(pm_buf[...], u, preferred_element_type=F32))

    def seg_copy(s, rows, src_row, dst_row):
        return pltpu.make_async_copy(lbuf.at[s, pl.ds(src_row, rows)], xs_ref.at[pl.ds(dst_row, rows)], sem)

    def drain(t):
        _for_each_copy(t, tabs, lambda rows, sr, dr: seg_copy(0, rows, 0, 0).wait())

    @pl.when(tile > 0)
    def _():
        drain(tile - 1)

    _for_each_copy(tile, tabs, lambda rows, sr, dr: seg_copy(slot, rows, sr, dr).start())

    @pl.when(tile == n_tiles - 1)
    def _():
        drain(tile)
        zbuf[...] = jnp.zeros(zbuf.shape, U32)

        def fill(e, rows, done_rows=0):
            done = tlen_ref[e] // done_rows * done_rows if done_rows else 0

            def zero_copy(cc):
                dst = pl.multiple_of(tstart_ref[e] + done + cc * rows, SEG)
                return pltpu.make_async_copy(zbuf.at[pl.ds(0, rows)], xs_ref.at[pl.ds(dst, rows)], sem)

            n = (tlen_ref[e] - done) // rows

            def start(cc, c2):
                zero_copy(cc).start()
                return c2

            def wait(cc, c2):
                zero_copy(cc).wait()
                return c2

            lax.fori_loop(0, n, start, 0)
            lax.fori_loop(0, n, wait, 0)

        def per_expert(e, carry):
            fill(e, COPY_ROWS[0])
            fill(e, SEG, COPY_ROWS[0])
            return carry

        lax.fori_loop(0, N_EXPERTS, per_expert, 0)
        fill(N_EXPERTS, BM)


def _dispatch(x, mod_l, loc, plan, n_slots):
    bsz, seq, d = x.shape
    t = min(T_MOE, seq)
    nt = seq // t
    n_loc = TOP_K * t + N_EXPERTS * SEG
    grid_spec = pltpu.PrefetchScalarGridSpec(
        num_scalar_prefetch=N_COPY_TABS + 2,
        grid=(bsz, nt),
        in_specs=[
            pl.BlockSpec((1, t, d), lambda b, i, *_: (b, i, 0)),
            pl.BlockSpec((1, 6, d), lambda b, i, *_: (b, 0, 0)),
            pl.BlockSpec((TOP_K, t), lambda b, i, *_: (0, b * nt + i)),
        ],
        out_specs=pl.BlockSpec(memory_space=pl.ANY),
        scratch_shapes=[pltpu.VMEM((n_loc, t), BF16), pltpu.VMEM((2, n_loc, d // 2), U32),
                        pltpu.VMEM((BM, d // 2), U32), pltpu.SemaphoreType.DMA(())],
    )
    return pl.pallas_call(
        _dispatch_kernel,
        grid_spec=grid_spec,
        out_shape=jax.ShapeDtypeStruct((n_slots, d // 2), U32),
        compiler_params=_cparams(("arbitrary", "arbitrary")),
        name="dispatch",
    )(*plan["copies"], plan["tstart"], plan["tlen"], x, mod_l, loc)


def _expert_kernel(bexp_ref, nused_ref, xs_ref, wgu_ref, bgu_ref, wdn_ref, bdn_ref, ys_ref, wgu_bf, wdn_bf):
    i = pl.program_id(0)
    active = i < nused_ref[0]
    new_expert = (i == 0) | (bexp_ref[i] != bexp_ref[jnp.maximum(i - 1, 0)])

    @pl.when(active & new_expert)
    def _():
        wgu_bf[...] = wgu_ref[0, 0].astype(BF16)
        wdn_bf[...] = wdn_ref[0, 0].astype(BF16)

    @pl.when(active)
    def _():
        x_lo, x_hi = _unpack_pairs(xs_ref[...])
        half = x_lo.shape[1]
        gu = (jnp.dot(x_lo, wgu_bf[0:half, :], preferred_element_type=F32)
              + jnp.dot(x_hi, wgu_bf[half:, :], preferred_element_type=F32) + bgu_ref[0, 0])
        gate = jnp.minimum(gu[:, :D_EXPERT], SWIGLU_LIMIT)
        up = jnp.clip(gu[:, D_EXPERT:], -SWIGLU_LIMIT, SWIGLU_LIMIT)
        glu = gate * _sigmoid(SWIGLU_ALPHA * gate)
        h = ((up + 1.0) * glu).astype(BF16)
        y = jnp.dot(h, wdn_bf[...], preferred_element_type=F32) + bdn_ref[0, 0]
        ys_ref[...] = _pack_pairs(y.astype(BF16).astype(F32))

    @pl.when(jnp.logical_not(active))
    def _():
        ys_ref[...] = jnp.zeros(ys_ref.shape, U32)


def _experts(xs, block_exp, n_used, layer, w_gu, b_gu, w_dn, b_dn):
    n_slots = xs.shape[0]
    d = w_dn.shape[-1]
    n_blocks = n_slots // BM
    depth = w_gu.shape[0]
    row_map = lambda i, be, nu: (jnp.maximum(jnp.minimum(i, nu[0] - 1), 0), 0)
    exp_map = lambda i, be, nu: (layer, be[i], 0, 0)
    grid_spec = pltpu.PrefetchScalarGridSpec(
        num_scalar_prefetch=2,
        grid=(n_blocks,),
        in_specs=[
            pl.BlockSpec((BM, d // 2), row_map),
            pl.BlockSpec((1, 1, d, 2 * D_EXPERT), exp_map),
            pl.BlockSpec((1, 1, 1, 2 * D_EXPERT), exp_map),
            pl.BlockSpec((1, 1, D_EXPERT, d), exp_map),
            pl.BlockSpec((1, 1, 1, d), exp_map),
        ],
        out_specs=pl.BlockSpec((BM, d // 2), lambda i, be, nu: (i, 0)),
        scratch_shapes=[pltpu.VMEM((d, 2 * D_EXPERT), BF16), pltpu.VMEM((D_EXPERT, d), BF16)],
    )
    return pl.pallas_call(
        _expert_kernel,
        grid_spec=grid_spec,
        out_shape=jax.ShapeDtypeStruct((n_slots, d // 2), U32),
        compiler_params=_cparams(("arbitrary",)),
        name="experts",
    )(block_exp, n_used, xs, w_gu, b_gu.reshape(depth, N_EXPERTS, 1, 2 * D_EXPERT), w_dn,
      b_dn.reshape(depth, N_EXPERTS, 1, d))


def _combine_kernel(*refs):
    _combine_body(refs[:N_COPY_TABS], *refs[N_COPY_TABS:])


def _combine_body(tabs, used_ref, x_ref, mod_ref, loct_ref, gatet_ref, ys_ref,
                  lng_ref, lnb_ref, o_ref, ybuf, sel_buf, sem):
    tile = _tile_index()
    n_tiles = pl.num_programs(0) * pl.num_programs(1)
    t_len = x_ref.shape[1]
    n_loc = ybuf.shape[1]
    slot = tile % 2

    def seg_copy(s, rows, src_row, dst_row):
        return pltpu.make_async_copy(ys_ref.at[pl.ds(dst_row, rows)], ybuf.at[s, pl.ds(src_row, rows)], sem.at[s])

    def fetch(t, s):
        used_t = used_ref[t]
        _for_each_copy(t, tabs, lambda rows, sr, dr: seg_copy(s, rows, sr, dr).start())

        def zero_rows(cc, c2):
            ybuf[s, pl.ds(pl.multiple_of(used_t + cc * SEG, SEG), SEG), :] = jnp.zeros((SEG, ybuf.shape[2]), U32)
            return c2

        lax.fori_loop(0, (n_loc - used_t) // SEG, zero_rows, 0)

    @pl.when(tile == 0)
    def _():
        fetch(0, 0)

    @pl.when(tile + 1 < n_tiles)
    def _():
        fetch(tile + 1, 1 - slot)

    loct = loct_ref[...]
    gatet = gatet_ref[...]
    for jc in range(n_loc // LANES):
        j = jc * LANES + lax.broadcasted_iota(jnp.int32, (t_len, LANES), 1)
        sel = jnp.zeros((t_len, LANES), F32)
        for kk in range(TOP_K):
            sel = jnp.where(j == loct[:, kk:kk + 1], gatet[:, kk:kk + 1], sel)
        sel_buf[:, jc * LANES:(jc + 1) * LANES] = sel.astype(BF16)

    _for_each_copy(tile, tabs, lambda rows, sr, dr: seg_copy(slot, rows, 0, 0).wait())

    y_lo, y_hi = _unpack_pairs(ybuf[slot])
    sel = sel_buf[...]
    y = jnp.concatenate([jnp.dot(sel, y_lo, preferred_element_type=F32),
                         jnp.dot(sel, y_hi, preferred_element_type=F32)], axis=1)
    x = x_ref[0]
    gate_f = mod_ref[0, 5:6, :]
    r = DEEPNORM_ALPHA * x + (1.0 + gate_f) * y
    o_ref[0] = _layer_norm(r, lng_ref[...], lnb_ref[...])


def _combine(x, mod_l, loc_t, gates_t, plan, ys, ln_g, ln_b):
    bsz, seq, d = x.shape
    t = min(T_MOE, seq)
    nt = seq // t
    n_loc = TOP_K * t + N_EXPERTS * SEG
    full = lambda shape: pl.BlockSpec(shape, lambda b, i, *_: (0,) * len(shape))
    grid_spec = pltpu.PrefetchScalarGridSpec(
        num_scalar_prefetch=N_COPY_TABS + 1,
        grid=(bsz, nt),
        in_specs=[
            pl.BlockSpec((1, t, d), lambda b, i, *_: (b, i, 0)),
            pl.BlockSpec((1, 6, d), lambda b, i, *_: (b, 0, 0)),
            pl.BlockSpec((t, TOP_K), lambda b, i, *_: (b * nt + i, 0)),
            pl.BlockSpec((t, TOP_K), lambda b, i, *_: (b * nt + i, 0)),
            pl.BlockSpec(memory_space=pl.ANY),
            full((1, d)), full((1, d)),
        ],
        out_specs=pl.BlockSpec((1, t, d), lambda b, i, *_: (b, i, 0)),
        scratch_shapes=[pltpu.VMEM((2, n_loc, d // 2), U32), pltpu.VMEM((t, n_loc), BF16),
                        pltpu.SemaphoreType.DMA((2,))],
    )
    return pl.pallas_call(
        _combine_kernel,
        grid_spec=grid_spec,
        out_shape=jax.ShapeDtypeStruct((bsz, seq, d), F32),
        compiler_params=_cparams(("arbitrary", "arbitrary")),
        name="combine",
    )(*plan["copies"], plan["used"], x, mod_l, loc_t, gates_t, ys,
      ln_g.reshape(1, d), ln_b.reshape(1, d))


def _slots_kernel(loc0_ref, topi_ref, lrank_ref, loc_ref):
    tile = pl.program_id(0)
    topi = topi_ref[...]
    loc = lrank_ref[...]
    for e in range(N_EXPERTS):
        loc = loc + jnp.where(topi == e, loc0_ref[tile * N_EXPERTS + e], 0)
    loc_ref[...] = loc


def _slots(loc0, topi, lrank, t):
    k, n = topi.shape
    spec = pl.BlockSpec((k, t), lambda i, l0: (0, i))
    return pl.pallas_call(
        _slots_kernel,
        grid_spec=pltpu.PrefetchScalarGridSpec(num_scalar_prefetch=1, grid=(n // t,), in_specs=[spec, spec],
                                               out_specs=spec),
        out_shape=jax.ShapeDtypeStruct((k, n), jnp.int32),
        compiler_params=_cparams(("arbitrary",)),
        name="slots",
    )(loc0, topi, lrank)


def _moe_plan(cnt, n_slots, n_loc):
    c = cnt[:, :, 0].astype(jnp.int32)
    p = (c + SEG - 1) // SEG * SEG
    tot = jnp.sum(p, axis=0)
    padded = (tot + BM - 1) // BM * BM
    pad_end = jnp.cumsum(padded)
    pad_start = pad_end - padded
    seg = pad_start[None, :] + jnp.cumsum(p, axis=0) - p
    loc0 = jnp.cumsum(p, axis=1) - p
    n_blocks = n_slots // BM
    block_lo = jnp.arange(n_blocks, dtype=jnp.int32) * BM
    block_exp = jnp.minimum(jnp.sum((pad_end[None, :] <= block_lo[:, None]).astype(jnp.int32), axis=1),
                            N_EXPERTS - 1)
    experts = jnp.arange(N_EXPERTS, dtype=jnp.int32)

    def copy_table(count, offset, rows, width):
        end = jnp.cumsum(count, axis=1)
        at = jnp.arange(width, dtype=jnp.int32)
        owner = jnp.sum((end[:, :, None] <= at[None, None, :]).astype(jnp.int32), axis=1)
        mine = owner[:, None, :] == experts[None, :, None]
        pick = lambda a: jnp.sum(jnp.where(mine, a[:, :, None], 0), axis=1)
        step = (at[None, :] - pick(end - count)) * rows
        return ((pick(loc0 + offset) + step).reshape(-1), (pick(seg + offset) + step).reshape(-1), end[:, -1])

    copies = ()
    left = p
    for idx, rows in enumerate(COPY_ROWS):
        count = left // rows
        width = n_loc // rows if idx == 0 else N_EXPERTS * (COPY_ROWS[idx - 1] // rows - 1)
        copies += copy_table(count, p - left, rows, width)
        left = left - count * rows
    return {
        "copies": copies, "loc0": loc0.reshape(-1), "used": jnp.sum(p, axis=1),
        "tstart": jnp.concatenate([pad_start + tot, pad_end[-1:]]),
        "tlen": jnp.concatenate([padded - tot, n_slots - pad_end[-1:]]),
        "n_used": pad_end[-1:] // BM, "block_exp": block_exp,
    }


def _moe_layer(x, mod_l, layer, router_w, router_b, w_gu, b_gu, w_dn, b_dn, ln_g, ln_b):
    bsz, seq, d = x.shape
    n_tok = bsz * seq
    t = min(T_MOE, seq)
    n_slots = n_tok * TOP_K + (n_tok // t) * N_EXPERTS * SEG + N_EXPERTS * BM
    topi, gates, lrank, cnt = _router(x, mod_l, router_w, router_b)
    plan = _moe_plan(cnt, n_slots, TOP_K * t + N_EXPERTS * SEG)
    loc = _slots(plan["loc0"], topi, lrank, t)
    xs = _dispatch(x, mod_l, loc, plan, n_slots)
    ys = _experts(xs, plan["block_exp"], plan["n_used"], layer, w_gu, b_gu, w_dn, b_dn)
    return _combine(x, mod_l, loc.T, gates.T, plan, ys, ln_g, ln_b)


def kernel(x, c, positions, ada_w, ada_b, ln_mix_g, ln_mix_b, ln_ffn_g, ln_ffn_b, ev_w_in, ev_conv_w, ev_sg_w, ev_sg_b, ev_vn_g, ev_vn_b, ev_w_out, od_w_in, od_dw_w, od_dw_b, od_cn_g, od_cn_b, od_qn_g, od_w_uq, od_kvn_g, od_w_ukv, od_w_out, moe_router_w, moe_router_b, moe_w_gu, moe_b_gu, moe_w_dn, moe_b_dn):
    bsz, seq, d = x.shape
    depth = ada_w.shape[0]
    mod = _ada(c, ada_w, ada_b).reshape(depth, bsz, 6, d)
    cos_t, sin_t = _rope_tables(positions)
    for layer in range(depth):
        i = layer // 2
        mod_l = mod[layer]
        if layer % 2 == 0:
            x = _even_layer(x, mod_l, ev_w_in[i], ev_conv_w[i], ev_sg_w[i], ev_sg_b[i], ev_vn_g[i], ev_vn_b[i],
                            ev_w_out[i], ln_mix_g[layer], ln_mix_b[layer])
        else:
            y_c, q, k, v = _odd_pre(x, mod_l, cos_t, sin_t, od_w_in[i], od_dw_w[i], od_dw_b[i], od_cn_g[i],
                                    od_cn_b[i], od_qn_g[i], od_w_uq[i], od_kvn_g[i], od_w_ukv[i])
            y_d = _attention(q, k, v)
            x = _odd_post(x, mod_l, y_c, y_d, od_w_out[i], ln_mix_g[layer], ln_mix_b[layer])
        x = _moe_layer(x, mod_l, layer, moe_router_w[layer], moe_router_b[layer], moe_w_gu, moe_b_gu,
                       moe_w_dn, moe_b_dn, ln_ffn_g[layer], ln_ffn_b[layer])
    return x
```

```python
import functools
import math

import jax
import jax.numpy as jnp
from jax import lax
from jax.experimental import pallas as pl
from jax.experimental.pallas import tpu as pltpu

F32 = jnp.float32
BF16 = jnp.bfloat16
U32 = jnp.uint32
HIGHEST = lax.Precision.HIGHEST

D_MODEL = 1024
DEPTH = 4
W_A = 512
W_B = 512
SG_HEADS = 8
SG_HEAD_DIM = 64
CHUNK = 128
CONV_A = 3
W_C = 512
CONV_C = 31
MLA_HEADS = 8
QK_NOPE = 64
QK_ROPE = 32
V_DIM = 64
Q_RANK = 256
KV_RANK = 128
ROPE_THETA = 10000.0
N_EXPERTS = 32
TOP_K = 4
D_EXPERT = 1024
SWIGLU_LIMIT = 7.0
SWIGLU_ALPHA = 1.702
DEEPNORM_ALPHA = (2.0 * DEPTH) ** 0.25
LN_EPS = 1e-5
RMS_EPS = 1e-6

LANES = 128
SUBLANES = 8
HEAD_PAD = 128
CONV_C_HALO = 32
CONV_A_HALO = 8
CONV_ROWS = 128

T_EVEN = 1024
EVEN_SUB = 512
T_ODD = 256
T_POST = 512
T_ATT = 512
ATT_HEADS = 8
T_MOE = 512
BM = 512
SEG = 8
COPY_ROWS = (64, 32, 16, 8)
PERM_ROWS = 256
N_COPY_TABS = 3 * len(COPY_ROWS)
VMEM_LIMIT = 56 * 1024 * 1024


def _cparams(sem):
    return pltpu.CompilerParams(dimension_semantics=sem, vmem_limit_bytes=VMEM_LIMIT)


def _layer_norm(v, g, b):
    mu = jnp.mean(v, axis=-1, keepdims=True)
    d = v - mu
    var = jnp.mean(d * d, axis=-1, keepdims=True)
    return d * lax.rsqrt(var + LN_EPS) * g + b


def _gelu(v):
    return 0.5 * v * (1.0 + lax.erf(v * (1.0 / math.sqrt(2.0))))


def _sigmoid(v):
    return 1.0 / (1.0 + jnp.exp(-v))


HIGH_HALF = 0xFFFF0000


def _pack_pairs(v):
    n = v.shape[1] // 2
    bits = lax.bitcast_convert_type(v, U32)
    return (bits[:, :n] >> 16) | (bits[:, n:] & jnp.uint32(HIGH_HALF))


def _unpack_pairs(w):
    lo = lax.bitcast_convert_type(w << 16, F32).astype(BF16)
    hi = lax.bitcast_convert_type(w & jnp.uint32(HIGH_HALF), F32).astype(BF16)
    return lo, hi


def _ada_kernel(c_ref, w_ref, b_ref, o_ref):
    c = c_ref[...]
    cond = c * _sigmoid(c)
    o_ref[0] = jnp.dot(cond, w_ref[0], precision=HIGHEST, preferred_element_type=F32) + b_ref[0]


def _ada(c, ada_w, ada_b):
    bsz, d = c.shape
    depth = ada_w.shape[0]
    n_chunk = ada_w.shape[2] // d
    return pl.pallas_call(
        _ada_kernel,
        grid=(depth, n_chunk),
        in_specs=[
            pl.BlockSpec((bsz, d), lambda l, j: (0, 0)),
            pl.BlockSpec((1, d, d), lambda l, j: (l, 0, j)),
            pl.BlockSpec((1, 1, d), lambda l, j: (l, 0, j)),
        ],
        out_specs=pl.BlockSpec((1, bsz, d), lambda l, j: (l, 0, j)),
        out_shape=jax.ShapeDtypeStruct((depth, bsz, n_chunk * d), F32),
        compiler_params=_cparams(("arbitrary", "arbitrary")),
        name="ada",
    )(c, ada_w, ada_b.reshape(depth, 1, n_chunk * d))


def _rope_kernel(pos_ref, freq_ref, cos_ref, sin_ref):
    ang = pos_ref[...].astype(F32) * freq_ref[...]
    cos_ref[...] = jnp.cos(ang)
    sin_ref[...] = jnp.sin(ang)


def _rope_tables(positions):
    n = positions.size
    t = 1024
    inv_freq = ROPE_THETA ** (-jnp.arange(0, QK_ROPE, 2, dtype=F32) / QK_ROPE)
    half = QK_ROPE // 2
    freq = jnp.zeros((1, LANES), F32).at[0, QK_NOPE:QK_NOPE + QK_ROPE].set(jnp.tile(inv_freq, 2))
    del half
    return pl.pallas_call(
        _rope_kernel,
        grid=(n // t,),
        in_specs=[pl.BlockSpec((t, 1), lambda i: (i, 0)), pl.BlockSpec((1, LANES), lambda i: (0, 0))],
        out_specs=[pl.BlockSpec((t, LANES), lambda i: (i, 0))] * 2,
        out_shape=[jax.ShapeDtypeStruct((n, LANES), F32)] * 2,
        compiler_params=_cparams(("arbitrary",)),
        name="rope",
    )(positions.reshape(n, 1), freq)


def _even_kernel(x_ref, mod_ref, win_ref, cw_ref, sgw_ref, sgb_ref, vng_ref, vnb_ref, wout_ref,
                 lng_ref, lnb_ref, o_ref, gbuf, ybuf):
    t_len = x_ref.shape[1]
    halo = CONV_A_HALO

    @pl.when(pl.program_id(1) == 0)
    def _():
        gbuf[0:halo, :] = jnp.zeros((halo, W_A), F32)

    shift, scale, gate = mod_ref[0, 0:1, :], mod_ref[0, 1:2, :], mod_ref[0, 2:3, :]
    row = lax.broadcasted_iota(jnp.int32, (CHUNK, CHUNK), 0)
    col = lax.broadcasted_iota(jnp.int32, (CHUNK, CHUNK), 1)
    w_stack = jnp.concatenate(
        [jnp.where(row >= col, sgw_ref[h], 0.0).astype(BF16) for h in range(SG_HEADS)], axis=0)
    col_head = lax.broadcasted_iota(jnp.int32, (CHUNK, W_B), 1) // SG_HEAD_DIM

    sub = min(EVEN_SUB, t_len)
    g = None
    for r0 in range(0, t_len, sub):
        x = x_ref[0, r0:r0 + sub, :]
        u = x * (1.0 + scale) + shift
        proj = jnp.dot(u.astype(BF16), win_ref[...], preferred_element_type=F32)
        b_gate = proj[:, 0:W_A]
        c_gate = proj[:, W_A:2 * W_A]
        xa = proj[:, 2 * W_A:3 * W_A]
        zu = proj[:, 3 * W_A:3 * W_A + W_B]
        zv = proj[:, 3 * W_A + W_B:3 * W_A + 2 * W_B]

        g = c_gate * xa
        gbuf[halo + r0:halo + r0 + sub, :] = g
        conv = (cw_ref[0:1, :] * gbuf[halo + r0 - 2:halo + r0 - 2 + sub, :]
                + cw_ref[1:2, :] * gbuf[halo + r0 - 1:halo + r0 - 1 + sub, :]
                + cw_ref[2:3, :] * g)
        ybuf[r0:r0 + sub, 0:W_A] = (b_gate * conv).astype(BF16)

        zu = _gelu(zu)
        zv = _layer_norm(_gelu(zv), vng_ref[...], vnb_ref[...]).astype(BF16)
        for ci in range(sub // CHUNK):
            lo = ci * CHUNK
            full = jnp.dot(w_stack, zv[lo:lo + CHUNK, :], preferred_element_type=F32)
            mixed = sgb_ref[...]
            for h in range(SG_HEADS):
                mixed = mixed + jnp.where(col_head == h, full[h * CHUNK:(h + 1) * CHUNK, :], 0.0)
            ybuf[r0 + lo:r0 + lo + CHUNK, W_A:W_A + W_B] = (zu[lo:lo + CHUNK, :] * mixed).astype(BF16)

        y = jnp.dot(ybuf[r0:r0 + sub, :], wout_ref[...], preferred_element_type=F32)
        r = DEEPNORM_ALPHA * x + (1.0 + gate) * y
        o_ref[0, r0:r0 + sub, :] = _layer_norm(r, lng_ref[...], lnb_ref[...])
    gbuf[0:halo, :] = g[sub - halo:sub, :]


def _even_layer(x, mod_l, w_in, conv_w, sg_w, sg_b, vn_g, vn_b, w_out, ln_g, ln_b):
    bsz, seq, d = x.shape
    t = min(T_EVEN, seq)
    sgb_full = jnp.repeat(sg_b.T, SG_HEAD_DIM, axis=1)
    full = lambda shape: pl.BlockSpec(shape, lambda b, i: (0,) * len(shape))
    return pl.pallas_call(
        _even_kernel,
        grid=(bsz, seq // t),
        in_specs=[
            pl.BlockSpec((1, t, d), lambda b, i: (b, i, 0)),
            pl.BlockSpec((1, 6, d), lambda b, i: (b, 0, 0)),
            full(w_in.shape), full(conv_w.shape), full(sg_w.shape), full(sgb_full.shape),
            full((1, W_B)), full((1, W_B)), full(w_out.shape), full((1, d)), full((1, d)),
        ],
        out_specs=pl.BlockSpec((1, t, d), lambda b, i: (b, i, 0)),
        out_shape=jax.ShapeDtypeStruct((bsz, seq, d), F32),
        scratch_shapes=[pltpu.VMEM((CONV_A_HALO + t, W_A), F32), pltpu.VMEM((t, W_A + W_B), BF16)],
        compiler_params=_cparams(("arbitrary", "arbitrary")),
        name="even",
    )(x, mod_l, w_in.astype(BF16), conv_w, sg_w, sgb_full, vn_g.reshape(1, W_B), vn_b.reshape(1, W_B),
      w_out.astype(BF16), ln_g.reshape(1, d), ln_b.reshape(1, d))


N_GLU = 2 * W_C
OFF_Q = N_GLU
OFF_KV = OFF_Q + Q_RANK
OFF_KR = OFF_KV + KV_RANK
OFF_KR_ROT = OFF_KR + HEAD_PAD
ODD_IN_PAD = OFF_KR_ROT + HEAD_PAD


def _odd_pre_kernel(x_ref, mod_ref, cos_ref, sin_ref, win_ref, dww_ref, dwb_ref, cng_ref, cnb_ref,
                    qng_ref, wq_ref, wqr_ref, kvng_ref, wk_ref, wv_ref,
                    yc_ref, q_ref, k_ref, v_ref, gbuf, sbuf, hbuf):
    t_len = x_ref.shape[1]
    halo = CONV_C_HALO

    @pl.when(pl.program_id(1) == 0)
    def _():
        gbuf[0:halo, :] = jnp.zeros((halo, W_C), F32)

    x = x_ref[0]
    shift, scale = mod_ref[0, 0:1, :], mod_ref[0, 1:2, :]
    u = x * (1.0 + scale) + shift
    proj = jnp.dot(u.astype(BF16), win_ref[...], preferred_element_type=F32)

    g = proj[:, 0:W_C] * _sigmoid(proj[:, W_C:2 * W_C])
    gbuf[halo:halo + t_len, :] = g
    first = halo - (CONV_C - 1)
    for res in range(1, SUBLANES):
        span = max(first + k - res for k in range(CONV_C) if (first + k) % SUBLANES == res) + t_len
        sbuf[res - 1, 0:span, :] = gbuf[res:res + span, :]
    for rb in range(t_len // CONV_ROWS):
        for cb in range(W_C // LANES):
            cs = slice(cb * LANES, (cb + 1) * LANES)
            acc = jnp.broadcast_to(dwb_ref[:, cs], (CONV_ROWS, LANES))
            for k in range(CONV_C):
                res = (first + k) % SUBLANES
                lo = first + k - res + rb * CONV_ROWS
                tap = gbuf[lo:lo + CONV_ROWS, cs] if res == 0 else sbuf[res - 1, lo:lo + CONV_ROWS, cs]
                acc = acc + dww_ref[k:k + 1, cs] * tap
            hbuf[rb * CONV_ROWS:(rb + 1) * CONV_ROWS, cs] = acc
    acc = hbuf[...]
    gbuf[0:halo, :] = g[t_len - halo:t_len, :]
    hn = _layer_norm(acc, cng_ref[...], cnb_ref[...])
    yc_ref[0] = (hn * _sigmoid(hn)).astype(BF16)

    cos = cos_ref[...]
    sin = sin_ref[...]
    cq = proj[:, OFF_Q:OFF_Q + Q_RANK]
    q_lat = (cq * lax.rsqrt(jnp.mean(cq * cq, axis=-1, keepdims=True) + RMS_EPS) * qng_ref[...]).astype(BF16)
    ckv = proj[:, OFF_KV:OFF_KV + KV_RANK]
    kv_lat = (ckv * lax.rsqrt(jnp.mean(ckv * ckv, axis=-1, keepdims=True) + RMS_EPS) * kvng_ref[...]).astype(BF16)
    q_all = jnp.dot(q_lat, wq_ref[...], preferred_element_type=F32)
    q_rot = jnp.dot(q_lat, wqr_ref[...], preferred_element_type=F32)
    k_all = jnp.dot(kv_lat, wk_ref[...], preferred_element_type=F32)
    v_all = jnp.dot(kv_lat, wv_ref[...], preferred_element_type=F32)
    k_rope = proj[:, OFF_KR:OFF_KR + HEAD_PAD] * cos + proj[:, OFF_KR_ROT:OFF_KR_ROT + HEAD_PAD] * sin
    sm_scale = math.log2(math.e) / math.sqrt(QK_NOPE + QK_ROPE)
    for h in range(MLA_HEADS):
        sl = slice(h * HEAD_PAD, (h + 1) * HEAD_PAD)
        q_ref[0, h] = ((q_all[:, sl] * cos + q_rot[:, sl] * sin) * sm_scale).astype(BF16)
        k_ref[0, h] = (k_all[:, sl] + k_rope).astype(BF16)
    for hp in range(MLA_HEADS // 2):
        v_ref[0, hp] = v_all[:, hp * LANES:(hp + 1) * LANES].astype(BF16)


def _odd_weights(w_in, w_uq, w_ukv):
    d = w_in.shape[0]
    half = QK_ROPE // 2
    kr = w_in[:, OFF_KR:OFF_KR + QK_ROPE]
    z = lambda n: jnp.zeros((d, n), w_in.dtype)
    kr_blk = jnp.concatenate([z(QK_NOPE), kr, z(HEAD_PAD - QK_NOPE - QK_ROPE)], axis=1)
    kr_rot = jnp.concatenate([z(QK_NOPE), -kr[:, half:], kr[:, :half], z(HEAD_PAD - QK_NOPE - QK_ROPE)], axis=1)
    w_in_p = jnp.concatenate([w_in[:, :OFF_KR], kr_blk, kr_rot], axis=1).astype(BF16)

    dq = QK_NOPE + QK_ROPE
    wq = w_uq.reshape(Q_RANK, MLA_HEADS, dq)
    zq = lambda n: jnp.zeros((Q_RANK, MLA_HEADS, n), w_uq.dtype)
    wq_p = jnp.concatenate([wq, zq(HEAD_PAD - dq)], axis=2).reshape(Q_RANK, MLA_HEADS * HEAD_PAD)
    wq_r = jnp.concatenate([zq(QK_NOPE), -wq[:, :, QK_NOPE + half:], wq[:, :, QK_NOPE:QK_NOPE + half],
                            zq(HEAD_PAD - dq)], axis=2).reshape(Q_RANK, MLA_HEADS * HEAD_PAD)

    wkv = w_ukv.reshape(KV_RANK, MLA_HEADS, QK_NOPE + V_DIM)
    wk_p = jnp.concatenate([wkv[:, :, :QK_NOPE], jnp.zeros((KV_RANK, MLA_HEADS, HEAD_PAD - QK_NOPE), w_ukv.dtype)],
                           axis=2).reshape(KV_RANK, MLA_HEADS * HEAD_PAD)
    wv_p = wkv[:, :, QK_NOPE:].reshape(KV_RANK, MLA_HEADS * V_DIM)
    return w_in_p, wq_p.astype(BF16), wq_r.astype(BF16), wk_p.astype(BF16), wv_p.astype(BF16)


def _odd_pre(x, mod_l, cos_t, sin_t, w_in, dw_w, dw_b, cn_g, cn_b, qn_g, w_uq, kvn_g, w_ukv):
    bsz, seq, d = x.shape
    t = min(T_ODD, seq)
    nt = seq // t
    w_in_p, wq_p, wq_r, wk_p, wv_p = _odd_weights(w_in, w_uq, w_ukv)
    full = lambda shape: pl.BlockSpec(shape, lambda b, i: (0,) * len(shape))
    return pl.pallas_call(
        _odd_pre_kernel,
        grid=(bsz, nt),
        in_specs=[
            pl.BlockSpec((1, t, d), lambda b, i: (b, i, 0)),
            pl.BlockSpec((1, 6, d), lambda b, i: (b, 0, 0)),
            pl.BlockSpec((t, LANES), lambda b, i: (b * nt + i, 0)),
            pl.BlockSpec((t, LANES), lambda b, i: (b * nt + i, 0)),
            full(w_in_p.shape), full(dw_w.shape), full((1, W_C)), full((1, W_C)), full((1, W_C)),
            full((1, Q_RANK)), full(wq_p.shape), full(wq_r.shape), full((1, KV_RANK)),
            full(wk_p.shape), full(wv_p.shape),
        ],
        out_specs=[
            pl.BlockSpec((1, t, W_C), lambda b, i: (b, i, 0)),
            pl.BlockSpec((1, MLA_HEADS, t, HEAD_PAD), lambda b, i: (b, 0, i, 0)),
            pl.BlockSpec((1, MLA_HEADS, t, HEAD_PAD), lambda b, i: (b, 0, i, 0)),
            pl.BlockSpec((1, MLA_HEADS // 2, t, LANES), lambda b, i: (b, 0, i, 0)),
        ],
        out_shape=[
            jax.ShapeDtypeStruct((bsz, seq, W_C), BF16),
            jax.ShapeDtypeStruct((bsz, MLA_HEADS, seq, HEAD_PAD), BF16),
            jax.ShapeDtypeStruct((bsz, MLA_HEADS, seq, HEAD_PAD), BF16),
            jax.ShapeDtypeStruct((bsz, MLA_HEADS // 2, seq, LANES), BF16),
        ],
        scratch_shapes=[pltpu.VMEM((CONV_C_HALO + t, W_C), F32),
                        pltpu.VMEM((SUBLANES - 1, CONV_C_HALO + t, W_C), F32), pltpu.VMEM((t, W_C), F32)],
        compiler_params=_cparams(("arbitrary", "arbitrary")),
        name="odd_pre",
    )(x, mod_l, cos_t, sin_t, w_in_p, dw_w, dw_b.reshape(1, W_C), cn_g.reshape(1, W_C), cn_b.reshape(1, W_C),
      qn_g.reshape(1, Q_RANK), wq_p, wq_r, kvn_g.reshape(1, KV_RANK), wk_p, wv_p)


ATT_FULL, ATT_DIAG_LATE, ATT_DIAG_EARLY = 0, 1, 2


def _attn_kernel(kj_tab, kind_tab, q_ref, k_ref, v_ref, o_ref, m_sc, l_sc, acc_sc):
    step_id = pl.program_id(2)
    kind = kind_tab[step_id]
    tq = q_ref.shape[2]
    tk = k_ref.shape[2]

    @pl.when(kj_tab[step_id] == 0)
    def _():
        m_sc[...] = jnp.full(m_sc.shape, -jnp.inf, F32)
        l_sc[...] = jnp.zeros(l_sc.shape, F32)
        acc_sc[...] = jnp.zeros(acc_sc.shape, F32)

    first_lanes = lax.broadcasted_iota(jnp.int32, (tq, LANES), 1) < V_DIM
    n_pairs = v_ref.shape[1]

    def step(n_keys, row_shift):
        n_col = n_keys // LANES
        for pp in range(n_pairs):
            v_pair = v_ref[0, pp, 0:n_keys, :]
            alphas = []
            pvs = []
            for a in range(2 * pp, 2 * pp + 2):
                s = lax.dot_general(q_ref[0, a], k_ref[0, a, 0:n_keys, :], (((1,), (1,)), ((), ())),
                                    preferred_element_type=F32)
                if row_shift is not None:
                    rows = row_shift + lax.broadcasted_iota(jnp.int32, (tq, n_keys), 0)
                    cols = lax.broadcasted_iota(jnp.int32, (tq, n_keys), 1)
                    s = jnp.where(cols <= rows, s, -1e30)
                cols_s = [s[:, c * LANES:(c + 1) * LANES] for c in range(n_col)]
                m_lane = cols_s[0]
                for c in range(1, n_col):
                    m_lane = jnp.maximum(m_lane, cols_s[c])
                m_prev = m_sc[a]
                m_next = jnp.maximum(m_prev, jnp.max(m_lane, axis=1, keepdims=True))
                alpha = jnp.exp2(m_prev - m_next)
                ps = [jnp.exp2(cs - m_next) for cs in cols_s]
                l_lane = ps[0]
                for c in range(1, n_col):
                    l_lane = l_lane + ps[c]
                l_sc[a] = alpha * l_sc[a] + l_lane
                m_sc[a] = m_next
                alphas.append(alpha)
                p = jnp.concatenate([pc.astype(BF16) for pc in ps], axis=1)
                pvs.append(jnp.dot(p, v_pair, preferred_element_type=F32))
            alpha_sel = jnp.where(first_lanes, alphas[0], alphas[1])
            acc_sc[pp] = alpha_sel * acc_sc[pp] + jnp.where(first_lanes, pvs[0], pvs[1])

    @pl.when(kind == ATT_FULL)
    def _():
        step(tk, None)

    @pl.when(kind == ATT_DIAG_LATE)
    def _():
        step(tk, tq)

    @pl.when(kind == ATT_DIAG_EARLY)
    def _():
        step(tq, 0)

    @pl.when(kind != ATT_FULL)
    def _():
        for pp in range(n_pairs):
            l_sel = jnp.where(first_lanes, jnp.sum(l_sc[2 * pp], axis=1, keepdims=True),
                              jnp.sum(l_sc[2 * pp + 1], axis=1, keepdims=True))
            o_ref[0, :, pp * LANES:(pp + 1) * LANES] = (acc_sc[pp] / l_sel).astype(o_ref.dtype)


def _attention(q, k, v):
    bsz, heads, seq, _ = q.shape
    t = min(T_ATT, seq // 2)
    nq = seq // t
    steps = [(i, j, ATT_FULL if j < i // 2 else (ATT_DIAG_LATE if i % 2 else ATT_DIAG_EARLY))
             for i in range(nq) for j in range(i // 2 + 1)]
    qi_tab = jnp.asarray([st[0] for st in steps], jnp.int32)
    kj_tab = jnp.asarray([st[1] for st in steps], jnp.int32)
    kind_tab = jnp.asarray([st[2] for st in steps], jnp.int32)
    hs = ATT_HEADS
    grid_spec = pltpu.PrefetchScalarGridSpec(
        num_scalar_prefetch=3,
        grid=(bsz, heads // hs, len(steps)),
        in_specs=[
            pl.BlockSpec((1, hs, t, HEAD_PAD), lambda b, hg, s, qt, kt, kd: (b, hg, qt[s], 0)),
            pl.BlockSpec((1, hs, 2 * t, HEAD_PAD), lambda b, hg, s, qt, kt, kd: (b, hg, kt[s], 0)),
            pl.BlockSpec((1, hs // 2, 2 * t, LANES), lambda b, hg, s, qt, kt, kd: (b, hg, kt[s], 0)),
        ],
        out_specs=pl.BlockSpec((1, t, hs * V_DIM), lambda b, hg, s, qt, kt, kd: (b, qt[s], hg)),
        scratch_shapes=[pltpu.VMEM((hs, t, LANES), F32), pltpu.VMEM((hs, t, LANES), F32),
                        pltpu.VMEM((hs // 2, t, LANES), F32)],
    )
    return pl.pallas_call(
        _attn_kernel_entry,
        grid_spec=grid_spec,
        out_shape=jax.ShapeDtypeStruct((bsz, seq, heads * V_DIM), BF16),
        compiler_params=_cparams(("arbitrary", "arbitrary", "arbitrary")),
        name="attn",
    )(qi_tab, kj_tab, kind_tab, q, k, v)


def _attn_kernel_entry(qi_tab, kj_tab, kind_tab, *refs):
    del qi_tab
    _attn_kernel(kj_tab, kind_tab, *refs)


def _odd_post_kernel(x_ref, mod_ref, yc_ref, yd_ref, wout_ref, lng_ref, lnb_ref, o_ref):
    x = x_ref[0]
    gate = mod_ref[0, 2:3, :]
    y = (jnp.dot(yc_ref[0], wout_ref[0:W_C, :], preferred_element_type=F32)
         + jnp.dot(yd_ref[0], wout_ref[W_C:, :], preferred_element_type=F32))
    r = DEEPNORM_ALPHA * x + (1.0 + gate) * y
    o_ref[0] = _layer_norm(r, lng_ref[...], lnb_ref[...])


def _odd_post(x, mod_l, y_c, y_d, w_out, ln_g, ln_b):
    bsz, seq, d = x.shape
    t = min(T_POST, seq)
    full = lambda shape: pl.BlockSpec(shape, lambda b, i: (0,) * len(shape))
    return pl.pallas_call(
        _odd_post_kernel,
        grid=(bsz, seq // t),
        in_specs=[
            pl.BlockSpec((1, t, d), lambda b, i: (b, i, 0)),
            pl.BlockSpec((1, 6, d), lambda b, i: (b, 0, 0)),
            pl.BlockSpec((1, t, W_C), lambda b, i: (b, i, 0)),
            pl.BlockSpec((1, t, MLA_HEADS * V_DIM), lambda b, i: (b, i, 0)),
            full(w_out.shape), full((1, d)), full((1, d)),
        ],
        out_specs=pl.BlockSpec((1, t, d), lambda b, i: (b, i, 0)),
        out_shape=jax.ShapeDtypeStruct((bsz, seq, d), F32),
        compiler_params=_cparams(("arbitrary", "arbitrary")),
        name="odd_post",
    )(x, mod_l, y_c, y_d, w_out.astype(BF16), ln_g.reshape(1, d), ln_b.reshape(1, d))


def _router_kernel(x_ref, mod_ref, rw_ref, rb_ref, topi_ref, gate_ref, lrank_ref, cnt_ref):
    t_len = x_ref.shape[1]
    x = x_ref[0]
    shift, scale = mod_ref[0, 3:4, :], mod_ref[0, 4:5, :]
    u = x * (1.0 + scale) + shift
    logits = lax.dot_general(rw_ref[...], u, (((1,), (1,)), ((), ())), precision=HIGHEST,
                             preferred_element_type=F32) + rb_ref[...]
    eidx = lax.broadcasted_iota(jnp.int32, (N_EXPERTS, t_len), 0)
    vals, hots = [], []
    for _ in range(TOP_K):
        m = jnp.max(logits, axis=0, keepdims=True)
        idx = jnp.min(jnp.where(logits == m, eidx, N_EXPERTS), axis=0, keepdims=True)
        hot = eidx == idx
        vals.append(m)
        hots.append(hot)
        logits = jnp.where(hot, -jnp.inf, logits)
        topi_ref[len(vals) - 1:len(vals), :] = idx
    exps = [jnp.exp(v - vals[0]) for v in vals]
    denom = exps[0] + exps[1] + exps[2] + exps[3]
    for kk in range(TOP_K):
        gate_ref[kk:kk + 1, :] = exps[kk] / denom

    chosen = jnp.where(hots[0] | hots[1] | hots[2] | hots[3], 1.0, 0.0)
    r_i = lax.broadcasted_iota(jnp.int32, (t_len, t_len), 0)
    c_i = lax.broadcasted_iota(jnp.int32, (t_len, t_len), 1)
    before = jnp.where(r_i < c_i, 1.0, 0.0).astype(BF16)
    base = jnp.dot(chosen.astype(BF16), before, preferred_element_type=F32)
    for kk in range(TOP_K):
        lrank_ref[kk:kk + 1, :] = jnp.sum(jnp.where(hots[kk], base, 0.0), axis=0, keepdims=True).astype(jnp.int32)
    cnt_ref[0] = jnp.broadcast_to(jnp.sum(chosen, axis=1, keepdims=True), cnt_ref.shape[1:])


def _router(x, mod_l, router_w, router_b):
    bsz, seq, d = x.shape
    t = min(T_MOE, seq)
    nt = seq // t
    n_tok = bsz * seq
    full = lambda shape: pl.BlockSpec(shape, lambda b, i: (0,) * len(shape))
    tok_spec = pl.BlockSpec((TOP_K, t), lambda b, i: (0, b * nt + i))
    return pl.pallas_call(
        _router_kernel,
        grid=(bsz, nt),
        in_specs=[
            pl.BlockSpec((1, t, d), lambda b, i: (b, i, 0)),
            pl.BlockSpec((1, 6, d), lambda b, i: (b, 0, 0)),
            full((N_EXPERTS, d)), full((N_EXPERTS, 1)),
        ],
        out_specs=[tok_spec, tok_spec, tok_spec,
                   pl.BlockSpec((1, N_EXPERTS, LANES), lambda b, i: (b * nt + i, 0, 0))],
        out_shape=[
            jax.ShapeDtypeStruct((TOP_K, n_tok), jnp.int32),
            jax.ShapeDtypeStruct((TOP_K, n_tok), F32),
            jax.ShapeDtypeStruct((TOP_K, n_tok), jnp.int32),
            jax.ShapeDtypeStruct((bsz * nt, N_EXPERTS, LANES), F32),
        ],
        compiler_params=_cparams(("arbitrary", "arbitrary")),
        name="router",
    )(x, mod_l, router_w.T, router_b.reshape(N_EXPERTS, 1))


def _tile_index():
    return pl.program_id(0) * pl.num_programs(1) + pl.program_id(1)


def _for_each_copy(tile, tabs, fn):
    for idx, rows in enumerate(COPY_ROWS):
        src_ref, dst_ref, n_ref = tabs[3 * idx:3 * idx + 3]
        width = src_ref.shape[0] // n_ref.shape[0]

        def one(cc, carry, rows=rows, src_ref=src_ref, dst_ref=dst_ref, width=width):
            at = tile * width + cc
            fn(rows, pl.multiple_of(src_ref[at], SEG), pl.multiple_of(dst_ref[at], SEG))
            return carry

        lax.fori_loop(0, n_ref[tile], one, 0)


def _dispatch_kernel(*refs):
    tabs = refs[:N_COPY_TABS]
    tstart_ref, tlen_ref, x_ref, mod_ref, loc_ref, xs_ref, pm_buf, lbuf, zbuf, sem = refs[N_COPY_TABS:]
    _dispatch_body(tabs, tstart_ref, tlen_ref, x_ref, mod_ref, loc_ref, xs_ref, pm_buf, lbuf, zbuf, sem)


def _dispatch_body(tabs, tstart_ref, tlen_ref, x_ref, mod_ref, loc_ref, xs_ref, pm_buf, lbuf, zbuf, sem):
    tile = _tile_index()
    n_tiles = pl.num_programs(0) * pl.num_programs(1)
    t_len = x_ref.shape[1]
    n_loc = lbuf.shape[1]
    x = x_ref[0]
    shift, scale = mod_ref[0, 3:4, :], mod_ref[0, 4:5, :]
    u = (x * (1.0 + scale) + shift).astype(BF16)
    loc = loc_ref[...]
    for jc in range(n_loc // PERM_ROWS):
        j = jc * PERM_ROWS + lax.broadcasted_iota(jnp.int32, (PERM_ROWS, t_len), 0)
        pm = jnp.zeros((PERM_ROWS, t_len), F32)
        for kk in range(TOP_K):
            pm = jnp.where(j == loc[kk:kk + 1, :], 1.0, pm)
        pm_buf[jc * PERM_ROWS:(jc + 1) * PERM_ROWS, :] = pm.astype(BF16)
    slot = tile % 2
    lbuf[slot] = _pack_pairs(jnp.dot(pm_buf[...], u, preferred_element_type=F32))

    def seg_copy(s, rows, src_row, dst_row):
        return pltpu.make_async_copy(lbuf.at[s, pl.ds(src_row, rows)], xs_ref.at[pl.ds(dst_row, rows)], sem)

    def drain(t):
        _for_each_copy(t, tabs, lambda rows, sr, dr: seg_copy(0, rows, 0, 0).wait())

    @pl.when(tile > 0)
    def _():
        drain(tile - 1)

    _for_each_copy(tile, tabs, lambda rows, sr, dr: seg_copy(slot, rows, sr, dr).start())

    @pl.when(tile == n_tiles - 1)
    def _():
        drain(tile)
        zbuf[...] = jnp.zeros(zbuf.shape, U32)

        def fill(e, rows, done_rows=0):
            done = tlen_ref[e] // done_rows * done_rows if done_rows else 0

            def zero_copy(cc):
                dst = pl.multiple_of(tstart_ref[e] + done + cc * rows, SEG)
                return pltpu.make_async_copy(zbuf.at[pl.ds(0, rows)], xs_ref.at[pl.ds(dst, rows)], sem)

            n = (tlen_ref[e] - done) // rows

            def start(cc, c2):
                zero_copy(cc).start()
                return c2

            def wait(cc, c2):
                zero_copy(cc).wait()
                return c2

            lax.fori_loop(0, n, start, 0)
            lax.fori_loop(0, n, wait, 0)

        def per_expert(e, carry):
            fill(e, COPY_ROWS[0])
            fill(e, SEG, COPY_ROWS[0])
            return carry

        lax.fori_loop(0, N_EXPERTS, per_expert, 0)
        fill(N_EXPERTS, BM)


def _dispatch(x, mod_l, loc, plan, n_slots):
    bsz, seq, d = x.shape
    t = min(T_MOE, seq)
    nt = seq // t
    n_loc = TOP_K * t + N_EXPERTS * SEG
    grid_spec = pltpu.PrefetchScalarGridSpec(
        num_scalar_prefetch=N_COPY_TABS + 2,
        grid=(bsz, nt),
        in_specs=[
            pl.BlockSpec((1, t, d), lambda b, i, *_: (b, i, 0)),
            pl.BlockSpec((1, 6, d), lambda b, i, *_: (b, 0, 0)),
            pl.BlockSpec((TOP_K, t), lambda b, i, *_: (0, b * nt + i)),
        ],
        out_specs=pl.BlockSpec(memory_space=pl.ANY),
        scratch_shapes=[pltpu.VMEM((n_loc, t), BF16), pltpu.VMEM((2, n_loc, d // 2), U32),
                        pltpu.VMEM((BM, d // 2), U32), pltpu.SemaphoreType.DMA(())],
    )
    return pl.pallas_call(
        _dispatch_kernel,
        grid_spec=grid_spec,
        out_shape=jax.ShapeDtypeStruct((n_slots, d // 2), U32),
        compiler_params=_cparams(("arbitrary", "arbitrary")),
        name="dispatch",
    )(*plan["copies"], plan["tstart"], plan["tlen"], x, mod_l, loc)


def _expert_kernel(bexp_ref, nused_ref, xs_ref, wgu_ref, bgu_ref, wdn_ref, bdn_ref, ys_ref, wgu_bf, wdn_bf):
    s = pl.program_id(0)
    n_blocks = pl.num_programs(0) - 1
    active = (s > 0) & (s - 1 < nused_ref[0])

    @pl.when(active)
    def _():
        x = jnp.concatenate(_unpack_pairs(xs_ref[...]), axis=1)
        gu = jnp.dot(x, wgu_bf[...], preferred_element_type=F32) + bgu_ref[0, 0]
        gate = jnp.minimum(gu[:, :D_EXPERT], SWIGLU_LIMIT)
        up = jnp.clip(gu[:, D_EXPERT:], -SWIGLU_LIMIT, SWIGLU_LIMIT)
        glu = gate * _sigmoid(SWIGLU_ALPHA * gate)
        h = ((up + 1.0) * glu).astype(BF16)
        y = jnp.dot(h, wdn_bf[...], preferred_element_type=F32) + bdn_ref[0, 0]
        ys_ref[...] = _pack_pairs(y.astype(BF16).astype(F32))

    @pl.when((s > 0) & jnp.logical_not(active))
    def _():
        ys_ref[...] = jnp.zeros(ys_ref.shape, U32)

    ahead = jnp.minimum(s, n_blocks - 1)
    new_expert = (s == 0) | (bexp_ref[ahead] != bexp_ref[jnp.maximum(s - 1, 0)])

    @pl.when((s < nused_ref[0]) & new_expert)
    def _():
        wgu_bf[...] = wgu_ref[0, 0].astype(BF16)
        wdn_bf[...] = wdn_ref[0, 0].astype(BF16)


def _experts(xs, block_exp, n_used, layer, w_gu, b_gu, w_dn, b_dn):
    n_slots = xs.shape[0]
    d = w_dn.shape[-1]
    n_blocks = n_slots // BM
    depth = w_gu.shape[0]
    row_map = lambda s, be, nu: (jnp.maximum(jnp.minimum(s - 1, nu[0] - 1), 0), 0)
    bias_map = lambda s, be, nu: (layer, be[jnp.maximum(s - 1, 0)], 0, 0)
    ahead_map = lambda s, be, nu: (layer, be[jnp.minimum(s, n_blocks - 1)], 0, 0)
    grid_spec = pltpu.PrefetchScalarGridSpec(
        num_scalar_prefetch=2,
        grid=(n_blocks + 1,),
        in_specs=[
            pl.BlockSpec((BM, d // 2), row_map),
            pl.BlockSpec((1, 1, d, 2 * D_EXPERT), ahead_map),
            pl.BlockSpec((1, 1, 1, 2 * D_EXPERT), bias_map),
            pl.BlockSpec((1, 1, D_EXPERT, d), ahead_map),
            pl.BlockSpec((1, 1, 1, d), bias_map),
        ],
        out_specs=pl.BlockSpec((BM, d // 2), lambda s, be, nu: (jnp.maximum(s - 1, 0), 0)),
        scratch_shapes=[pltpu.VMEM((d, 2 * D_EXPERT), BF16), pltpu.VMEM((D_EXPERT, d), BF16)],
    )
    return pl.pallas_call(
        _expert_kernel,
        grid_spec=grid_spec,
        out_shape=jax.ShapeDtypeStruct((n_slots, d // 2), U32),
        compiler_params=_cparams(("arbitrary",)),
        name="experts",
    )(block_exp, n_used, xs, w_gu, b_gu.reshape(depth, N_EXPERTS, 1, 2 * D_EXPERT), w_dn,
      b_dn.reshape(depth, N_EXPERTS, 1, d))


def _combine_kernel(*refs):
    _combine_body(refs[:N_COPY_TABS], *refs[N_COPY_TABS:])


def _combine_body(tabs, used_ref, x_ref, mod_ref, loct_ref, gatet_ref, ys_ref,
                  lng_ref, lnb_ref, o_ref, ybuf, sel_buf, sem):
    tile = _tile_index()
    n_tiles = pl.num_programs(0) * pl.num_programs(1)
    t_len = x_ref.shape[1]
    n_loc = ybuf.shape[1]
    slot = tile % 2

    def seg_copy(s, rows, src_row, dst_row):
        return pltpu.make_async_copy(ys_ref.at[pl.ds(dst_row, rows)], ybuf.at[s, pl.ds(src_row, rows)], sem.at[s])

    def fetch(t, s):
        used_t = used_ref[t]
        _for_each_copy(t, tabs, lambda rows, sr, dr: seg_copy(s, rows, sr, dr).start())

        def zero_rows(cc, c2):
            ybuf[s, pl.ds(pl.multiple_of(used_t + cc * SEG, SEG), SEG), :] = jnp.zeros((SEG, ybuf.shape[2]), U32)
            return c2

        lax.fori_loop(0, (n_loc - used_t) // SEG, zero_rows, 0)

    @pl.when(tile == 0)
    def _():
        fetch(0, 0)

    @pl.when(tile + 1 < n_tiles)
    def _():
        fetch(tile + 1, 1 - slot)

    loct = loct_ref[...]
    gatet = gatet_ref[...]
    for jc in range(n_loc // LANES):
        j = jc * LANES + lax.broadcasted_iota(jnp.int32, (t_len, LANES), 1)
        sel = jnp.zeros((t_len, LANES), F32)
        for kk in range(TOP_K):
            sel = jnp.where(j == loct[:, kk:kk + 1], gatet[:, kk:kk + 1], sel)
        sel_buf[:, jc * LANES:(jc + 1) * LANES] = sel.astype(BF16)

    _for_each_copy(tile, tabs, lambda rows, sr, dr: seg_copy(slot, rows, 0, 0).wait())

    y_lo, y_hi = _unpack_pairs(ybuf[slot])
    sel = sel_buf[...]
    y = jnp.concatenate([jnp.dot(sel, y_lo, preferred_element_type=F32),
                         jnp.dot(sel, y_hi, preferred_element_type=F32)], axis=1)
    x = x_ref[0]
    gate_f = mod_ref[0, 5:6, :]
    r = DEEPNORM_ALPHA * x + (1.0 + gate_f) * y
    o_ref[0] = _layer_norm(r, lng_ref[...], lnb_ref[...])


def _combine(x, mod_l, loc_t, gates_t, plan, ys, ln_g, ln_b):
    bsz, seq, d = x.shape
    t = min(T_MOE, seq)
    nt = seq // t
    n_loc = TOP_K * t + N_EXPERTS * SEG
    full = lambda shape: pl.BlockSpec(shape, lambda b, i, *_: (0,) * len(shape))
    grid_spec = pltpu.PrefetchScalarGridSpec(
        num_scalar_prefetch=N_COPY_TABS + 1,
        grid=(bsz, nt),
        in_specs=[
            pl.BlockSpec((1, t, d), lambda b, i, *_: (b, i, 0)),
            pl.BlockSpec((1, 6, d), lambda b, i, *_: (b, 0, 0)),
            pl.BlockSpec((t, TOP_K), lambda b, i, *_: (b * nt + i, 0)),
            pl.BlockSpec((t, TOP_K), lambda b, i, *_: (b * nt + i, 0)),
            pl.BlockSpec(memory_space=pl.ANY),
            full((1, d)), full((1, d)),
        ],
        out_specs=pl.BlockSpec((1, t, d), lambda b, i, *_: (b, i, 0)),
        scratch_shapes=[pltpu.VMEM((2, n_loc, d // 2), U32), pltpu.VMEM((t, n_loc), BF16),
                        pltpu.SemaphoreType.DMA((2,))],
    )
    return pl.pallas_call(
        _combine_kernel,
        grid_spec=grid_spec,
        out_shape=jax.ShapeDtypeStruct((bsz, seq, d), F32),
        compiler_params=_cparams(("arbitrary", "arbitrary")),
        name="combine",
    )(*plan["copies"], plan["used"], x, mod_l, loc_t, gates_t, ys,
      ln_g.reshape(1, d), ln_b.reshape(1, d))


def _slots_kernel(loc0_ref, topi_ref, lrank_ref, loc_ref):
    tile = pl.program_id(0)
    topi = topi_ref[...]
    loc = lrank_ref[...]
    for e in range(N_EXPERTS):
        loc = loc + jnp.where(topi == e, loc0_ref[tile * N_EXPERTS + e], 0)
    loc_ref[...] = loc


def _slots(loc0, topi, lrank, t):
    k, n = topi.shape
    spec = pl.BlockSpec((k, t), lambda i, l0: (0, i))
    return pl.pallas_call(
        _slots_kernel,
        grid_spec=pltpu.PrefetchScalarGridSpec(num_scalar_prefetch=1, grid=(n // t,), in_specs=[spec, spec],
                                               out_specs=spec),
        out_shape=jax.ShapeDtypeStruct((k, n), jnp.int32),
        compiler_params=_cparams(("arbitrary",)),
        name="slots",
    )(loc0, topi, lrank)


def _moe_plan(cnt, n_slots, n_loc):
    c = cnt[:, :, 0].astype(jnp.int32)
    p = (c + SEG - 1) // SEG * SEG
    tot = jnp.sum(p, axis=0)
    padded = (tot + BM - 1) // BM * BM
    pad_end = jnp.cumsum(padded)
    pad_start = pad_end - padded
    seg = pad_start[None, :] + jnp.cumsum(p, axis=0) - p
    loc0 = jnp.cumsum(p, axis=1) - p
    n_blocks = n_slots // BM
    block_lo = jnp.arange(n_blocks, dtype=jnp.int32) * BM
    block_exp = jnp.minimum(jnp.sum((pad_end[None, :] <= block_lo[:, None]).astype(jnp.int32), axis=1),
                            N_EXPERTS - 1)
    experts = jnp.arange(N_EXPERTS, dtype=jnp.int32)

    def copy_table(count, offset, rows, width):
        end = jnp.cumsum(count, axis=1)
        at = jnp.arange(width, dtype=jnp.int32)
        owner = jnp.sum((end[:, :, None] <= at[None, None, :]).astype(jnp.int32), axis=1)
        mine = owner[:, None, :] == experts[None, :, None]
        pick = lambda a: jnp.sum(jnp.where(mine, a[:, :, None], 0), axis=1)
        step = (at[None, :] - pick(end - count)) * rows
        return ((pick(loc0 + offset) + step).reshape(-1), (pick(seg + offset) + step).reshape(-1), end[:, -1])

    copies = ()
    left = p
    for idx, rows in enumerate(COPY_ROWS):
        count = left // rows
        width = n_loc // rows if idx == 0 else N_EXPERTS * (COPY_ROWS[idx - 1] // rows - 1)
        copies += copy_table(count, p - left, rows, width)
        left = left - count * rows
    return {
        "copies": copies, "loc0": loc0.reshape(-1), "used": jnp.sum(p, axis=1),
        "tstart": jnp.concatenate([pad_start + tot, pad_end[-1:]]),
        "tlen": jnp.concatenate([padded - tot, n_slots - pad_end[-1:]]),
        "n_used": pad_end[-1:] // BM, "block_exp": block_exp,
    }


def _moe_layer(x, mod_l, layer, router_w, router_b, w_gu, b_gu, w_dn, b_dn, ln_g, ln_b):
    bsz, seq, d = x.shape
    n_tok = bsz * seq
    t = min(T_MOE, seq)
    n_slots = n_tok * TOP_K + (n_tok // t) * N_EXPERTS * SEG + N_EXPERTS * BM
    topi, gates, lrank, cnt = _router(x, mod_l, router_w, router_b)
    plan = _moe_plan(cnt, n_slots, TOP_K * t + N_EXPERTS * SEG)
    loc = _slots(plan["loc0"], topi, lrank, t)
    xs = _dispatch(x, mod_l, loc, plan, n_slots)
    ys = _experts(xs, plan["block_exp"], plan["n_used"], layer, w_gu, b_gu, w_dn, b_dn)
    return _combine(x, mod_l, loc.T, gates.T, plan, ys, ln_g, ln_b)


def kernel(x, c, positions, ada_w, ada_b, ln_mix_g, ln_mix_b, ln_ffn_g, ln_ffn_b, ev_w_in, ev_conv_w, ev_sg_w, ev_sg_b, ev_vn_g, ev_vn_b, ev_w_out, od_w_in, od_dw_w, od_dw_b, od_cn_g, od_cn_b, od_qn_g, od_w_uq, od_kvn_g, od_w_ukv, od_w_out, moe_router_w, moe_router_b, moe_w_gu, moe_b_gu, moe_w_dn, moe_b_dn):
    bsz, seq, d = x.shape
    depth = ada_w.shape[0]
    mod = _ada(c, ada_w, ada_b).reshape(depth, bsz, 6, d)
    cos_t, sin_t = _rope_tables(positions)
    for layer in range(depth):
        i = layer // 2
        mod_l = mod[layer]
        if layer % 2 == 0:
            x = _even_layer(x, mod_l, ev_w_in[i], ev_conv_w[i], ev_sg_w[i], ev_sg_b[i], ev_vn_g[i], ev_vn_b[i],
                            ev_w_out[i], ln_mix_g[layer], ln_mix_b[layer])
        else:
            y_c, q, k, v = _odd_pre(x, mod_l, cos_t, sin_t, od_w_in[i], od_dw_w[i], od_dw_b[i], od_cn_g[i],
                                    od_cn_b[i], od_qn_g[i], od_w_uq[i], od_kvn_g[i], od_w_ukv[i])
            y_d = _attention(q, k, v)
            x = _odd_post(x, mod_l, y_c, y_d, od_w_out[i], ln_mix_g[layer], ln_mix_b[layer])
        x = _moe_layer(x, mod_l, layer, moe_router_w[layer], moe_router_b[layer], moe_w_gu, moe_b_gu,
                       moe_w_dn, moe_b_dn, ln_ffn_g[layer], ln_ffn_b[layer])
    return x
```
